```python
import jax, jax.numpy as jnp
from jax import lax
import numpy as np


D_MODEL = 1024
BATCH = 1
SEQ = 16384
DEPTH = 2

HEAD_DIM = 64
A_Q_HEADS = 8
A_KV_HEADS = 2
A_GROUP = A_Q_HEADS // A_KV_HEADS
A_WINDOW = 128
A_BLOCK = 128
B_HEADS = 8
GRID_W = 64
NA_KH = 8
NA_KW = 16
RET_HEADS = 8
RET_DK = D_MODEL // RET_HEADS
RET_DV = 2 * D_MODEL // RET_HEADS
RET_CHUNK = 128
D_FF = 4 * D_MODEL
N_EVEN = (DEPTH + 1) // 2
N_ODD = DEPTH // 2
EVEN_SPLITS = (A_Q_HEADS * HEAD_DIM, A_KV_HEADS * HEAD_DIM, A_KV_HEADS * HEAD_DIM,
               B_HEADS * HEAD_DIM, B_HEADS * HEAD_DIM, B_HEADS * HEAD_DIM)
EVEN_IN = sum(EVEN_SPLITS)
EVEN_MIX = (A_Q_HEADS + B_HEADS) * HEAD_DIM
ODD_SPLITS = (RET_HEADS * RET_DK, RET_HEADS * RET_DK, RET_HEADS * RET_DV, RET_HEADS * RET_DV)
ODD_IN = sum(ODD_SPLITS)
ODD_MIX = RET_HEADS * RET_DV
RMS_EPS = 1e-6
GN_EPS = 1e-5

kernel_name = 'hybrid_swa_natten_retention_encoder'


def _split(t, sizes):
    idx = np.cumsum(sizes)[:-1].tolist()
    return jnp.split(t, idx, axis=-1)


def _rmsnorm(x, g):
    xf = x.astype(jnp.float32)
    y = xf * lax.rsqrt(jnp.mean(xf * xf, axis=-1, keepdims=True) + RMS_EPS)
    return (y * g.astype(jnp.float32)).astype(x.dtype)


def _alibi_slopes(n):
    return 2.0 ** (-(8.0 / n) * (jnp.arange(n, dtype=jnp.float32) + 1.0))


def _window_gqa(q, k, v, sink):
    B, S = q.shape[0], q.shape[1]
    nb = S // A_BLOCK
    qb = q.reshape(B, nb, A_BLOCK, A_KV_HEADS, A_GROUP, HEAD_DIM)
    pad = ((0, 0), (A_BLOCK, A_BLOCK), (0, 0), (0, 0))
    kp = jnp.pad(k, pad).reshape(B, nb + 2, A_BLOCK, A_KV_HEADS, HEAD_DIM)
    vp = jnp.pad(v, pad).reshape(B, nb + 2, A_BLOCK, A_KV_HEADS, HEAD_DIM)
    kb = jnp.concatenate([kp[:, :-2], kp[:, 1:-1], kp[:, 2:]], axis=2)
    vb = jnp.concatenate([vp[:, :-2], vp[:, 1:-1], vp[:, 2:]], axis=2)
    s = jnp.einsum('bnqhgd,bnkhd->bnhgqk', qb, kb).astype(jnp.float32) * (HEAD_DIM ** -0.5)
    qi = jnp.arange(A_BLOCK)[:, None]
    kj = jnp.arange(3 * A_BLOCK)[None, :]
    rel = kj - A_BLOCK - qi
    kpos = jnp.arange(nb)[:, None, None] * A_BLOCK + kj[None] - A_BLOCK
    valid = (jnp.abs(rel) <= A_WINDOW)[None] & (kpos >= 0) & (kpos < S)
    slopes = _alibi_slopes(A_Q_HEADS).reshape(A_KV_HEADS, A_GROUP, 1, 1)
    s = s - slopes * jnp.abs(rel).astype(jnp.float32)
    s = jnp.where(valid[None, :, None, None], s, -jnp.inf)
    sk = sink.astype(jnp.float32).reshape(A_KV_HEADS, A_GROUP)[None, None, :, :, None, None]
    m = jnp.maximum(jnp.max(s, axis=-1, keepdims=True), sk)
    p = jnp.exp(s - m)
    p = p / (jnp.sum(p, axis=-1, keepdims=True) + jnp.exp(sk - m))
    o = jnp.einsum('bnhgqk,bnkhd->bnqhgd', p.astype(v.dtype), vb)
    return o.reshape(B, S, A_Q_HEADS * HEAD_DIM)


def _neighbourhood_attn(q, k, v, rpb):
    B, S = q.shape[0], q.shape[1]
    rows = S // GRID_W
    kh = min(NA_KH, rows)
    kw = NA_KW
    qg = q.reshape(B, rows, GRID_W, B_HEADS, HEAD_DIM)
    kg = k.reshape(B, rows, GRID_W, B_HEADS, HEAD_DIM)
    vg = v.reshape(B, rows, GRID_W, B_HEADS, HEAD_DIM)
    qc = jnp.arange(GRID_W)
    cstart = jnp.clip(qc - kw // 2, 0, GRID_W - kw)
    kc = jnp.arange(GRID_W)
    col_valid = (kc[None, :] >= cstart[:, None]) & (kc[None, :] < cstart[:, None] + kw)
    col_idx = jnp.clip(kc[None, :] - qc[:, None] + NA_KW - 1, 0, 2 * NA_KW - 2)
    rpb_f = rpb.astype(jnp.float32)
    scale = HEAD_DIM ** -0.5

    def row_fn(r):
        rstart = jnp.clip(r - kh // 2, 0, rows - kh)
        qr = lax.dynamic_index_in_dim(qg, r, axis=1, keepdims=False)
        kr = lax.dynamic_slice_in_dim(kg, rstart, kh, axis=1)
        vr = lax.dynamic_slice_in_dim(vg, rstart, kh, axis=1)
        s = jnp.einsum('bqhd,brkhd->bhqrk', qr, kr).astype(jnp.float32) * scale
        row_idx = rstart + jnp.arange(kh) - r + NA_KH - 1
        bias = rpb_f[:, row_idx[None, :, None], col_idx[:, None, :]]
        s = jnp.where(col_valid[:, None, :], s + bias, -jnp.inf)
        p = jax.nn.softmax(s.reshape(B, B_HEADS, GRID_W, kh * GRID_W), axis=-1)
        p = p.reshape(B, B_HEADS, GRID_W, kh, GRID_W).astype(v.dtype)
        return jnp.einsum('bhqrk,brkhd->bqhd', p, vr)

    out = lax.map(row_fn, jnp.arange(rows))
    return out.transpose(1, 0, 2, 3, 4).reshape(B, S, B_HEADS * HEAD_DIM)


def _retention_dir(q, k, v, log_gamma, strict):
    B, S, H, dk = q.shape
    dv = v.shape[-1]
    nc = S // RET_CHUNK
    dt = q.dtype

    def chunks(t):
        return t.reshape(B, nc, RET_CHUNK, H, t.shape[-1]).transpose(1, 0, 3, 2, 4)

    lg = log_gamma[:, None]
    pos = jnp.arange(RET_CHUNK, dtype=jnp.float32)
    diff = pos[:, None] - pos[None, :]
    allowed = (diff > 0) if strict else (diff >= 0)
    dmask = jnp.where(allowed[None], jnp.exp(lg[:, :, None] * jnp.maximum(diff, 0.0)[None]), 0.0).astype(dt)
    cross = jnp.exp(lg * (pos + 1.0)).astype(dt)
    zeta = jnp.exp(lg * (RET_CHUNK - 1.0 - pos)).astype(dt)
    chunk_decay = jnp.exp(log_gamma * RET_CHUNK).astype(dt)

    def step(R, inp):
        qc, kc, vc = inp
        a = jnp.einsum('bhid,bhjd->bhij', qc, kc) * dmask
        o = jnp.einsum('bhij,bhjv->bhiv', a, vc) + jnp.einsum('bhid,bhdv->bhiv', qc, R) * cross[:, :, None]
        R = R * chunk_decay[:, None, None] + jnp.einsum('bhjd,bhjv->bhdv', kc * zeta[:, :, None], vc)
        return R, o

    R0 = jnp.zeros((B, H, dk, dv), dt)
    _, o = lax.scan(step, R0, (chunks(q), chunks(k), chunks(v)))
    return o.transpose(1, 0, 3, 2, 4).reshape(B, S, H, dv)


def _even_mixer(x, norm_g, w_in, qn_a, kn_a, sink_a, qn_b, kn_b, rpb_b, w_out):
    B, S, _ = x.shape
    h = _rmsnorm(x, norm_g)
    qa, ka, va, qb, kb, vb = _split(h @ w_in, EVEN_SPLITS)
    qa = _rmsnorm(qa.reshape(B, S, A_KV_HEADS, A_GROUP, HEAD_DIM), qn_a)
    ka = _rmsnorm(ka.reshape(B, S, A_KV_HEADS, HEAD_DIM), kn_a)
    va = va.reshape(B, S, A_KV_HEADS, HEAD_DIM)
    oa = _window_gqa(qa, ka, va, sink_a)
    qb = _rmsnorm(qb.reshape(B, S, B_HEADS, HEAD_DIM), qn_b)
    kb = _rmsnorm(kb.reshape(B, S, B_HEADS, HEAD_DIM), kn_b)
    vb = vb.reshape(B, S, B_HEADS, HEAD_DIM)
    ob = _neighbourhood_attn(qb, kb, vb, rpb_b)
    return jnp.concatenate([oa, ob], axis=-1) @ w_out


def _odd_mixer(x, norm_g, w_in, dec_f, dec_b, gn_g, w_out):
    B, S, _ = x.shape
    h = _rmsnorm(x, norm_g)
    q, k, v, g = _split(h @ w_in, ODD_SPLITS)
    q = q.reshape(B, S, RET_HEADS, RET_DK)
    k = k.reshape(B, S, RET_HEADS, RET_DK) * (RET_DK ** -0.5)
    v = v.reshape(B, S, RET_HEADS, RET_DV)
    lgf = jax.nn.log_sigmoid(dec_f.astype(jnp.float32))
    lgb = jax.nn.log_sigmoid(dec_b.astype(jnp.float32))
    y_f = _retention_dir(q, k, v, lgf, False)
    y_b = jnp.flip(_retention_dir(jnp.flip(q, 1), jnp.flip(k, 1), jnp.flip(v, 1), lgb, True), 1)
    yf = (y_f + y_b).astype(jnp.float32)
    mu = jnp.mean(yf, axis=-1, keepdims=True)
    var = jnp.mean(jnp.square(yf - mu), axis=-1, keepdims=True)
    yn = (yf - mu) * lax.rsqrt(var + GN_EPS) * gn_g.astype(jnp.float32).reshape(RET_HEADS, RET_DV)
    yn = yn.astype(x.dtype).reshape(B, S, ODD_MIX)
    return (jax.nn.silu(g) * yn) @ w_out


def _mlp(x, norm_g, w1, w2):
    h = _rmsnorm(x, norm_g)
    return jnp.square(jax.nn.relu(h @ w1)) @ w2


def setup_inputs(seed: int = 0) -> dict:
    key = jax.random.key(seed)
    ks = jax.random.split(key, 20)
    f32 = jnp.float32

    def nrm(k, shape, scale):
        return jax.random.normal(k, shape, f32) * scale

    def gain(k, shape):
        return 1.0 + 0.02 * jax.random.normal(k, shape, f32)

    hr = jnp.arange(RET_HEADS, dtype=f32)
    gamma = 1.0 - 2.0 ** (-5.0 - hr)
    base_logit = jnp.log(gamma) - jnp.log1p(-gamma)
    return {
        'x': nrm(ks[0], (BATCH, SEQ, D_MODEL), 1.0),
        'attn_norm_e': gain(ks[1], (N_EVEN, D_MODEL)),
        'w_in_e': nrm(ks[2], (N_EVEN, D_MODEL, EVEN_IN), D_MODEL ** -0.5),
        'q_norm_a': gain(ks[3], (N_EVEN, HEAD_DIM)),
        'k_norm_a': gain(ks[4], (N_EVEN, HEAD_DIM)),
        'sink_a': nrm(ks[5], (N_EVEN, A_Q_HEADS), 0.5),
        'q_norm_b': gain(ks[6], (N_EVEN, HEAD_DIM)),
        'k_norm_b': gain(ks[7], (N_EVEN, HEAD_DIM)),
        'rpb_b': nrm(ks[8], (N_EVEN, B_HEADS, 2 * NA_KH - 1, 2 * NA_KW - 1), 0.02),
        'w_out_e': nrm(ks[9], (N_EVEN, EVEN_MIX, D_MODEL), EVEN_MIX ** -0.5),
        'ret_norm_o': gain(ks[10], (N_ODD, D_MODEL)),
        'w_in_o': nrm(ks[11], (N_ODD, D_MODEL, ODD_IN), D_MODEL ** -0.5),
        'decay_fwd_o': base_logit[None] + nrm(ks[12], (N_ODD, RET_HEADS), 0.05),
        'decay_bwd_o': base_logit[None] + nrm(ks[13], (N_ODD, RET_HEADS), 0.05),
        'ret_gn_o': gain(ks[14], (N_ODD, ODD_MIX)),
        'w_out_o': nrm(ks[15], (N_ODD, ODD_MIX, D_MODEL), ODD_MIX ** -0.5),
        'mlp_norm': gain(ks[16], (DEPTH, D_MODEL)),
        'w_mlp_in': nrm(ks[17], (DEPTH, D_MODEL, D_FF), D_MODEL ** -0.5),
        'w_mlp_out': nrm(ks[18], (DEPTH, D_FF, D_MODEL), D_FF ** -0.5),
    }


def reference(x, attn_norm_e, w_in_e, q_norm_a, k_norm_a, sink_a, q_norm_b, k_norm_b, rpb_b, w_out_e,
              ret_norm_o, w_in_o, decay_fwd_o, decay_bwd_o, ret_gn_o, w_out_o,
              mlp_norm, w_mlp_in, w_mlp_out):
    for layer in range(DEPTH):
        i = layer // 2
        if layer % 2 == 0:
            x = x + _even_mixer(x, attn_norm_e[i], w_in_e[i], q_norm_a[i], k_norm_a[i], sink_a[i],
                                q_norm_b[i], k_norm_b[i], rpb_b[i], w_out_e[i])
        else:
            x = x + _odd_mixer(x, ret_norm_o[i], w_in_o[i], decay_fwd_o[i], decay_bwd_o[i],
                               ret_gn_o[i], w_out_o[i])
        x = x + _mlp(x, mlp_norm[layer], w_mlp_in[layer], w_mlp_out[layer])
    return x
```

```python
import functools

import numpy as np
import jax
import jax.numpy as jnp
from jax import lax
from jax.experimental import pallas as pl
from jax.experimental.pallas import tpu as pltpu

D_MODEL = 1024
SEQ = 16384
HEAD_DIM = 64
A_Q_HEADS = 8
A_KV_HEADS = 2
A_GROUP = A_Q_HEADS // A_KV_HEADS
A_WINDOW = 128
A_BLOCK = 128
B_HEADS = 8
GRID_W = 64
GRID_ROWS = SEQ // GRID_W
NA_KH = 8
NA_KW = 16
RET_HEADS = 8
RET_DK = D_MODEL // RET_HEADS
RET_DV = 2 * D_MODEL // RET_HEADS
RET_CHUNK = 128
D_FF = 4 * D_MODEL
ODD_MIX = RET_HEADS * RET_DV
RMS_EPS = 1e-6
GN_EPS = 1e-5

LANES = 128
MXU_N = 256
NEG = -1e30
VMEM_LIMIT = 60 * 1024 * 1024

F32 = jnp.float32
BF16 = jnp.bfloat16

E_QA, E_KA, E_VA, E_QB, E_KB, E_VB = 0, 512, 768, 1024, 1536, 2048
E_COLS = 2560


def _const_spec(shape):
    return pl.BlockSpec(shape, lambda *_: (0,) * len(shape), pipeline_mode=pl.Buffered(1))


def _params(*sem):
    return pltpu.CompilerParams(dimension_semantics=sem, vmem_limit_bytes=VMEM_LIMIT)


def _rmsnorm_rows(x, g):
    return x * lax.rsqrt(jnp.mean(x * x, axis=-1, keepdims=True) + RMS_EPS) * g


def _dot(a, b):
    return jnp.dot(a, b, preferred_element_type=F32)


def _dot_nt(a, b):
    return lax.dot_general(a, b, (((1,), (1,)), ((), ())), preferred_element_type=F32)


def _dot_tn(a, b):
    return lax.dot_general(a, b, (((0,), (0,)), ((), ())), preferred_element_type=F32)


def _inproj_even_kernel(x_ref, g_ref, w_ref, gain_ref, qa_ref, ka_ref, va_ref, qb_ref, kb_ref, vb_ref):
    xb = _rmsnorm_rows(x_ref[...], g_ref[...]).astype(BF16)
    r = lax.broadcasted_iota(jnp.int32, (MXU_N, MXU_N), 0) // HEAD_DIM
    c = lax.broadcasted_iota(jnp.int32, (MXU_N, MXU_N), 1) // HEAD_DIM
    head_mean = jnp.where(r == c, 1.0 / HEAD_DIM, 0.0).astype(BF16)
    outs = ((qa_ref, E_QA, 512, True), (ka_ref, E_KA, 256, True), (va_ref, E_VA, 256, False),
            (qb_ref, E_QB, 512, True), (kb_ref, E_KB, 512, True), (vb_ref, E_VB, 512, False))
    for ref, start, width, normed in outs:
        for off in range(0, width, MXU_N):
            col = start + off
            y = _dot(xb, w_ref[:, col:col + MXU_N])
            if normed:
                ms = _dot((y * y).astype(BF16), head_mean)
                y = y * lax.rsqrt(ms + RMS_EPS) * gain_ref[:, col:col + MXU_N]
            ref[:, off:off + MXU_N] = y.astype(BF16)


def _inproj_even(x, g, w, gain, tm=512):
    s = x.shape[0]
    widths = (512, 256, 256, 512, 512, 512)
    return pl.pallas_call(
        _inproj_even_kernel,
        grid=(s // tm,),
        in_specs=[pl.BlockSpec((tm, D_MODEL), lambda i: (i, 0)),
                  _const_spec((1, D_MODEL)), _const_spec((D_MODEL, E_COLS)), _const_spec((1, E_COLS))],
        out_specs=[pl.BlockSpec((tm, wd), lambda i: (i, 0)) for wd in widths],
        out_shape=[jax.ShapeDtypeStruct((s, wd), BF16) for wd in widths],
        compiler_params=_params("parallel"),
        name="inproj_even",
    )(x, g, w, gain)


def _window_kernel(sink_ref, q_ref, kp_ref, kc_ref, kn_ref, vp_ref, vc_ref, vn_ref, o_ref, *, tq):
    i = pl.program_id(0)
    lane = lax.broadcasted_iota(jnp.int32, (1, LANES), 1)
    low_half = lane < HEAD_DIM
    qi = lax.broadcasted_iota(jnp.int32, (A_BLOCK, 3 * A_BLOCK), 0)
    kj = lax.broadcasted_iota(jnp.int32, (A_BLOCK, 3 * A_BLOCK), 1)
    rel = kj - A_BLOCK - qi
    absrel = jnp.abs(rel)
    absrel_f = absrel.astype(F32)
    for j in range(A_KV_HEADS):
        ks = slice(LANES * j, LANES * (j + 1))
        kcat = jnp.concatenate([kp_ref[tq - A_BLOCK:, ks], kc_ref[:, ks], kn_ref[:A_BLOCK, ks]], axis=0)
        vcat = jnp.concatenate([vp_ref[tq - A_BLOCK:, ks], vc_ref[:, ks], vn_ref[:A_BLOCK, ks]], axis=0)
        for b in range(tq // A_BLOCK):
            rows = slice(A_BLOCK * b, A_BLOCK * (b + 1))
            kw = kcat[A_BLOCK * b:A_BLOCK * (b + 3)]
            vw = vcat[A_BLOCK * b:A_BLOCK * (b + 3)]
            kpos = i * tq + A_BLOCK * b + kj - A_BLOCK
            valid = (absrel <= A_WINDOW) & (kpos >= 0) & (kpos < SEQ)
            lhs = []
            for e in range(A_GROUP):
                pair = q_ref[rows, LANES * (2 * j + e // 2):LANES * (2 * j + e // 2 + 1)]
                lhs.append(jnp.where(low_half if e % 2 == 0 else ~low_half, pair, jnp.zeros_like(pair)))
            s_all = _dot_nt(jnp.concatenate(lhs, axis=0), kw)
            probs, inv = [], []
            for e in range(A_GROUP):
                h = A_GROUP * j + e
                slope = 2.0 ** (-(8.0 / A_Q_HEADS) * (h + 1.0))
                sk = sink_ref[h]
                s = s_all[A_BLOCK * e:A_BLOCK * (e + 1)] - slope * absrel_f
                s = jnp.where(valid, s, NEG)
                m = jnp.maximum(jnp.max(s, axis=-1, keepdims=True), sk)
                p = jnp.exp(s - m)
                inv.append(1.0 / (jnp.sum(p, axis=-1, keepdims=True) + jnp.exp(sk - m)))
                probs.append(p.astype(BF16))
            o = _dot(jnp.concatenate(probs, axis=0), vw)
            o = o * jnp.concatenate(inv, axis=0)
            for pr in range(A_GROUP // 2):
                lo = o[A_BLOCK * 2 * pr:A_BLOCK * (2 * pr + 1)]
                hi = o[A_BLOCK * (2 * pr + 1):A_BLOCK * (2 * pr + 2)]
                col = LANES * (2 * j + pr)
                o_ref[rows, col:col + LANES] = jnp.where(low_half, lo, hi).astype(BF16)


def _window_attn(sink, qa, ka2, va2, tq=512):
    s = qa.shape[0]
    nb = s // tq
    prev = lambda i: (jnp.maximum(i - 1, 0), 0)
    cur = lambda i: (i, 0)
    nxt = lambda i: (jnp.minimum(i + 1, nb - 1), 0)
    kv = lambda f: pl.BlockSpec((tq, 2 * LANES), f)
    return pl.pallas_call(
        functools.partial(_window_kernel, tq=tq),
        grid=(nb,),
        in_specs=[pl.BlockSpec(memory_space=pltpu.SMEM),
                  pl.BlockSpec((tq, 512), cur), kv(prev), kv(cur), kv(nxt), kv(prev), kv(cur), kv(nxt)],
        out_specs=pl.BlockSpec((tq, 512), cur),
        out_shape=jax.ShapeDtypeStruct((s, 512), BF16),
        compiler_params=_params("parallel"),
        name="window_gqa",
    )(sink, qa, ka2, ka2, ka2, va2, va2, va2)


NA_ROWS_PER_STEP = 8
NA_TOK = NA_ROWS_PER_STEP * GRID_W
NA_KEYS = NA_KH * GRID_W


def _na_kernel(q_ref, kp_ref, kc_ref, kn_ref, vp_ref, vc_ref, vn_ref, bias_ref, o_ref, kcat, vcat):
    i = pl.program_id(0)
    kcat[0:NA_TOK] = kp_ref[...]
    kcat[NA_TOK:2 * NA_TOK] = kc_ref[...]
    kcat[2 * NA_TOK:3 * NA_TOK] = kn_ref[...]
    vcat[0:NA_TOK] = vp_ref[...]
    vcat[NA_TOK:2 * NA_TOK] = vc_ref[...]
    vcat[2 * NA_TOK:3 * NA_TOK] = vn_ref[...]
    lane = lax.broadcasted_iota(jnp.int32, (1, LANES), 1)
    low_half = lane < HEAD_DIM

    def row_body(rr, carry):
        r = i * NA_ROWS_PER_STEP + rr
        rstart = jnp.clip(r - NA_KH // 2, 0, GRID_ROWS - NA_KH)
        start = pl.multiple_of((rstart - (i - 1) * NA_ROWS_PER_STEP) * GRID_W, GRID_W)
        ri0 = rstart - r + NA_KH - 1
        qrows = pl.ds(pl.multiple_of(rr * GRID_W, GRID_W), GRID_W)
        for p in range(B_HEADS // 2):
            cols = slice(LANES * p, LANES * (p + 1))
            qp = q_ref[qrows, cols]
            lhs = jnp.concatenate([jnp.where(low_half, qp, jnp.zeros_like(qp)),
                                   jnp.where(low_half, jnp.zeros_like(qp), qp)], axis=0)
            kw = kcat[pl.ds(start, NA_KEYS), cols]
            vw = vcat[pl.ds(start, NA_KEYS), cols]
            s = _dot_nt(lhs, kw)
            bias = jnp.concatenate(
                [jnp.concatenate([bias_ref[2 * p + e, ri0 + 2 * m] for m in range(NA_KH // 2)], axis=1)
                 for e in range(2)], axis=0)
            s = s + bias
            m_ = jnp.max(s, axis=-1, keepdims=True)
            pexp = jnp.exp(s - m_)
            inv = 1.0 / jnp.sum(pexp, axis=-1, keepdims=True)
            o = _dot(pexp.astype(BF16), vw) * inv
            o_ref[qrows, cols] = jnp.where(low_half, o[:GRID_W], o[GRID_W:]).astype(BF16)
        return carry

    lax.fori_loop(0, NA_ROWS_PER_STEP, row_body, 0)


def _na_bias_table(rpb):
    qc = np.arange(GRID_W)[:, None]
    kc = np.arange(GRID_W)[None, :]
    cstart = np.clip(qc - NA_KW // 2, 0, GRID_W - NA_KW)
    col_valid = (kc >= cstart) & (kc < cstart + NA_KW)
    col_idx = np.clip(kc - qc + NA_KW - 1, 0, 2 * NA_KW - 2)
    t = jnp.where(col_valid[None, None], rpb.astype(F32)[:, :, col_idx], NEG)
    return jnp.concatenate([t[:, :-1], t[:, 1:]], axis=-1)


def _na_attn(qb, kb, vb, bias):
    s = qb.shape[0]
    nb = s // NA_TOK
    prev = lambda i: (jnp.maximum(i - 1, 0), 0)
    cur = lambda i: (i, 0)
    nxt = lambda i: (jnp.minimum(i + 1, nb - 1), 0)
    blk = lambda f: pl.BlockSpec((NA_TOK, 512), f)
    return pl.pallas_call(
        _na_kernel,
        grid=(nb,),
        in_specs=[blk(cur), blk(prev), blk(cur), blk(nxt), blk(prev), blk(cur), blk(nxt),
                  _const_spec(bias.shape)],
        out_specs=blk(cur),
        out_shape=jax.ShapeDtypeStruct((s, 512), BF16),
        scratch_shapes=[pltpu.VMEM((3 * NA_TOK, 512), BF16), pltpu.VMEM((3 * NA_TOK, 512), BF16)],
        compiler_params=_params("parallel"),
        name="neighbourhood_attn",
    )(qb, kb, kb, kb, vb, vb, vb, bias)


FF_CHUNK = 1024


def _outproj_mlp_kernel(*refs, n_mix):
    x_ref = refs[0]
    mix_refs = refs[1:1 + n_mix]
    wout_ref, g_ref, w1_ref, w2_ref, o_ref = refs[1 + n_mix:]
    mix = jnp.concatenate([m_ref[...] for m_ref in mix_refs], axis=1)
    x1 = x_ref[...] + _dot(mix, wout_ref[...])
    hb = _rmsnorm_rows(x1, g_ref[...]).astype(BF16)
    acc = x1
    for c in range(0, D_FF, FF_CHUNK):
        a = jnp.maximum(_dot(hb, w1_ref[:, c:c + FF_CHUNK]), 0.0)
        acc = acc + _dot((a * a).astype(BF16), w2_ref[c:c + FF_CHUNK, :])
    o_ref[...] = acc


def _outproj_mlp(x, mixes, wout, g, w1, w2, tm=512, name="outproj_mlp"):
    s = x.shape[0]
    row = lambda i: (i, 0)
    in_specs = [pl.BlockSpec((tm, D_MODEL), row)]
    in_specs += [pl.BlockSpec((tm, m.shape[1]), row) for m in mixes]
    in_specs += [_const_spec(wout.shape), _const_spec((1, D_MODEL)), _const_spec(w1.shape), _const_spec(w2.shape)]
    return pl.pallas_call(
        functools.partial(_outproj_mlp_kernel, n_mix=len(mixes)),
        grid=(s // tm,),
        in_specs=in_specs,
        out_specs=pl.BlockSpec((tm, D_MODEL), row),
        out_shape=jax.ShapeDtypeStruct((s, D_MODEL), F32),
        compiler_params=_params("parallel"),
        name=name,
    )(x, *mixes, wout, g, w1, w2)


O_Q, O_K, O_V, O_G = 0, 1024, 2048, 4096
O_COLS = 6144
PROJ_CHUNK = 1024


def _inproj_odd_kernel(x_ref, g_ref, w_ref, q_ref, k_ref, v_ref, gate_ref):
    xb = _rmsnorm_rows(x_ref[...], g_ref[...]).astype(BF16)
    outs = ((q_ref, O_Q, 1024, 1.0), (k_ref, O_K, 1024, RET_DK ** -0.5), (v_ref, O_V, 2048, 1.0),
            (gate_ref, O_G, 2048, 1.0))
    for ref, start, width, scale in outs:
        for off in range(0, width, PROJ_CHUNK):
            y = _dot(xb, w_ref[:, start + off:start + off + PROJ_CHUNK])
            if scale != 1.0:
                y = y * scale
            ref[:, off:off + PROJ_CHUNK] = y.astype(BF16)


def _inproj_odd(x, g, w, tm=512):
    s = x.shape[0]
    widths = (1024, 1024, 2048, 2048)
    return pl.pallas_call(
        _inproj_odd_kernel,
        grid=(s // tm,),
        in_specs=[pl.BlockSpec((tm, D_MODEL), lambda i: (i, 0)),
                  _const_spec((1, D_MODEL)), _const_spec((D_MODEL, O_COLS))],
        out_specs=[pl.BlockSpec((tm, wd), lambda i: (i, 0)) for wd in widths],
        out_shape=[jax.ShapeDtypeStruct((s, wd), BF16) for wd in widths],
        compiler_params=_params("parallel"),
        name="inproj_odd",
    )(x, g, w)


RET_CHUNKS_PER_STEP = 8
RET_TOK = RET_CHUNKS_PER_STEP * RET_CHUNK
RET_STEPS = SEQ // RET_TOK


def _retention_kernel(lg_ref, q_ref, k_ref, v_ref, gate_ref, gn_ref, o_ref, state, yb):
    h = pl.program_id(0)
    t = pl.program_id(1)
    fwd = t >= RET_STEPS
    blk = jnp.where(fwd, t - RET_STEPS, RET_STEPS - 1 - t)

    @pl.when((t == 0) | (t == RET_STEPS))
    def _():
        state[...] = jnp.zeros_like(state)

    pos_i = lax.broadcasted_iota(jnp.int32, (RET_CHUNK, RET_CHUNK), 0).astype(F32)
    pos_j = lax.broadcasted_iota(jnp.int32, (RET_CHUNK, RET_CHUNK), 1).astype(F32)
    pos_c = lax.broadcasted_iota(jnp.int32, (RET_CHUNK, 1), 0).astype(F32)

    def direction(lg, diff, cross_pow, zeta_pow, strict, chunk_order, finish):
        allowed = (diff > 0) if strict else (diff >= 0)
        dmask = jnp.where(allowed, jnp.exp(lg * jnp.maximum(diff, 0.0)), 0.0)
        cross = jnp.exp(lg * cross_pow)
        zeta = jnp.exp(lg * zeta_pow)
        decay = jnp.exp(jnp.full((1, 1), RET_CHUNK, F32) * lg)

        def chunk(ci, carry):
            c = chunk_order(ci)
            rows = pl.ds(pl.multiple_of(c * RET_CHUNK, RET_CHUNK), RET_CHUNK)
            seq_rows = pl.ds(pl.multiple_of(blk * RET_TOK + c * RET_CHUNK, RET_CHUNK), RET_CHUNK)
            q = q_ref[rows, :]
            k = k_ref[rows, :]
            v = v_ref[rows, :]
            r = state[...]
            a = (_dot_nt(q, k) * dmask).astype(BF16)
            o = _dot(a, v) + _dot(q, r.astype(BF16)) * cross
            kz = (k.astype(F32) * zeta).astype(BF16)
            state[...] = r * decay + _dot_tn(kz, v)
            finish(rows, seq_rows, o)
            return carry

        lax.fori_loop(0, RET_CHUNKS_PER_STEP, chunk, 0)

    @pl.when(jnp.logical_not(fwd))
    def _():
        def park(rows, seq_rows, o):
            yb[seq_rows, :] = o
        direction(lg_ref[1, h], pos_j - pos_i, RET_CHUNK - pos_c, pos_c, True,
                  lambda ci: RET_CHUNKS_PER_STEP - 1 - ci, park)

    @pl.when(fwd)
    def _():
        def emit(rows, seq_rows, o):
            y = o + yb[seq_rows, :]
            mu = jnp.mean(y, axis=-1, keepdims=True)
            d = y - mu
            var = jnp.mean(d * d, axis=-1, keepdims=True)
            yn = d * lax.rsqrt(var + GN_EPS) * gn_ref[...]
            g = gate_ref[rows, :].astype(F32)
            o_ref[rows, :] = (g * (1.0 / (1.0 + jnp.exp(-g))) * yn).astype(BF16)
        direction(lg_ref[0, h], pos_i - pos_j, pos_c + 1.0, RET_CHUNK - 1.0 - pos_c, False,
                  lambda ci: ci, emit)


def _retention(lg, q, k, v, gate, gn):
    s = q.shape[0]

    def blk_of(t):
        return jnp.where(t >= RET_STEPS, t - RET_STEPS, RET_STEPS - 1 - t)

    qk_spec = pl.BlockSpec((RET_TOK, RET_DK), lambda h, t: (blk_of(t), h))
    v_spec = pl.BlockSpec((RET_TOK, RET_DV), lambda h, t: (blk_of(t), h))
    fwd_spec = pl.BlockSpec((RET_TOK, RET_DV), lambda h, t: (jnp.maximum(t - RET_STEPS, 0), h))
    return pl.pallas_call(
        _retention_kernel,
        grid=(RET_HEADS, 2 * RET_STEPS),
        in_specs=[pl.BlockSpec(memory_space=pltpu.SMEM), qk_spec, qk_spec, v_spec, fwd_spec,
                  pl.BlockSpec((1, RET_DV), lambda h, t: (0, h))],
        out_specs=fwd_spec,
        out_shape=jax.ShapeDtypeStruct((s, ODD_MIX), BF16),
        scratch_shapes=[pltpu.VMEM((RET_DK, RET_DV), F32), pltpu.VMEM((s, RET_DV), F32)],
        compiler_params=_params("arbitrary", "arbitrary"),
        name="retention",
    )(lg, q, k, v, gate, gn)


def _even_proj_weights(w_in, qn_a, kn_a, qn_b, kn_b):
    qa, ka, va, qb, kb, vb = jnp.split(w_in, np.cumsum([512, 128, 128, 512, 512]).tolist(), axis=-1)
    dup = lambda w: jnp.concatenate([w[:, :HEAD_DIM]] * 2 + [w[:, HEAD_DIM:]] * 2, axis=-1)
    w = jnp.concatenate([qa, dup(ka), dup(va), qb, kb, vb], axis=-1).astype(BF16)
    scale = HEAD_DIM ** -0.5
    gain = jnp.concatenate([jnp.tile(qn_a, A_Q_HEADS) * scale, jnp.tile(kn_a, 4), jnp.ones((256,), F32),
                            jnp.tile(qn_b, B_HEADS) * scale, jnp.tile(kn_b, B_HEADS),
                            jnp.ones((512,), F32)]).astype(F32)
    return w, gain[None, :]


def kernel(x, attn_norm_e, w_in_e, q_norm_a, k_norm_a, sink_a, q_norm_b, k_norm_b, rpb_b, w_out_e,
           ret_norm_o, w_in_o, decay_fwd_o, decay_bwd_o, ret_gn_o, w_out_o,
           mlp_norm, w_mlp_in, w_mlp_out):
    b, s, d = x.shape
    xs = x.reshape(b * s, d)
    depth = mlp_norm.shape[0]
    for layer in range(depth):
        i = layer // 2
        w1 = w_mlp_in[layer].astype(BF16)
        w2 = w_mlp_out[layer].astype(BF16)
        g_mlp = mlp_norm[layer][None, :]
        if layer % 2 == 0:
            w, gain = _even_proj_weights(w_in_e[i], q_norm_a[i], k_norm_a[i], q_norm_b[i], k_norm_b[i])
            qa, ka2, va2, qb, kb, vb = _inproj_even(xs, attn_norm_e[i][None, :], w, gain)
            oa = _window_attn(sink_a[i].astype(F32), qa, ka2, va2)
            ob = _na_attn(qb, kb, vb, _na_bias_table(rpb_b[i]))
            xs = _outproj_mlp(xs, (oa, ob), w_out_e[i].astype(BF16), g_mlp, w1, w2, name="outproj_mlp_even")
        else:
            q, k, v, gate = _inproj_odd(xs, ret_norm_o[i][None, :], w_in_o[i].astype(BF16))
            lg = jnp.stack([jax.nn.log_sigmoid(decay_fwd_o[i].astype(F32)),
                            jax.nn.log_sigmoid(decay_bwd_o[i].astype(F32))])
            mix = _retention(lg, q, k, v, gate, ret_gn_o[i].astype(F32)[None, :])
            xs = _outproj_mlp(xs, (mix,), w_out_o[i].astype(BF16), g_mlp, w1, w2, name="outproj_mlp_odd")
    return xs.reshape(b, s, d)
```

```python
import functools

import numpy as np
import jax
import jax.numpy as jnp
from jax import lax
from jax.experimental import pallas as pl
from jax.experimental.pallas import tpu as pltpu

D_MODEL = 1024
SEQ = 16384
HEAD_DIM = 64
A_Q_HEADS = 8
A_KV_HEADS = 2
A_GROUP = A_Q_HEADS // A_KV_HEADS
A_WINDOW = 128
A_BLOCK = 128
B_HEADS = 8
GRID_W = 64
GRID_ROWS = SEQ // GRID_W
NA_KH = 8
NA_KW = 16
RET_HEADS = 8
RET_DK = D_MODEL // RET_HEADS
RET_DV = 2 * D_MODEL // RET_HEADS
RET_CHUNK = 128
D_FF = 4 * D_MODEL
ODD_MIX = RET_HEADS * RET_DV
RMS_EPS = 1e-6
GN_EPS = 1e-5

LANES = 128
MXU_N = 256
NEG = -1e30
VMEM_LIMIT = 60 * 1024 * 1024

F32 = jnp.float32
BF16 = jnp.bfloat16

E_QA, E_KA, E_VA, E_QB, E_KB, E_VB = 0, 512, 768, 1024, 1536, 2048
E_COLS = 2560


def _const_spec(shape):
    return pl.BlockSpec(shape, lambda *_: (0,) * len(shape), pipeline_mode=pl.Buffered(1))


def _params(*sem):
    return pltpu.CompilerParams(dimension_semantics=sem, vmem_limit_bytes=VMEM_LIMIT)


def _rmsnorm_rows(x, g):
    return x * lax.rsqrt(jnp.mean(x * x, axis=-1, keepdims=True) + RMS_EPS) * g


def _dot(a, b):
    return jnp.dot(a, b, preferred_element_type=F32)


def _dot_nt(a, b):
    return lax.dot_general(a, b, (((1,), (1,)), ((), ())), preferred_element_type=F32)


def _dot_tn(a, b):
    return lax.dot_general(a, b, (((0,), (0,)), ((), ())), preferred_element_type=F32)


def _inproj_even_kernel(x_ref, g_ref, w_ref, gain_ref, qa_ref, ka_ref, va_ref, qb_ref, kb_ref, vb_ref):
    xb = _rmsnorm_rows(x_ref[...], g_ref[...]).astype(BF16)
    r = lax.broadcasted_iota(jnp.int32, (MXU_N, MXU_N), 0) // HEAD_DIM
    c = lax.broadcasted_iota(jnp.int32, (MXU_N, MXU_N), 1) // HEAD_DIM
    head_mean = jnp.where(r == c, 1.0 / HEAD_DIM, 0.0).astype(BF16)
    outs = ((qa_ref, E_QA, 512, True), (ka_ref, E_KA, 256, True), (va_ref, E_VA, 256, False),
            (qb_ref, E_QB, 512, True), (kb_ref, E_KB, 512, True), (vb_ref, E_VB, 512, False))
    for ref, start, width, normed in outs:
        for off in range(0, width, MXU_N):
            col = start + off
            y = _dot(xb, w_ref[:, col:col + MXU_N])
            if normed:
                ms = _dot((y * y).astype(BF16), head_mean)
                y = y * lax.rsqrt(ms + RMS_EPS) * gain_ref[:, col:col + MXU_N]
            ref[:, off:off + MXU_N] = y.astype(BF16)


def _inproj_even(x, g, w, gain, tm=512):
    s = x.shape[0]
    widths = (512, 256, 256, 512, 512, 512)
    return pl.pallas_call(
        _inproj_even_kernel,
        grid=(s // tm,),
        in_specs=[pl.BlockSpec((tm, D_MODEL), lambda i: (i, 0)),
                  _const_spec((1, D_MODEL)), _const_spec((D_MODEL, E_COLS)), _const_spec((1, E_COLS))],
        out_specs=[pl.BlockSpec((tm, wd), lambda i: (i, 0)) for wd in widths],
        out_shape=[jax.ShapeDtypeStruct((s, wd), BF16) for wd in widths],
        compiler_params=_params("parallel"),
        name="inproj_even",
    )(x, g, w, gain)


def _window_kernel(sink_ref, q_ref, kp_ref, kc_ref, kn_ref, vp_ref, vc_ref, vn_ref, o_ref, *, tq):
    i = pl.program_id(0)
    lane = lax.broadcasted_iota(jnp.int32, (1, LANES), 1)
    low_half = lane < HEAD_DIM
    qi = lax.broadcasted_iota(jnp.int32, (A_BLOCK, 3 * A_BLOCK), 0)
    kj = lax.broadcasted_iota(jnp.int32, (A_BLOCK, 3 * A_BLOCK), 1)
    rel = kj - A_BLOCK - qi
    absrel = jnp.abs(rel)
    absrel_f = absrel.astype(F32)
    for j in range(A_KV_HEADS):
        ks = slice(LANES * j, LANES * (j + 1))
        kcat = jnp.concatenate([kp_ref[tq - A_BLOCK:, ks], kc_ref[:, ks], kn_ref[:A_BLOCK, ks]], axis=0)
        vcat = jnp.concatenate([vp_ref[tq - A_BLOCK:, ks], vc_ref[:, ks], vn_ref[:A_BLOCK, ks]], axis=0)
        for b in range(tq // A_BLOCK):
            rows = slice(A_BLOCK * b, A_BLOCK * (b + 1))
            kw = kcat[A_BLOCK * b:A_BLOCK * (b + 3)]
            vw = vcat[A_BLOCK * b:A_BLOCK * (b + 3)]
            kpos = i * tq + A_BLOCK * b + kj - A_BLOCK
            valid = (absrel <= A_WINDOW) & (kpos >= 0) & (kpos < SEQ)
            lhs = []
            for e in range(A_GROUP):
                pair = q_ref[rows, LANES * (2 * j + e // 2):LANES * (2 * j + e // 2 + 1)]
                lhs.append(jnp.where(low_half if e % 2 == 0 else ~low_half, pair, jnp.zeros_like(pair)))
            s_all = _dot_nt(jnp.concatenate(lhs, axis=0), kw)
            probs, inv = [], []
            for e in range(A_GROUP):
                h = A_GROUP * j + e
                slope = 2.0 ** (-(8.0 / A_Q_HEADS) * (h + 1.0))
                sk = sink_ref[h]
                s = s_all[A_BLOCK * e:A_BLOCK * (e + 1)] - slope * absrel_f
                s = jnp.where(valid, s, NEG)
                m = jnp.maximum(jnp.max(s, axis=-1, keepdims=True), sk)
                p = jnp.exp(s - m)
                inv.append(1.0 / (jnp.sum(p, axis=-1, keepdims=True) + jnp.exp(sk - m)))
                probs.append(p.astype(BF16))
            o = _dot(jnp.concatenate(probs, axis=0), vw)
            o = o * jnp.concatenate(inv, axis=0)
            for pr in range(A_GROUP // 2):
                lo = o[A_BLOCK * 2 * pr:A_BLOCK * (2 * pr + 1)]
                hi = o[A_BLOCK * (2 * pr + 1):A_BLOCK * (2 * pr + 2)]
                col = LANES * (2 * j + pr)
                o_ref[rows, col:col + LANES] = jnp.where(low_half, lo, hi).astype(BF16)


def _window_attn(sink, qa, ka2, va2, tq=512):
    s = qa.shape[0]
    nb = s // tq
    prev = lambda i: (jnp.maximum(i - 1, 0), 0)
    cur = lambda i: (i, 0)
    nxt = lambda i: (jnp.minimum(i + 1, nb - 1), 0)
    kv = lambda f: pl.BlockSpec((tq, 2 * LANES), f)
    return pl.pallas_call(
        functools.partial(_window_kernel, tq=tq),
        grid=(nb,),
        in_specs=[pl.BlockSpec(memory_space=pltpu.SMEM),
                  pl.BlockSpec((tq, 512), cur), kv(prev), kv(cur), kv(nxt), kv(prev), kv(cur), kv(nxt)],
        out_specs=pl.BlockSpec((tq, 512), cur),
        out_shape=jax.ShapeDtypeStruct((s, 512), BF16),
        compiler_params=_params("parallel"),
        name="window_gqa",
    )(sink, qa, ka2, ka2, ka2, va2, va2, va2)


NA_ROWS_PER_STEP = 8
NA_TOK = NA_ROWS_PER_STEP * GRID_W
NA_KEYS = NA_KH * GRID_W


def _na_kernel(q_ref, kp_ref, kc_ref, kn_ref, vp_ref, vc_ref, vn_ref, bias_ref, o_ref, kcat, vcat):
    i = pl.program_id(0)
    kcat[0:NA_TOK] = kp_ref[...]
    kcat[NA_TOK:2 * NA_TOK] = kc_ref[...]
    kcat[2 * NA_TOK:3 * NA_TOK] = kn_ref[...]
    vcat[0:NA_TOK] = vp_ref[...]
    vcat[NA_TOK:2 * NA_TOK] = vc_ref[...]
    vcat[2 * NA_TOK:3 * NA_TOK] = vn_ref[...]
    lane = lax.broadcasted_iota(jnp.int32, (1, LANES), 1)
    low_half = lane < HEAD_DIM

    def row_body(rr, carry):
        r = i * NA_ROWS_PER_STEP + rr
        rstart = jnp.clip(r - NA_KH // 2, 0, GRID_ROWS - NA_KH)
        start = pl.multiple_of((rstart - (i - 1) * NA_ROWS_PER_STEP) * GRID_W, GRID_W)
        ri0 = rstart - r + NA_KH - 1
        qrows = pl.ds(pl.multiple_of(rr * GRID_W, GRID_W), GRID_W)
        for p in range(B_HEADS // 2):
            cols = slice(LANES * p, LANES * (p + 1))
            qp = q_ref[qrows, cols]
            lhs = jnp.concatenate([jnp.where(low_half, qp, jnp.zeros_like(qp)),
                                   jnp.where(low_half, jnp.zeros_like(qp), qp)], axis=0)
            kw = kcat[pl.ds(start, NA_KEYS), cols]
            vw = vcat[pl.ds(start, NA_KEYS), cols]
            s = _dot_nt(lhs, kw)
            bias = jnp.concatenate(
                [jnp.concatenate([bias_ref[2 * p + e, ri0 + 2 * m] for m in range(NA_KH // 2)], axis=1)
                 for e in range(2)], axis=0)
            s = s + bias
            m_ = jnp.max(s, axis=-1, keepdims=True)
            pexp = jnp.exp(s - m_)
            inv = 1.0 / jnp.sum(pexp, axis=-1, keepdims=True)
            o = _dot(pexp.astype(BF16), vw) * inv
            o_ref[qrows, cols] = jnp.where(low_half, o[:GRID_W], o[GRID_W:]).astype(BF16)
        return carry

    lax.fori_loop(0, NA_ROWS_PER_STEP, row_body, 0)


def _na_bias_table(rpb):
    qc = np.arange(GRID_W)[:, None]
    kc = np.arange(GRID_W)[None, :]
    cstart = np.clip(qc - NA_KW // 2, 0, GRID_W - NA_KW)
    col_valid = (kc >= cstart) & (kc < cstart + NA_KW)
    col_idx = np.clip(kc - qc + NA_KW - 1, 0, 2 * NA_KW - 2)
    t = jnp.where(col_valid[None, None], rpb.astype(F32)[:, :, col_idx], NEG)
    return jnp.concatenate([t[:, :-1], t[:, 1:]], axis=-1)


def _na_attn(qb, kb, vb, bias):
    s = qb.shape[0]
    nb = s // NA_TOK
    prev = lambda i: (jnp.maximum(i - 1, 0), 0)
    cur = lambda i: (i, 0)
    nxt = lambda i: (jnp.minimum(i + 1, nb - 1), 0)
    blk = lambda f: pl.BlockSpec((NA_TOK, 512), f)
    return pl.pallas_call(
        _na_kernel,
        grid=(nb,),
        in_specs=[blk(cur), blk(prev), blk(cur), blk(nxt), blk(prev), blk(cur), blk(nxt),
                  _const_spec(bias.shape)],
        out_specs=blk(cur),
        out_shape=jax.ShapeDtypeStruct((s, 512), BF16),
        scratch_shapes=[pltpu.VMEM((3 * NA_TOK, 512), BF16), pltpu.VMEM((3 * NA_TOK, 512), BF16)],
        compiler_params=_params("parallel"),
        name="neighbourhood_attn",
    )(qb, kb, kb, kb, vb, vb, vb, bias)


FF_CHUNK = 1024


def _outproj_mlp_kernel(*refs, n_mix):
    x_ref = refs[0]
    mix_refs = refs[1:1 + n_mix]
    wout_ref, g_ref, w1_ref, w2_ref, o_ref = refs[1 + n_mix:]
    mix = jnp.concatenate([m_ref[...] for m_ref in mix_refs], axis=1)
    x1 = x_ref[...] + _dot(mix, wout_ref[...])
    hb = _rmsnorm_rows(x1, g_ref[...]).astype(BF16)
    acc = x1
    for c in range(0, D_FF, FF_CHUNK):
        a = jnp.maximum(_dot(hb, w1_ref[:, c:c + FF_CHUNK]), 0.0)
        acc = acc + _dot((a * a).astype(BF16), w2_ref[c:c + FF_CHUNK, :])
    o_ref[...] = acc


def _outproj_mlp(x, mixes, wout, g, w1, w2, tm=512, name="outproj_mlp"):
    s = x.shape[0]
    row = lambda i: (i, 0)
    in_specs = [pl.BlockSpec((tm, D_MODEL), row)]
    in_specs += [pl.BlockSpec((tm, m.shape[1]), row) for m in mixes]
    in_specs += [_const_spec(wout.shape), _const_spec((1, D_MODEL)), _const_spec(w1.shape), _const_spec(w2.shape)]
    return pl.pallas_call(
        functools.partial(_outproj_mlp_kernel, n_mix=len(mixes)),
        grid=(s // tm,),
        in_specs=in_specs,
        out_specs=pl.BlockSpec((tm, D_MODEL), row),
        out_shape=jax.ShapeDtypeStruct((s, D_MODEL), F32),
        compiler_params=_params("parallel"),
        name=name,
    )(x, *mixes, wout, g, w1, w2)


PROJ_CHUNK = 1024


def _inproj_odd_kernel(x_ref, g_ref, wq_ref, wkt_ref, wv_ref, wg_ref, q_ref, kt_ref, v_ref, gate_ref):
    xb = _rmsnorm_rows(x_ref[...], g_ref[...]).astype(BF16)
    q_ref[...] = _dot(xb, wq_ref[...]).astype(BF16)
    kt_ref[...] = (_dot_nt(wkt_ref[...], xb) * RET_DK ** -0.5).astype(BF16)
    for w_ref, ref in ((wv_ref, v_ref), (wg_ref, gate_ref)):
        for off in range(0, ODD_MIX, PROJ_CHUNK):
            ref[:, off:off + PROJ_CHUNK] = _dot(xb, w_ref[:, off:off + PROJ_CHUNK]).astype(BF16)


def _inproj_odd(x, g, wq, wkt, wv, wg, tm=512):
    s = x.shape[0]
    row = lambda i: (i, 0)
    return pl.pallas_call(
        _inproj_odd_kernel,
        grid=(s // tm,),
        in_specs=[pl.BlockSpec((tm, D_MODEL), row), _const_spec((1, D_MODEL)),
                  _const_spec(wq.shape), _const_spec(wkt.shape), _const_spec(wv.shape), _const_spec(wg.shape)],
        out_specs=[pl.BlockSpec((tm, D_MODEL), row), pl.BlockSpec((D_MODEL, tm), lambda i: (0, i)),
                   pl.BlockSpec((tm, ODD_MIX), row), pl.BlockSpec((tm, ODD_MIX), row)],
        out_shape=[jax.ShapeDtypeStruct((s, D_MODEL), BF16), jax.ShapeDtypeStruct((D_MODEL, s), BF16),
                   jax.ShapeDtypeStruct((s, ODD_MIX), BF16), jax.ShapeDtypeStruct((s, ODD_MIX), BF16)],
        compiler_params=_params("parallel"),
        name="inproj_odd",
    )(x, g, wq, wkt, wv, wg)


RET_CHUNKS_PER_STEP = 8
RET_TOK = RET_CHUNKS_PER_STEP * RET_CHUNK
RET_STEPS = SEQ // RET_TOK


RET_NCHUNK = SEQ // RET_CHUNK


def _retention_kernel(lg_ref, q_ref, kt_ref, v_ref, gate_ref, gn_ref, o_ref, rf_all, state):
    h = pl.program_id(0)
    t = pl.program_id(1)
    second = t >= RET_STEPS
    blk = jnp.where(second, 2 * RET_STEPS - 1 - t, t)
    lgf = lg_ref[0, h]
    lgb = lg_ref[1, h]
    tok_lane = lax.broadcasted_iota(jnp.int32, (1, RET_CHUNK), 1).astype(F32)
    tok_row = lax.broadcasted_iota(jnp.int32, (RET_CHUNK, 1), 0).astype(F32)
    chunk_len = jnp.full((1, 1), RET_CHUNK, F32)

    @pl.when((t == 0) | (t == RET_STEPS))
    def _():
        state[...] = jnp.zeros_like(state)

    @pl.when(jnp.logical_not(second))
    def _():
        zeta = jnp.exp(lgf * (RET_CHUNK - 1.0 - tok_lane))
        decay = jnp.exp(lgf * chunk_len)
        r = state[...]
        for c in range(RET_CHUNKS_PER_STEP):
            tok = slice(c * RET_CHUNK, (c + 1) * RET_CHUNK)
            kz = (kt_ref[:, tok].astype(F32) * zeta).astype(BF16)
            rf_all[blk * RET_CHUNKS_PER_STEP + c] = r.astype(BF16)
            r = r * decay + _dot(kz, v_ref[tok, :])
        state[...] = r

    @pl.when(second)
    def _():
        diff = (lax.broadcasted_iota(jnp.int32, (RET_CHUNK, RET_CHUNK), 0)
                - lax.broadcasted_iota(jnp.int32, (RET_CHUNK, RET_CHUNK), 1)).astype(F32)
        dsum = jnp.where(diff >= 0, jnp.exp(lgf * jnp.maximum(diff, 0.0)), jnp.exp(lgb * jnp.maximum(-diff, 0.0)))
        cross_f = jnp.exp(lgf * (tok_row + 1.0))
        cross_b = jnp.exp(lgb * (RET_CHUNK - tok_row))
        zeta = jnp.exp(lgb * tok_lane)
        decay = jnp.exp(lgb * chunk_len)
        gn = gn_ref[...]
        r = state[...]
        for c in reversed(range(RET_CHUNKS_PER_STEP)):
            tok = slice(c * RET_CHUNK, (c + 1) * RET_CHUNK)
            q = q_ref[tok, :]
            kt = kt_ref[:, tok]
            v = v_ref[tok, :]
            a = (_dot(q, kt) * dsum).astype(BF16)
            qf = q.astype(F32)
            lhs = jnp.concatenate([a, (qf * cross_f).astype(BF16), (qf * cross_b).astype(BF16)], axis=1)
            rhs = jnp.concatenate([v, rf_all[blk * RET_CHUNKS_PER_STEP + c], r.astype(BF16)], axis=0)
            y = _dot(lhs, rhs)
            r = r * decay + _dot((kt.astype(F32) * zeta).astype(BF16), v)
            mu = jnp.mean(y, axis=-1, keepdims=True)
            d = y - mu
            var = jnp.mean(d * d, axis=-1, keepdims=True)
            yn = d * lax.rsqrt(var + GN_EPS) * gn
            g = gate_ref[tok, :].astype(F32)
            o_ref[tok, :] = (g * (1.0 / (1.0 + jnp.exp(-g))) * yn).astype(BF16)
        state[...] = r


def _retention(lg, q, kt, v, gate, gn):
    s = q.shape[0]

    def blk_of(t):
        return jnp.where(t >= RET_STEPS, 2 * RET_STEPS - 1 - t, t)

    def second_blk(t):
        return jnp.where(t >= RET_STEPS, 2 * RET_STEPS - 1 - t, RET_STEPS - 1)

    return pl.pallas_call(
        _retention_kernel,
        grid=(RET_HEADS, 2 * RET_STEPS),
        in_specs=[pl.BlockSpec(memory_space=pltpu.SMEM),
                  pl.BlockSpec((RET_TOK, RET_DK), lambda h, t: (second_blk(t), h)),
                  pl.BlockSpec((RET_DK, RET_TOK), lambda h, t: (h, blk_of(t))),
                  pl.BlockSpec((RET_TOK, RET_DV), lambda h, t: (blk_of(t), h)),
                  pl.BlockSpec((RET_TOK, RET_DV), lambda h, t: (second_blk(t), h)),
                  pl.BlockSpec((1, RET_DV), lambda h, t: (0, h))],
        out_specs=pl.BlockSpec((RET_TOK, RET_DV), lambda h, t: (second_blk(t), h)),
        out_shape=jax.ShapeDtypeStruct((s, ODD_MIX), BF16),
        scratch_shapes=[pltpu.VMEM((RET_NCHUNK, RET_DK, RET_DV), BF16), pltpu.VMEM((RET_DK, RET_DV), F32)],
        compiler_params=_params("arbitrary", "arbitrary"),
        name="retention",
    )(lg, q, kt, v, gate, gn)


def _even_proj_weights(w_in, qn_a, kn_a, qn_b, kn_b):
    qa, ka, va, qb, kb, vb = jnp.split(w_in, np.cumsum([512, 128, 128, 512, 512]).tolist(), axis=-1)
    dup = lambda w: jnp.concatenate([w[:, :HEAD_DIM]] * 2 + [w[:, HEAD_DIM:]] * 2, axis=-1)
    w = jnp.concatenate([qa, dup(ka), dup(va), qb, kb, vb], axis=-1).astype(BF16)
    scale = HEAD_DIM ** -0.5
    gain = jnp.concatenate([jnp.tile(qn_a, A_Q_HEADS) * scale, jnp.tile(kn_a, 4), jnp.ones((256,), F32),
                            jnp.tile(qn_b, B_HEADS) * scale, jnp.tile(kn_b, B_HEADS),
                            jnp.ones((512,), F32)]).astype(F32)
    return w, gain[None, :]


def kernel(x, attn_norm_e, w_in_e, q_norm_a, k_norm_a, sink_a, q_norm_b, k_norm_b, rpb_b, w_out_e,
           ret_norm_o, w_in_o, decay_fwd_o, decay_bwd_o, ret_gn_o, w_out_o,
           mlp_norm, w_mlp_in, w_mlp_out):
    b, s, d = x.shape
    xs = x.reshape(b * s, d)
    depth = mlp_norm.shape[0]
    for layer in range(depth):
        i = layer // 2
        w1 = w_mlp_in[layer].astype(BF16)
        w2 = w_mlp_out[layer].astype(BF16)
        g_mlp = mlp_norm[layer][None, :]
        if layer % 2 == 0:
            w, gain = _even_proj_weights(w_in_e[i], q_norm_a[i], k_norm_a[i], q_norm_b[i], k_norm_b[i])
            qa, ka2, va2, qb, kb, vb = _inproj_even(xs, attn_norm_e[i][None, :], w, gain)
            oa = _window_attn(sink_a[i].astype(F32), qa, ka2, va2)
            ob = _na_attn(qb, kb, vb, _na_bias_table(rpb_b[i]))
            xs = _outproj_mlp(xs, (oa, ob), w_out_e[i].astype(BF16), g_mlp, w1, w2, name="outproj_mlp_even")
        else:
            wq, wk, wv, wg = jnp.split(w_in_o[i].astype(BF16), [1024, 2048, 4096], axis=-1)
            q, kt, v, gate = _inproj_odd(xs, ret_norm_o[i][None, :], wq, wk.T, wv, wg)
            lg = jnp.stack([jax.nn.log_sigmoid(decay_fwd_o[i].astype(F32)),
                            jax.nn.log_sigmoid(decay_bwd_o[i].astype(F32))])
            mix = _retention(lg, q, kt, v, gate, ret_gn_o[i].astype(F32)[None, :])
            xs = _outproj_mlp(xs, (mix,), w_out_o[i].astype(BF16), g_mlp, w1, w2, name="outproj_mlp_odd")
    return xs.reshape(b, s, d)
```

```python
import functools

import numpy as np
import jax
import jax.numpy as jnp
from jax import lax
from jax.experimental import pallas as pl
from jax.experimental.pallas import tpu as pltpu

D_MODEL = 1024
SEQ = 16384
HEAD_DIM = 64
A_Q_HEADS = 8
A_KV_HEADS = 2
A_GROUP = A_Q_HEADS // A_KV_HEADS
A_WINDOW = 128
A_BLOCK = 128
B_HEADS = 8
GRID_W = 64
GRID_ROWS = SEQ // GRID_W
NA_KH = 8
NA_KW = 16
RET_HEADS = 8
RET_DK = D_MODEL // RET_HEADS
RET_DV = 2 * D_MODEL // RET_HEADS
RET_CHUNK = 128
D_FF = 4 * D_MODEL
ODD_MIX = RET_HEADS * RET_DV
RMS_EPS = 1e-6
GN_EPS = 1e-5

LANES = 128
MXU_N = 256
NEG = -1e30
VMEM_LIMIT = 60 * 1024 * 1024

F32 = jnp.float32
BF16 = jnp.bfloat16

E_QA, E_KA, E_VA, E_QB, E_KB, E_VB = 0, 512, 768, 1024, 1536, 2048
E_COLS = 2560


def _const_spec(shape):
    return pl.BlockSpec(shape, lambda *_: (0,) * len(shape), pipeline_mode=pl.Buffered(1))


def _params(*sem):
    return pltpu.CompilerParams(dimension_semantics=sem, vmem_limit_bytes=VMEM_LIMIT)


def _rmsnorm_rows(x, g):
    return x * lax.rsqrt(jnp.mean(x * x, axis=-1, keepdims=True) + RMS_EPS) * g


def _dot(a, b):
    return jnp.dot(a, b, preferred_element_type=F32)


def _dot_nt(a, b):
    return lax.dot_general(a, b, (((1,), (1,)), ((), ())), preferred_element_type=F32)


def _dot_tn(a, b):
    return lax.dot_general(a, b, (((0,), (0,)), ((), ())), preferred_element_type=F32)


def _inproj_even_kernel(x_ref, g_ref, w_ref, gain_ref, qa_ref, ka_ref, va_ref, qb_ref, kb_ref, vb_ref):
    xb = _rmsnorm_rows(x_ref[...], g_ref[...]).astype(BF16)
    r = lax.broadcasted_iota(jnp.int32, (MXU_N, MXU_N), 0) // HEAD_DIM
    c = lax.broadcasted_iota(jnp.int32, (MXU_N, MXU_N), 1) // HEAD_DIM
    head_mean = jnp.where(r == c, 1.0 / HEAD_DIM, 0.0).astype(BF16)
    outs = ((qa_ref, E_QA, 512, True), (ka_ref, E_KA, 256, True), (va_ref, E_VA, 256, False),
            (qb_ref, E_QB, 512, True), (kb_ref, E_KB, 512, True), (vb_ref, E_VB, 512, False))
    for ref, start, width, normed in outs:
        for off in range(0, width, MXU_N):
            col = start + off
            y = _dot(xb, w_ref[:, col:col + MXU_N])
            if normed:
                ms = _dot((y * y).astype(BF16), head_mean)
                y = y * lax.rsqrt(ms + RMS_EPS) * gain_ref[:, col:col + MXU_N]
            ref[:, off:off + MXU_N] = y.astype(BF16)


def _inproj_even(x, g, w, gain, tm=512):
    s = x.shape[0]
    widths = (512, 256, 256, 512, 512, 512)
    return pl.pallas_call(
        _inproj_even_kernel,
        grid=(s // tm,),
        in_specs=[pl.BlockSpec((tm, D_MODEL), lambda i: (i, 0)),
                  _const_spec((1, D_MODEL)), _const_spec((D_MODEL, E_COLS)), _const_spec((1, E_COLS))],
        out_specs=[pl.BlockSpec((tm, wd), lambda i: (i, 0)) for wd in widths],
        out_shape=[jax.ShapeDtypeStruct((s, wd), BF16) for wd in widths],
        compiler_params=_params("parallel"),
        name="inproj_even",
    )(x, g, w, gain)


def _window_kernel(sink_ref, q_ref, kp_ref, kc_ref, kn_ref, vp_ref, vc_ref, vn_ref, o_ref, *, tq):
    i = pl.program_id(0)
    lane = lax.broadcasted_iota(jnp.int32, (1, LANES), 1)
    low_half = lane < HEAD_DIM
    qi = lax.broadcasted_iota(jnp.int32, (A_BLOCK, 3 * A_BLOCK), 0)
    kj = lax.broadcasted_iota(jnp.int32, (A_BLOCK, 3 * A_BLOCK), 1)
    absrel = jnp.abs(kj - A_BLOCK - qi)
    in_window = absrel <= A_WINDOW
    absrel_f = absrel.astype(F32)
    kcol = lax.broadcasted_iota(jnp.int32, (1, 3 * A_BLOCK), 1)
    edge_first = jnp.where((i == 0) & (kcol < A_BLOCK), NEG, 0.0)
    edge_last = jnp.where((i == pl.num_programs(0) - 1) & (kcol >= 2 * A_BLOCK), NEG, 0.0)
    nblk = tq // A_BLOCK
    penalty = []
    for h in range(A_Q_HEADS):
        slope = 2.0 ** (-(8.0 / A_Q_HEADS) * (h + 1.0))
        base = jnp.where(in_window, -slope * absrel_f, NEG)
        penalty.append({0: base + edge_first, nblk - 1: base + edge_last, None: base})
    units = [(j, b) for j in range(A_KV_HEADS) for b in range(nblk)]
    kcat, vcat = [], []
    for j in range(A_KV_HEADS):
        ks = slice(LANES * j, LANES * (j + 1))
        kcat.append(jnp.concatenate([kp_ref[tq - A_BLOCK:, ks], kc_ref[:, ks], kn_ref[:A_BLOCK, ks]], axis=0))
        vcat.append(jnp.concatenate([vp_ref[tq - A_BLOCK:, ks], vc_ref[:, ks], vn_ref[:A_BLOCK, ks]], axis=0))
    scores = []
    for j, b in units:
        rows = slice(A_BLOCK * b, A_BLOCK * (b + 1))
        lhs = []
        for e in range(A_GROUP):
            pair = q_ref[rows, LANES * (2 * j + e // 2):LANES * (2 * j + e // 2 + 1)]
            lhs.append(jnp.where(low_half if e % 2 == 0 else ~low_half, pair, jnp.zeros_like(pair)))
        scores.append(_dot_nt(jnp.concatenate(lhs, axis=0), kcat[j][A_BLOCK * b:A_BLOCK * (b + 3)]))
    probs, invs = [], []
    for (j, b), s_all in zip(units, scores):
        p_unit, inv_unit = [], []
        for e in range(A_GROUP):
            h = A_GROUP * j + e
            sk = sink_ref[h]
            pen = penalty[h]
            s = s_all[A_BLOCK * e:A_BLOCK * (e + 1)] + pen.get(b, pen[None])
            m = jnp.maximum(jnp.max(s, axis=-1, keepdims=True), sk)
            p = jnp.exp(s - m)
            inv_unit.append(1.0 / (jnp.sum(p, axis=-1, keepdims=True) + jnp.exp(sk - m)))
            p_unit.append(p.astype(BF16))
        probs.append(jnp.concatenate(p_unit, axis=0))
        invs.append(jnp.concatenate(inv_unit, axis=0))
    for (j, b), p_all, inv in zip(units, probs, invs):
        rows = slice(A_BLOCK * b, A_BLOCK * (b + 1))
        o = _dot(p_all, vcat[j][A_BLOCK * b:A_BLOCK * (b + 3)]) * inv
        for pr in range(A_GROUP // 2):
            lo = o[A_BLOCK * 2 * pr:A_BLOCK * (2 * pr + 1)]
            hi = o[A_BLOCK * (2 * pr + 1):A_BLOCK * (2 * pr + 2)]
            col = LANES * (2 * j + pr)
            o_ref[rows, col:col + LANES] = jnp.where(low_half, lo, hi).astype(BF16)


def _window_attn(sink, qa, ka2, va2, tq=512):
    s = qa.shape[0]
    nb = s // tq
    prev = lambda i: (jnp.maximum(i - 1, 0), 0)
    cur = lambda i: (i, 0)
    nxt = lambda i: (jnp.minimum(i + 1, nb - 1), 0)
    kv = lambda f: pl.BlockSpec((tq, 2 * LANES), f)
    return pl.pallas_call(
        functools.partial(_window_kernel, tq=tq),
        grid=(nb,),
        in_specs=[pl.BlockSpec(memory_space=pltpu.SMEM),
                  pl.BlockSpec((tq, 512), cur), kv(prev), kv(cur), kv(nxt), kv(prev), kv(cur), kv(nxt)],
        out_specs=pl.BlockSpec((tq, 512), cur),
        out_shape=jax.ShapeDtypeStruct((s, 512), BF16),
        compiler_params=_params("parallel"),
        name="window_gqa",
    )(sink, qa, ka2, ka2, ka2, va2, va2, va2)


NA_ROWS_PER_STEP = 8
NA_TOK = NA_ROWS_PER_STEP * GRID_W
NA_KEYS = NA_KH * GRID_W


def _na_kernel(q_ref, kp_ref, kc_ref, kn_ref, vp_ref, vc_ref, vn_ref, bias_ref, o_ref, kcat, vcat):
    i = pl.program_id(0)
    kcat[0:NA_TOK] = kp_ref[...]
    kcat[NA_TOK:2 * NA_TOK] = kc_ref[...]
    kcat[2 * NA_TOK:3 * NA_TOK] = kn_ref[...]
    vcat[0:NA_TOK] = vp_ref[...]
    vcat[NA_TOK:2 * NA_TOK] = vc_ref[...]
    vcat[2 * NA_TOK:3 * NA_TOK] = vn_ref[...]
    lane = lax.broadcasted_iota(jnp.int32, (1, LANES), 1)
    low_half = lane < HEAD_DIM

    def row_body(rr, carry):
        r = i * NA_ROWS_PER_STEP + rr
        rstart = jnp.clip(r - NA_KH // 2, 0, GRID_ROWS - NA_KH)
        start = pl.multiple_of((rstart - (i - 1) * NA_ROWS_PER_STEP) * GRID_W, GRID_W)
        ri0 = rstart - r + NA_KH - 1
        qrows = pl.ds(pl.multiple_of(rr * GRID_W, GRID_W), GRID_W)
        npair = B_HEADS // 2
        scores = []
        for p in range(npair):
            cols = slice(LANES * p, LANES * (p + 1))
            qp = q_ref[qrows, cols]
            lhs = jnp.concatenate([jnp.where(low_half, qp, jnp.zeros_like(qp)),
                                   jnp.where(low_half, jnp.zeros_like(qp), qp)], axis=0)
            scores.append(_dot_nt(lhs, kcat[pl.ds(start, NA_KEYS), cols]))
        probs, invs = [], []
        for p in range(npair):
            bias = jnp.concatenate(
                [jnp.concatenate([bias_ref[2 * p + e, ri0 + 2 * m] for m in range(NA_KH // 2)], axis=1)
                 for e in range(2)], axis=0)
            s = scores[p] + bias
            pexp = jnp.exp(s - jnp.max(s, axis=-1, keepdims=True))
            invs.append(1.0 / jnp.sum(pexp, axis=-1, keepdims=True))
            probs.append(pexp.astype(BF16))
        for p in range(npair):
            cols = slice(LANES * p, LANES * (p + 1))
            o = _dot(probs[p], vcat[pl.ds(start, NA_KEYS), cols]) * invs[p]
            o_ref[qrows, cols] = jnp.where(low_half, o[:GRID_W], o[GRID_W:]).astype(BF16)
        return carry

    lax.fori_loop(0, NA_ROWS_PER_STEP, row_body, 0, unroll=True)


def _na_bias_table(rpb):
    qc = np.arange(GRID_W)[:, None]
    kc = np.arange(GRID_W)[None, :]
    cstart = np.clip(qc - NA_KW // 2, 0, GRID_W - NA_KW)
    col_valid = (kc >= cstart) & (kc < cstart + NA_KW)
    col_idx = np.clip(kc - qc + NA_KW - 1, 0, 2 * NA_KW - 2)
    t = jnp.where(col_valid[None, None], rpb.astype(F32)[:, :, col_idx], NEG)
    return jnp.concatenate([t[:, :-1], t[:, 1:]], axis=-1)


def _na_attn(qb, kb, vb, bias):
    s = qb.shape[0]
    nb = s // NA_TOK
    prev = lambda i: (jnp.maximum(i - 1, 0), 0)
    cur = lambda i: (i, 0)
    nxt = lambda i: (jnp.minimum(i + 1, nb - 1), 0)
    blk = lambda f: pl.BlockSpec((NA_TOK, 512), f)
    return pl.pallas_call(
        _na_kernel,
        grid=(nb,),
        in_specs=[blk(cur), blk(prev), blk(cur), blk(nxt), blk(prev), blk(cur), blk(nxt),
                  _const_spec(bias.shape)],
        out_specs=blk(cur),
        out_shape=jax.ShapeDtypeStruct((s, 512), BF16),
        scratch_shapes=[pltpu.VMEM((3 * NA_TOK, 512), BF16), pltpu.VMEM((3 * NA_TOK, 512), BF16)],
        compiler_params=_params("parallel"),
        name="neighbourhood_attn",
    )(qb, kb, kb, kb, vb, vb, vb, bias)


FF_CHUNK = 1024


def _outproj_mlp_kernel(*refs, n_mix):
    x_ref = refs[0]
    mix_refs = refs[1:1 + n_mix]
    wout_ref, g_ref, w1_ref, w2_ref, o_ref = refs[1 + n_mix:]
    mix = jnp.concatenate([m_ref[...] for m_ref in mix_refs], axis=1)
    x1 = x_ref[...] + _dot(mix, wout_ref[...])
    hb = _rmsnorm_rows(x1, g_ref[...]).astype(BF16)
    acc = x1
    for c in range(0, D_FF, FF_CHUNK):
        a = jnp.maximum(_dot(hb, w1_ref[:, c:c + FF_CHUNK]), 0.0)
        acc = acc + _dot((a * a).astype(BF16), w2_ref[c:c + FF_CHUNK, :])
    o_ref[...] = acc


def _outproj_mlp(x, mixes, wout, g, w1, w2, tm=512, name="outproj_mlp"):
    s = x.shape[0]
    row = lambda i: (i, 0)
    in_specs = [pl.BlockSpec((tm, D_MODEL), row)]
    in_specs += [pl.BlockSpec((tm, m.shape[1]), row) for m in mixes]
    in_specs += [_const_spec(wout.shape), _const_spec((1, D_MODEL)), _const_spec(w1.shape), _const_spec(w2.shape)]
    return pl.pallas_call(
        functools.partial(_outproj_mlp_kernel, n_mix=len(mixes)),
        grid=(s // tm,),
        in_specs=in_specs,
        out_specs=pl.BlockSpec((tm, D_MODEL), row),
        out_shape=jax.ShapeDtypeStruct((s, D_MODEL), F32),
        compiler_params=_params("parallel"),
        name=name,
    )(x, *mixes, wout, g, w1, w2)


PROJ_CHUNK = 1024


def _inproj_odd_kernel(x_ref, g_ref, wq_ref, wkt_ref, wv_ref, wg_ref, q_ref, kt_ref, v_ref, gate_ref):
    xb = _rmsnorm_rows(x_ref[...], g_ref[...]).astype(BF16)
    q_ref[...] = _dot(xb, wq_ref[...]).astype(BF16)
    kt_ref[...] = (_dot_nt(wkt_ref[...], xb) * RET_DK ** -0.5).astype(BF16)
    for w_ref, ref in ((wv_ref, v_ref), (wg_ref, gate_ref)):
        for off in range(0, ODD_MIX, PROJ_CHUNK):
            ref[:, off:off + PROJ_CHUNK] = _dot(xb, w_ref[:, off:off + PROJ_CHUNK]).astype(BF16)


def _inproj_odd(x, g, wq, wkt, wv, wg, tm=512):
    s = x.shape[0]
    row = lambda i: (i, 0)
    return pl.pallas_call(
        _inproj_odd_kernel,
        grid=(s // tm,),
        in_specs=[pl.BlockSpec((tm, D_MODEL), row), _const_spec((1, D_MODEL)),
                  _const_spec(wq.shape), _const_spec(wkt.shape), _const_spec(wv.shape), _const_spec(wg.shape)],
        out_specs=[pl.BlockSpec((tm, D_MODEL), row), pl.BlockSpec((D_MODEL, tm), lambda i: (0, i)),
                   pl.BlockSpec((tm, ODD_MIX), row), pl.BlockSpec((tm, ODD_MIX), row)],
        out_shape=[jax.ShapeDtypeStruct((s, D_MODEL), BF16), jax.ShapeDtypeStruct((D_MODEL, s), BF16),
                   jax.ShapeDtypeStruct((s, ODD_MIX), BF16), jax.ShapeDtypeStruct((s, ODD_MIX), BF16)],
        compiler_params=_params("parallel"),
        name="inproj_odd",
    )(x, g, wq, wkt, wv, wg)


RET_CHUNKS_PER_STEP = 8
RET_TOK = RET_CHUNKS_PER_STEP * RET_CHUNK
RET_STEPS = SEQ // RET_TOK


RET_NCHUNK = SEQ // RET_CHUNK


def _retention_kernel(lg_ref, q_ref, kt_ref, v_ref, gate_ref, gn_ref, o_ref, rf_all, state):
    h = pl.program_id(0)
    t = pl.program_id(1)
    second = t >= RET_STEPS
    blk = jnp.where(second, 2 * RET_STEPS - 1 - t, t)
    lgf = lg_ref[0, h]
    lgb = lg_ref[1, h]
    tok_lane = lax.broadcasted_iota(jnp.int32, (1, RET_CHUNK), 1).astype(F32)
    tok_row = lax.broadcasted_iota(jnp.int32, (RET_CHUNK, 1), 0).astype(F32)
    chunk_len = jnp.full((1, 1), RET_CHUNK, F32)

    @pl.when((t == 0) | (t == RET_STEPS))
    def _():
        state[...] = jnp.zeros_like(state)

    @pl.when(jnp.logical_not(second))
    def _():
        zeta = jnp.exp(lgf * (RET_CHUNK - 1.0 - tok_lane))
        decay = jnp.exp(lgf * chunk_len)
        r = state[...]
        for c in range(RET_CHUNKS_PER_STEP):
            tok = slice(c * RET_CHUNK, (c + 1) * RET_CHUNK)
            kz = (kt_ref[:, tok].astype(F32) * zeta).astype(BF16)
            rf_all[blk * RET_CHUNKS_PER_STEP + c] = r.astype(BF16)
            r = r * decay + _dot(kz, v_ref[tok, :])
        state[...] = r

    @pl.when(second)
    def _():
        diff = (lax.broadcasted_iota(jnp.int32, (RET_CHUNK, RET_CHUNK), 0)
                - lax.broadcasted_iota(jnp.int32, (RET_CHUNK, RET_CHUNK), 1)).astype(F32)
        dsum = jnp.where(diff >= 0, jnp.exp(lgf * jnp.maximum(diff, 0.0)), jnp.exp(lgb * jnp.maximum(-diff, 0.0)))
        cross_f = jnp.exp(lgf * (tok_row + 1.0))
        cross_b = jnp.exp(lgb * (RET_CHUNK - tok_row))
        zeta = jnp.exp(lgb * tok_lane)
        decay = jnp.exp(lgb * chunk_len)
        gn = gn_ref[...]
        r = state[...]
        for c in reversed(range(RET_CHUNKS_PER_STEP)):
            tok = slice(c * RET_CHUNK, (c + 1) * RET_CHUNK)
            q = q_ref[tok, :]
            kt = kt_ref[:, tok]
            v = v_ref[tok, :]
            a = (_dot(q, kt) * dsum).astype(BF16)
            qf = q.astype(F32)
            lhs = jnp.concatenate([a, (qf * cross_f).astype(BF16), (qf * cross_b).astype(BF16)], axis=1)
            rhs = jnp.concatenate([v, rf_all[blk * RET_CHUNKS_PER_STEP + c], r.astype(BF16)], axis=0)
            y = _dot(lhs, rhs)
            r = r * decay + _dot((kt.astype(F32) * zeta).astype(BF16), v)
            mu = jnp.mean(y, axis=-1, keepdims=True)
            d = y - mu
            var = jnp.mean(d * d, axis=-1, keepdims=True)
            yn = d * lax.rsqrt(var + GN_EPS) * gn
            g = gate_ref[tok, :].astype(F32)
            o_ref[tok, :] = (g * (1.0 / (1.0 + jnp.exp(-g))) * yn).astype(BF16)
        state[...] = r


def _retention(lg, q, kt, v, gate, gn):
    s = q.shape[0]

    def blk_of(t):
        return jnp.where(t >= RET_STEPS, 2 * RET_STEPS - 1 - t, t)

    def second_blk(t):
        return jnp.where(t >= RET_STEPS, 2 * RET_STEPS - 1 - t, RET_STEPS - 1)

    return pl.pallas_call(
        _retention_kernel,
        grid=(RET_HEADS, 2 * RET_STEPS),
        in_specs=[pl.BlockSpec(memory_space=pltpu.SMEM),
                  pl.BlockSpec((RET_TOK, RET_DK), lambda h, t: (second_blk(t), h)),
                  pl.BlockSpec((RET_DK, RET_TOK), lambda h, t: (h, blk_of(t))),
                  pl.BlockSpec((RET_TOK, RET_DV), lambda h, t: (blk_of(t), h)),
                  pl.BlockSpec((RET_TOK, RET_DV), lambda h, t: (second_blk(t), h)),
                  pl.BlockSpec((1, RET_DV), lambda h, t: (0, h))],
        out_specs=pl.BlockSpec((RET_TOK, RET_DV), lambda h, t: (second_blk(t), h)),
        out_shape=jax.ShapeDtypeStruct((s, ODD_MIX), BF16),
        scratch_shapes=[pltpu.VMEM((RET_NCHUNK, RET_DK, RET_DV), BF16), pltpu.VMEM((RET_DK, RET_DV), F32)],
        compiler_params=_params("arbitrary", "arbitrary"),
        name="retention",
    )(lg, q, kt, v, gate, gn)


def _even_proj_weights(w_in, qn_a, kn_a, qn_b, kn_b):
    qa, ka, va, qb, kb, vb = jnp.split(w_in, np.cumsum([512, 128, 128, 512, 512]).tolist(), axis=-1)
    dup = lambda w: jnp.concatenate([w[:, :HEAD_DIM]] * 2 + [w[:, HEAD_DIM:]] * 2, axis=-1)
    w = jnp.concatenate([qa, dup(ka), dup(va), qb, kb, vb], axis=-1).astype(BF16)
    scale = HEAD_DIM ** -0.5
    gain = jnp.concatenate([jnp.tile(qn_a, A_Q_HEADS) * scale, jnp.tile(kn_a, 4), jnp.ones((256,), F32),
                            jnp.tile(qn_b, B_HEADS) * scale, jnp.tile(kn_b, B_HEADS),
                            jnp.ones((512,), F32)]).astype(F32)
    return w, gain[None, :]


def kernel(x, attn_norm_e, w_in_e, q_norm_a, k_norm_a, sink_a, q_norm_b, k_norm_b, rpb_b, w_out_e,
           ret_norm_o, w_in_o, decay_fwd_o, decay_bwd_o, ret_gn_o, w_out_o,
           mlp_norm, w_mlp_in, w_mlp_out):
    b, s, d = x.shape
    xs = x.reshape(b * s, d)
    depth = mlp_norm.shape[0]
    for layer in range(depth):
        i = layer // 2
        w1 = w_mlp_in[layer].astype(BF16)
        w2 = w_mlp_out[layer].astype(BF16)
        g_mlp = mlp_norm[layer][None, :]
        if layer % 2 == 0:
            w, gain = _even_proj_weights(w_in_e[i], q_norm_a[i], k_norm_a[i], q_norm_b[i], k_norm_b[i])
            qa, ka2, va2, qb, kb, vb = _inproj_even(xs, attn_norm_e[i][None, :], w, gain)
            oa = _window_attn(sink_a[i].astype(F32), qa, ka2, va2)
            ob = _na_attn(qb, kb, vb, _na_bias_table(rpb_b[i]))
            xs = _outproj_mlp(xs, (oa, ob), w_out_e[i].astype(BF16), g_mlp, w1, w2, name="outproj_mlp_even")
        else:
            wq, wk, wv, wg = jnp.split(w_in_o[i].astype(BF16), [1024, 2048, 4096], axis=-1)
            q, kt, v, gate = _inproj_odd(xs, ret_norm_o[i][None, :], wq, wk.T, wv, wg)
            lg = jnp.stack([jax.nn.log_sigmoid(decay_fwd_o[i].astype(F32)),
                            jax.nn.log_sigmoid(decay_bwd_o[i].astype(F32))])
            mix = _retention(lg, q, kt, v, gate, ret_gn_o[i].astype(F32)[None, :])
            xs = _outproj_mlp(xs, (mix,), w_out_o[i].astype(BF16), g_mlp, w1, w2, name="outproj_mlp_odd")
    return xs.reshape(b, s, d)
```

```python
import functools

import numpy as np
import jax
import jax.numpy as jnp
from jax import lax
from jax.experimental import pallas as pl
from jax.experimental.pallas import tpu as pltpu

D_MODEL = 1024
SEQ = 16384
HEAD_DIM = 64
A_Q_HEADS = 8
A_KV_HEADS = 2
A_GROUP = A_Q_HEADS // A_KV_HEADS
A_WINDOW = 128
A_BLOCK = 128
B_HEADS = 8
GRID_W = 64
GRID_ROWS = SEQ // GRID_W
NA_KH = 8
NA_KW = 16
RET_HEADS = 8
RET_DK = D_MODEL // RET_HEADS
RET_DV = 2 * D_MODEL // RET_HEADS
RET_CHUNK = 128
D_FF = 4 * D_MODEL
ODD_MIX = RET_HEADS * RET_DV
RMS_EPS = 1e-6
GN_EPS = 1e-5

LANES = 128
MXU_N = 256
PROJ_CHUNK = 1024
NEG = -1e30
VMEM_LIMIT = 60 * 1024 * 1024

F32 = jnp.float32
BF16 = jnp.bfloat16

E_QA, E_KA, E_VA, E_QB, E_KB, E_VB = 0, 512, 768, 1024, 1536, 2048
E_COLS = 2560


def _const_spec(shape):
    return pl.BlockSpec(shape, lambda *_: (0,) * len(shape), pipeline_mode=pl.Buffered(1))


def _params(*sem):
    return pltpu.CompilerParams(dimension_semantics=sem, vmem_limit_bytes=VMEM_LIMIT)


def _rmsnorm_rows(x, g):
    return x * lax.rsqrt(jnp.mean(x * x, axis=-1, keepdims=True) + RMS_EPS) * g


def _silu(g):
    return g * (1.0 / (1.0 + jnp.exp(-g)))


def _dot(a, b):
    return jnp.dot(a, b, preferred_element_type=F32)


def _dot_nt(a, b):
    return lax.dot_general(a, b, (((1,), (1,)), ((), ())), preferred_element_type=F32)


def _inproj_even_kernel(x_ref, g_ref, w_ref, gain_ref, qa_ref, ka_ref, va_ref, qb_ref, kb_ref, vb_ref):
    xb = _rmsnorm_rows(x_ref[...], g_ref[...]).astype(BF16)
    r = lax.broadcasted_iota(jnp.int32, (MXU_N, MXU_N), 0) // HEAD_DIM
    c = lax.broadcasted_iota(jnp.int32, (MXU_N, MXU_N), 1) // HEAD_DIM
    head_mean = jnp.where(r == c, 1.0 / HEAD_DIM, 0.0).astype(BF16)
    outs = ((qa_ref, E_QA, 512, True), (ka_ref, E_KA, 256, True), (va_ref, E_VA, 256, False),
            (qb_ref, E_QB, 512, True), (kb_ref, E_KB, 512, True), (vb_ref, E_VB, 512, False))
    chunks = [(ref, start + off, off, normed) for ref, start, width, normed in outs
              for off in range(0, width, MXU_N)]
    ys = [_dot(xb, w_ref[:, col:col + MXU_N]) for _, col, _, _ in chunks]
    ms = [_dot((y * y).astype(BF16), head_mean) if normed else None
          for y, (_, _, _, normed) in zip(ys, chunks)]
    for y, m, (ref, col, off, normed) in zip(ys, ms, chunks):
        if normed:
            y = y * lax.rsqrt(m + RMS_EPS) * gain_ref[:, col:col + MXU_N]
        ref[:, off:off + MXU_N] = y.astype(BF16)


def _inproj_even(x, g, w, gain, tm=512):
    s = x.shape[0]
    widths = (512, 256, 256, 512, 512, 512)
    return pl.pallas_call(
        _inproj_even_kernel,
        grid=(s // tm,),
        in_specs=[pl.BlockSpec((tm, D_MODEL), lambda i: (i, 0)),
                  _const_spec((1, D_MODEL)), _const_spec((D_MODEL, E_COLS)), _const_spec((1, E_COLS))],
        out_specs=[pl.BlockSpec((tm, wd), lambda i: (i, 0)) for wd in widths],
        out_shape=[jax.ShapeDtypeStruct((s, wd), BF16) for wd in widths],
        compiler_params=_params("parallel"),
        name="inproj_even",
    )(x, g, w, gain)


def _window_kernel(sink_ref, q_ref, kp_ref, kc_ref, kn_ref, vp_ref, vc_ref, vn_ref, o_ref, *, tq):
    i = pl.program_id(0)
    lane = lax.broadcasted_iota(jnp.int32, (1, LANES), 1)
    low_half = lane < HEAD_DIM
    qi = lax.broadcasted_iota(jnp.int32, (A_BLOCK, 3 * A_BLOCK), 0)
    kj = lax.broadcasted_iota(jnp.int32, (A_BLOCK, 3 * A_BLOCK), 1)
    absrel = jnp.abs(kj - A_BLOCK - qi)
    in_window = absrel <= A_WINDOW
    absrel_f = absrel.astype(F32)
    kcol = lax.broadcasted_iota(jnp.int32, (1, 3 * A_BLOCK), 1)
    edge_first = jnp.where((i == 0) & (kcol < A_BLOCK), NEG, 0.0)
    edge_last = jnp.where((i == pl.num_programs(0) - 1) & (kcol >= 2 * A_BLOCK), NEG, 0.0)
    nblk = tq // A_BLOCK
    penalty = []
    for h in range(A_Q_HEADS):
        slope = 2.0 ** (-(8.0 / A_Q_HEADS) * (h + 1.0))
        base = jnp.where(in_window, -slope * absrel_f, NEG)
        penalty.append({0: base + edge_first, nblk - 1: base + edge_last, None: base})
    units = [(j, b) for j in range(A_KV_HEADS) for b in range(nblk)]
    kcat, vcat = [], []
    for j in range(A_KV_HEADS):
        ks = slice(LANES * j, LANES * (j + 1))
        kcat.append(jnp.concatenate([kp_ref[tq - A_BLOCK:, ks], kc_ref[:, ks], kn_ref[:A_BLOCK, ks]], axis=0))
        vcat.append(jnp.concatenate([vp_ref[tq - A_BLOCK:, ks], vc_ref[:, ks], vn_ref[:A_BLOCK, ks]], axis=0))
    scores = []
    for j, b in units:
        rows = slice(A_BLOCK * b, A_BLOCK * (b + 1))
        lhs = []
        for e in range(A_GROUP):
            pair = q_ref[rows, LANES * (2 * j + e // 2):LANES * (2 * j + e // 2 + 1)]
            lhs.append(jnp.where(low_half if e % 2 == 0 else ~low_half, pair, jnp.zeros_like(pair)))
        scores.append(_dot_nt(jnp.concatenate(lhs, axis=0), kcat[j][A_BLOCK * b:A_BLOCK * (b + 3)]))
    probs, invs = [], []
    for (j, b), s_all in zip(units, scores):
        p_unit, inv_unit = [], []
        for e in range(A_GROUP):
            h = A_GROUP * j + e
            sk = sink_ref[h]
            pen = penalty[h]
            s = s_all[A_BLOCK * e:A_BLOCK * (e + 1)] + pen.get(b, pen[None])
            m = jnp.maximum(jnp.max(s, axis=-1, keepdims=True), sk)
            p = jnp.exp(s - m)
            inv_unit.append(1.0 / (jnp.sum(p, axis=-1, keepdims=True) + jnp.exp(sk - m)))
            p_unit.append(p.astype(BF16))
        probs.append(jnp.concatenate(p_unit, axis=0))
        invs.append(jnp.concatenate(inv_unit, axis=0))
    for (j, b), p_all, inv in zip(units, probs, invs):
        rows = slice(A_BLOCK * b, A_BLOCK * (b + 1))
        o = _dot(p_all, vcat[j][A_BLOCK * b:A_BLOCK * (b + 3)]) * inv
        for pr in range(A_GROUP // 2):
            lo = o[A_BLOCK * 2 * pr:A_BLOCK * (2 * pr + 1)]
            hi = o[A_BLOCK * (2 * pr + 1):A_BLOCK * (2 * pr + 2)]
            col = LANES * (2 * j + pr)
            o_ref[rows, col:col + LANES] = jnp.where(low_half, lo, hi).astype(BF16)


def _window_attn(sink, qa, ka2, va2, tq=512):
    s = qa.shape[0]
    nb = s // tq
    prev = lambda i: (jnp.maximum(i - 1, 0), 0)
    cur = lambda i: (i, 0)
    nxt = lambda i: (jnp.minimum(i + 1, nb - 1), 0)
    kv = lambda f: pl.BlockSpec((tq, 2 * LANES), f)
    return pl.pallas_call(
        functools.partial(_window_kernel, tq=tq),
        grid=(nb,),
        in_specs=[pl.BlockSpec(memory_space=pltpu.SMEM),
                  pl.BlockSpec((tq, 512), cur), kv(prev), kv(cur), kv(nxt), kv(prev), kv(cur), kv(nxt)],
        out_specs=pl.BlockSpec((tq, 512), cur),
        out_shape=jax.ShapeDtypeStruct((s, 512), BF16),
        compiler_params=_params("parallel"),
        name="window_gqa",
    )(sink, qa, ka2, ka2, ka2, va2, va2, va2)


NA_ROWS_PER_STEP = 8
NA_TOK = NA_ROWS_PER_STEP * GRID_W
NA_KEYS = NA_KH * GRID_W


def _na_kernel(q_ref, kp_ref, kc_ref, kn_ref, vp_ref, vc_ref, vn_ref, bias_ref, o_ref, kcat, vcat):
    i = pl.program_id(0)
    kcat[0:NA_TOK] = kp_ref[...]
    kcat[NA_TOK:2 * NA_TOK] = kc_ref[...]
    kcat[2 * NA_TOK:3 * NA_TOK] = kn_ref[...]
    vcat[0:NA_TOK] = vp_ref[...]
    vcat[NA_TOK:2 * NA_TOK] = vc_ref[...]
    vcat[2 * NA_TOK:3 * NA_TOK] = vn_ref[...]
    lane = lax.broadcasted_iota(jnp.int32, (1, LANES), 1)
    low_half = lane < HEAD_DIM

    def row_body(rr, carry):
        r = i * NA_ROWS_PER_STEP + rr
        rstart = jnp.clip(r - NA_KH // 2, 0, GRID_ROWS - NA_KH)
        start = pl.multiple_of((rstart - (i - 1) * NA_ROWS_PER_STEP) * GRID_W, GRID_W)
        ri0 = rstart - r + NA_KH - 1
        qrows = pl.ds(pl.multiple_of(rr * GRID_W, GRID_W), GRID_W)
        npair = B_HEADS // 2
        scores = []
        for p in range(npair):
            cols = slice(LANES * p, LANES * (p + 1))
            qp = q_ref[qrows, cols]
            lhs = jnp.concatenate([jnp.where(low_half, qp, jnp.zeros_like(qp)),
                                   jnp.where(low_half, jnp.zeros_like(qp), qp)], axis=0)
            scores.append(_dot_nt(lhs, kcat[pl.ds(start, NA_KEYS), cols]))
        probs, invs = [], []
        for p in range(npair):
            bias = jnp.concatenate(
                [jnp.concatenate([bias_ref[2 * p + e, ri0 + 2 * m] for m in range(NA_KH // 2)], axis=1)
                 for e in range(2)], axis=0)
            s = scores[p] + bias
            pexp = jnp.exp(s - jnp.max(s, axis=-1, keepdims=True))
            invs.append(1.0 / jnp.sum(pexp, axis=-1, keepdims=True))
            probs.append(pexp.astype(BF16))
        for p in range(npair):
            cols = slice(LANES * p, LANES * (p + 1))
            o = _dot(probs[p], vcat[pl.ds(start, NA_KEYS), cols]) * invs[p]
            o_ref[qrows, cols] = jnp.where(low_half, o[:GRID_W], o[GRID_W:]).astype(BF16)
        return carry

    lax.fori_loop(0, NA_ROWS_PER_STEP, row_body, 0, unroll=True)


def _na_bias_table(rpb):
    qc = np.arange(GRID_W)[:, None]
    kc = np.arange(GRID_W)[None, :]
    cstart = np.clip(qc - NA_KW // 2, 0, GRID_W - NA_KW)
    col_valid = (kc >= cstart) & (kc < cstart + NA_KW)
    col_idx = np.clip(kc - qc + NA_KW - 1, 0, 2 * NA_KW - 2)
    onehot = (col_idx.reshape(1, -1) == np.arange(2 * NA_KW - 1)[:, None]).astype(np.float32)
    n_ri = 2 * NA_KH - 1
    t = jnp.dot(rpb.astype(F32).reshape(B_HEADS * n_ri, 2 * NA_KW - 1), jnp.asarray(onehot),
                precision=lax.Precision.HIGHEST).reshape(B_HEADS, n_ri, GRID_W, GRID_W)
    t = jnp.where(col_valid[None, None], t, NEG)
    return jnp.concatenate([t[:, :-1], t[:, 1:]], axis=-1)


def _na_attn(qb, kb, vb, bias):
    s = qb.shape[0]
    nb = s // NA_TOK
    prev = lambda i: (jnp.maximum(i - 1, 0), 0)
    cur = lambda i: (i, 0)
    nxt = lambda i: (jnp.minimum(i + 1, nb - 1), 0)
    blk = lambda f: pl.BlockSpec((NA_TOK, 512), f)
    return pl.pallas_call(
        _na_kernel,
        grid=(nb,),
        in_specs=[blk(cur), blk(prev), blk(cur), blk(nxt), blk(prev), blk(cur), blk(nxt),
                  _const_spec(bias.shape)],
        out_specs=blk(cur),
        out_shape=jax.ShapeDtypeStruct((s, 512), BF16),
        scratch_shapes=[pltpu.VMEM((3 * NA_TOK, 512), BF16), pltpu.VMEM((3 * NA_TOK, 512), BF16)],
        compiler_params=_params("parallel"),
        name="neighbourhood_attn",
    )(qb, kb, kb, kb, vb, vb, vb, bias)


FF_CHUNK = 1024


def _outproj_mlp_kernel(*refs, n_mix, gated):
    x_ref = refs[0]
    mix_refs = refs[1:1 + n_mix]
    rest = refs[1 + n_mix:]
    mix = jnp.concatenate([m_ref[...] for m_ref in mix_refs], axis=1)
    if gated:
        mix = (mix.astype(F32) * rest[0][...].astype(F32)).astype(BF16)
        rest = rest[1:]
    wout_ref, g_ref, w1_ref, w2_ref, o_ref = rest
    x1 = x_ref[...] + _dot(mix, wout_ref[...])
    hb = _rmsnorm_rows(x1, g_ref[...]).astype(BF16)
    acc = x1
    for c in range(0, D_FF, FF_CHUNK):
        a = jnp.maximum(_dot(hb, w1_ref[:, c:c + FF_CHUNK]), 0.0)
        acc = acc + _dot((a * a).astype(BF16), w2_ref[c:c + FF_CHUNK, :])
    o_ref[...] = acc


def _outproj_mlp(x, mixes, gate, wout, g, w1, w2, tm=512, name="outproj_mlp"):
    s = x.shape[0]
    row = lambda i: (i, 0)
    acts = tuple(mixes) + (() if gate is None else (gate,))
    in_specs = [pl.BlockSpec((tm, D_MODEL), row)]
    in_specs += [pl.BlockSpec((tm, m.shape[1]), row) for m in acts]
    in_specs += [_const_spec(wout.shape), _const_spec((1, D_MODEL)), _const_spec(w1.shape), _const_spec(w2.shape)]
    return pl.pallas_call(
        functools.partial(_outproj_mlp_kernel, n_mix=len(mixes), gated=gate is not None),
        grid=(s // tm,),
        in_specs=in_specs,
        out_specs=pl.BlockSpec((tm, D_MODEL), row),
        out_shape=jax.ShapeDtypeStruct((s, D_MODEL), F32),
        compiler_params=_params("parallel"),
        name=name,
    )(x, *acts, wout, g, w1, w2)


def _inproj_odd_kernel(x_ref, g_ref, wq_ref, wkt_ref, wv_ref, wg_ref, q_ref, kt_ref, v_ref, gate_ref):
    xb = _rmsnorm_rows(x_ref[...], g_ref[...]).astype(BF16)
    q_ref[...] = _dot(xb, wq_ref[...]).astype(BF16)
    kt_ref[...] = (_dot_nt(wkt_ref[...], xb) * RET_DK ** -0.5).astype(BF16)
    for off in range(0, ODD_MIX, PROJ_CHUNK):
        v_ref[:, off:off + PROJ_CHUNK] = _dot(xb, wv_ref[:, off:off + PROJ_CHUNK]).astype(BF16)
    for off in range(0, ODD_MIX, PROJ_CHUNK):
        gate_ref[:, off:off + PROJ_CHUNK] = _silu(_dot(xb, wg_ref[:, off:off + PROJ_CHUNK])).astype(BF16)


def _inproj_odd(x, g, wq, wkt, wv, wg, tm=512):
    s = x.shape[0]
    row = lambda i: (i, 0)
    return pl.pallas_call(
        _inproj_odd_kernel,
        grid=(s // tm,),
        in_specs=[pl.BlockSpec((tm, D_MODEL), row), _const_spec((1, D_MODEL)),
                  _const_spec(wq.shape), _const_spec(wkt.shape), _const_spec(wv.shape), _const_spec(wg.shape)],
        out_specs=[pl.BlockSpec((tm, D_MODEL), row), pl.BlockSpec((D_MODEL, tm), lambda i: (0, i)),
                   pl.BlockSpec((tm, ODD_MIX), row), pl.BlockSpec((tm, ODD_MIX), row)],
        out_shape=[jax.ShapeDtypeStruct((s, D_MODEL), BF16), jax.ShapeDtypeStruct((D_MODEL, s), BF16),
                   jax.ShapeDtypeStruct((s, ODD_MIX), BF16), jax.ShapeDtypeStruct((s, ODD_MIX), BF16)],
        compiler_params=_params("parallel"),
        name="inproj_odd",
    )(x, g, wq, wkt, wv, wg)


RET_CHUNKS_PER_STEP = 16
RET_TOK = RET_CHUNKS_PER_STEP * RET_CHUNK
RET_STEPS = SEQ // RET_TOK
RET_NCHUNK = SEQ // RET_CHUNK


def _retention_kernel(lg_ref, q_ref, kt_ref, v_ref, gn_ref, o_ref, rf_all, state):
    h = pl.program_id(0)
    t = pl.program_id(1)
    second = t >= RET_STEPS
    blk = jnp.where(second, 2 * RET_STEPS - 1 - t, t)
    lgf = lg_ref[0, h]
    lgb = lg_ref[1, h]
    tok_lane = lax.broadcasted_iota(jnp.int32, (1, RET_CHUNK), 1).astype(F32)
    tok_row = lax.broadcasted_iota(jnp.int32, (RET_CHUNK, 1), 0).astype(F32)
    chunk_len = jnp.full((1, 1), RET_CHUNK, F32)

    @pl.when((t == 0) | (t == RET_STEPS))
    def _():
        state[...] = jnp.zeros_like(state)

    @pl.when(jnp.logical_not(second))
    def _():
        zeta = jnp.exp(lgf * (RET_CHUNK - 1.0 - tok_lane))
        decay = jnp.exp(lgf * chunk_len)
        r = state[...]
        for c in range(RET_CHUNKS_PER_STEP):
            tok = slice(c * RET_CHUNK, (c + 1) * RET_CHUNK)
            kz = (kt_ref[:, tok].astype(F32) * zeta).astype(BF16)
            rf_all[blk * RET_CHUNKS_PER_STEP + c] = r.astype(BF16)
            r = r * decay + _dot(kz, v_ref[tok, :])
        state[...] = r

    @pl.when(second)
    def _():
        diff = (lax.broadcasted_iota(jnp.int32, (RET_CHUNK, RET_CHUNK), 0)
                - lax.broadcasted_iota(jnp.int32, (RET_CHUNK, RET_CHUNK), 1)).astype(F32)
        dsum = jnp.where(diff >= 0, jnp.exp(lgf * jnp.maximum(diff, 0.0)), jnp.exp(lgb * jnp.maximum(-diff, 0.0)))
        cross_f = jnp.exp(lgf * (tok_row + 1.0))
        cross_b = jnp.exp(lgb * (RET_CHUNK - tok_row))
        zeta = jnp.exp(lgb * tok_lane)
        decay = jnp.exp(lgb * chunk_len)
        gn = gn_ref[...]
        r = state[...]
        for c in reversed(range(RET_CHUNKS_PER_STEP)):
            tok = slice(c * RET_CHUNK, (c + 1) * RET_CHUNK)
            q = q_ref[tok, :]
            kt = kt_ref[:, tok]
            v = v_ref[tok, :]
            a = (_dot(q, kt) * dsum).astype(BF16)
            qf = q.astype(F32)
            lhs = jnp.concatenate([a, (qf * cross_f).astype(BF16), (qf * cross_b).astype(BF16)], axis=1)
            rhs = jnp.concatenate([v, rf_all[blk * RET_CHUNKS_PER_STEP + c], r.astype(BF16)], axis=0)
            y = _dot(lhs, rhs)
            r = r * decay + _dot((kt.astype(F32) * zeta).astype(BF16), v)
            mu = jnp.mean(y, axis=-1, keepdims=True)
            d = y - mu
            var = jnp.mean(d * d, axis=-1, keepdims=True)
            o_ref[tok, :] = (d * lax.rsqrt(var + GN_EPS) * gn).astype(BF16)
        state[...] = r


def _retention(lg, q, kt, v, gn):
    s = q.shape[0]

    def blk_of(t):
        return jnp.where(t >= RET_STEPS, 2 * RET_STEPS - 1 - t, t)

    def second_blk(t):
        return jnp.where(t >= RET_STEPS, 2 * RET_STEPS - 1 - t, RET_STEPS - 1)

    return pl.pallas_call(
        _retention_kernel,
        grid=(RET_HEADS, 2 * RET_STEPS),
        in_specs=[pl.BlockSpec(memory_space=pltpu.SMEM),
                  pl.BlockSpec((RET_TOK, RET_DK), lambda h, t: (second_blk(t), h)),
                  pl.BlockSpec((RET_DK, RET_TOK), lambda h, t: (h, blk_of(t))),
                  pl.BlockSpec((RET_TOK, RET_DV), lambda h, t: (blk_of(t), h)),
                  pl.BlockSpec((1, RET_DV), lambda h, t: (0, h))],
        out_specs=pl.BlockSpec((RET_TOK, RET_DV), lambda h, t: (second_blk(t), h)),
        out_shape=jax.ShapeDtypeStruct((s, ODD_MIX), BF16),
        scratch_shapes=[pltpu.VMEM((RET_NCHUNK, RET_DK, RET_DV), BF16), pltpu.VMEM((RET_DK, RET_DV), F32)],
        compiler_params=_params("arbitrary", "arbitrary"),
        name="retention",
    )(lg, q, kt, v, gn)


def _even_proj_weights(w_in, qn_a, kn_a, qn_b, kn_b):
    qa, ka, va, qb, kb, vb = jnp.split(w_in, np.cumsum([512, 128, 128, 512, 512]).tolist(), axis=-1)
    dup = lambda w: jnp.concatenate([w[:, :HEAD_DIM]] * 2 + [w[:, HEAD_DIM:]] * 2, axis=-1)
    w = jnp.concatenate([qa, dup(ka), dup(va), qb, kb, vb], axis=-1).astype(BF16)
    scale = HEAD_DIM ** -0.5
    gain = jnp.concatenate([jnp.tile(qn_a, A_Q_HEADS) * scale, jnp.tile(kn_a, 4), jnp.ones((256,), F32),
                            jnp.tile(qn_b, B_HEADS) * scale, jnp.tile(kn_b, B_HEADS),
                            jnp.ones((512,), F32)]).astype(F32)
    return w, gain[None, :]


def kernel(x, attn_norm_e, w_in_e, q_norm_a, k_norm_a, sink_a, q_norm_b, k_norm_b, rpb_b, w_out_e,
           ret_norm_o, w_in_o, decay_fwd_o, decay_bwd_o, ret_gn_o, w_out_o,
           mlp_norm, w_mlp_in, w_mlp_out):
    b, s, d = x.shape
    xs = x.reshape(b * s, d)
    depth = mlp_norm.shape[0]
    for layer in range(depth):
        i = layer // 2
        w1 = w_mlp_in[layer].astype(BF16)
        w2 = w_mlp_out[layer].astype(BF16)
        g_mlp = mlp_norm[layer][None, :]
        if layer % 2 == 0:
            w, gain = _even_proj_weights(w_in_e[i], q_norm_a[i], k_norm_a[i], q_norm_b[i], k_norm_b[i])
            qa, ka2, va2, qb, kb, vb = _inproj_even(xs, attn_norm_e[i][None, :], w, gain)
            oa = _window_attn(sink_a[i].astype(F32), qa, ka2, va2)
            ob = _na_attn(qb, kb, vb, _na_bias_table(rpb_b[i]))
            xs = _outproj_mlp(xs, (oa, ob), None, w_out_e[i].astype(BF16), g_mlp, w1, w2, name="outproj_mlp_even")
        else:
            wq, wk, wv, wg = jnp.split(w_in_o[i].astype(BF16), [1024, 2048, 4096], axis=-1)
            q, kt, v, gate = _inproj_odd(xs, ret_norm_o[i][None, :], wq, wk.T, wv, wg)
            lg = jnp.stack([jax.nn.log_sigmoid(decay_fwd_o[i].astype(F32)),
                            jax.nn.log_sigmoid(decay_bwd_o[i].astype(F32))])
            yn = _retention(lg, q, kt, v, ret_gn_o[i].astype(F32)[None, :])
            xs = _outproj_mlp(xs, (yn,), gate, w_out_o[i].astype(BF16), g_mlp, w1, w2, name="outproj_mlp_odd")
    return xs.reshape(b, s, d)
```

```python
import functools

import numpy as np
import jax
import jax.numpy as jnp
from jax import lax
from jax.experimental import pallas as pl
from jax.experimental.pallas import tpu as pltpu

D_MODEL = 1024
SEQ = 16384
HEAD_DIM = 64
A_Q_HEADS = 8
A_KV_HEADS = 2
A_GROUP = A_Q_HEADS // A_KV_HEADS
A_WINDOW = 128
A_BLOCK = 128
B_HEADS = 8
GRID_W = 64
GRID_ROWS = SEQ // GRID_W
NA_KH = 8
NA_KW = 16
RET_HEADS = 8
RET_DK = D_MODEL // RET_HEADS
RET_DV = 2 * D_MODEL // RET_HEADS
RET_CHUNK = 256
D_FF = 4 * D_MODEL
ODD_MIX = RET_HEADS * RET_DV
RMS_EPS = 1e-6
GN_EPS = 1e-5

LANES = 128
MXU_N = 256
PROJ_CHUNK = 1024
NEG = -1e30
VMEM_LIMIT = 60 * 1024 * 1024

F32 = jnp.float32
BF16 = jnp.bfloat16

E_QA, E_KA, E_VA, E_QB, E_KB, E_VB = 0, 512, 768, 1024, 1536, 2048
E_COLS = 2560


def _const_spec(shape):
    return pl.BlockSpec(shape, lambda *_: (0,) * len(shape), pipeline_mode=pl.Buffered(1))


def _params(*sem):
    return pltpu.CompilerParams(dimension_semantics=sem, vmem_limit_bytes=VMEM_LIMIT)


def _rmsnorm_rows(x, g):
    return x * lax.rsqrt(jnp.mean(x * x, axis=-1, keepdims=True) + RMS_EPS) * g


def _silu(g):
    return g * (1.0 / (1.0 + jnp.exp(-g)))


def _dot(a, b):
    return jnp.dot(a, b, preferred_element_type=F32)


def _dot_nt(a, b):
    return lax.dot_general(a, b, (((1,), (1,)), ((), ())), preferred_element_type=F32)


def _inproj_even_kernel(x_ref, g_ref, w_ref, gain_ref, qa_ref, ka_ref, va_ref, qb_ref, kb_ref, vb_ref):
    xb = _rmsnorm_rows(x_ref[...], g_ref[...]).astype(BF16)
    r = lax.broadcasted_iota(jnp.int32, (MXU_N, MXU_N), 0) // HEAD_DIM
    c = lax.broadcasted_iota(jnp.int32, (MXU_N, MXU_N), 1) // HEAD_DIM
    head_mean = jnp.where(r == c, 1.0 / HEAD_DIM, 0.0).astype(BF16)
    outs = ((qa_ref, E_QA, 512, True), (ka_ref, E_KA, 256, True), (va_ref, E_VA, 256, False),
            (qb_ref, E_QB, 512, True), (kb_ref, E_KB, 512, True), (vb_ref, E_VB, 512, False))
    chunks = [(ref, start + off, off, normed) for ref, start, width, normed in outs
              for off in range(0, width, MXU_N)]
    ys = [_dot(xb, w_ref[:, col:col + MXU_N]) for _, col, _, _ in chunks]
    ms = [_dot((y * y).astype(BF16), head_mean) if normed else None
          for y, (_, _, _, normed) in zip(ys, chunks)]
    for y, m, (ref, col, off, normed) in zip(ys, ms, chunks):
        if normed:
            y = y * lax.rsqrt(m + RMS_EPS) * gain_ref[:, col:col + MXU_N]
        ref[:, off:off + MXU_N] = y.astype(BF16)


def _inproj_even(x, g, w, gain, tm=512):
    s = x.shape[0]
    widths = (512, 256, 256, 512, 512, 512)
    return pl.pallas_call(
        _inproj_even_kernel,
        grid=(s // tm,),
        in_specs=[pl.BlockSpec((tm, D_MODEL), lambda i: (i, 0)),
                  _const_spec((1, D_MODEL)), _const_spec((D_MODEL, E_COLS)), _const_spec((1, E_COLS))],
        out_specs=[pl.BlockSpec((tm, wd), lambda i: (i, 0)) for wd in widths],
        out_shape=[jax.ShapeDtypeStruct((s, wd), BF16) for wd in widths],
        compiler_params=_params("parallel"),
        name="inproj_even",
    )(x, g, w, gain)


def _window_kernel(sink_ref, q_ref, kp_ref, kc_ref, kn_ref, vp_ref, vc_ref, vn_ref, o_ref, *, tq):
    i = pl.program_id(0)
    lane = lax.broadcasted_iota(jnp.int32, (1, LANES), 1)
    low_half = lane < HEAD_DIM
    qi = lax.broadcasted_iota(jnp.int32, (A_BLOCK, 3 * A_BLOCK), 0)
    kj = lax.broadcasted_iota(jnp.int32, (A_BLOCK, 3 * A_BLOCK), 1)
    absrel = jnp.abs(kj - A_BLOCK - qi)
    in_window = absrel <= A_WINDOW
    absrel_f = absrel.astype(F32)
    kcol = lax.broadcasted_iota(jnp.int32, (1, 3 * A_BLOCK), 1)
    edge_first = jnp.where((i == 0) & (kcol < A_BLOCK), NEG, 0.0)
    edge_last = jnp.where((i == pl.num_programs(0) - 1) & (kcol >= 2 * A_BLOCK), NEG, 0.0)
    nblk = tq // A_BLOCK
    penalty = []
    for h in range(A_Q_HEADS):
        slope = 2.0 ** (-(8.0 / A_Q_HEADS) * (h + 1.0))
        base = jnp.where(in_window, -slope * absrel_f, NEG)
        penalty.append({0: base + edge_first, nblk - 1: base + edge_last, None: base})
    units = [(j, b) for j in range(A_KV_HEADS) for b in range(nblk)]
    kcat, vcat = [], []
    for j in range(A_KV_HEADS):
        ks = slice(LANES * j, LANES * (j + 1))
        kcat.append(jnp.concatenate([kp_ref[tq - A_BLOCK:, ks], kc_ref[:, ks], kn_ref[:A_BLOCK, ks]], axis=0))
        vcat.append(jnp.concatenate([vp_ref[tq - A_BLOCK:, ks], vc_ref[:, ks], vn_ref[:A_BLOCK, ks]], axis=0))
    scores = []
    for j, b in units:
        rows = slice(A_BLOCK * b, A_BLOCK * (b + 1))
        lhs = []
        for e in range(A_GROUP):
            pair = q_ref[rows, LANES * (2 * j + e // 2):LANES * (2 * j + e // 2 + 1)]
            lhs.append(jnp.where(low_half if e % 2 == 0 else ~low_half, pair, jnp.zeros_like(pair)))
        scores.append(_dot_nt(jnp.concatenate(lhs, axis=0), kcat[j][A_BLOCK * b:A_BLOCK * (b + 3)]))
    probs, invs = [], []
    for (j, b), s_all in zip(units, scores):
        p_unit, inv_unit = [], []
        for e in range(A_GROUP):
            h = A_GROUP * j + e
            sk = sink_ref[h]
            pen = penalty[h]
            s = s_all[A_BLOCK * e:A_BLOCK * (e + 1)] + pen.get(b, pen[None])
            m = jnp.maximum(jnp.max(s, axis=-1, keepdims=True), sk)
            p = jnp.exp(s - m)
            inv_unit.append(1.0 / (jnp.sum(p, axis=-1, keepdims=True) + jnp.exp(sk - m)))
            p_unit.append(p.astype(BF16))
        probs.append(jnp.concatenate(p_unit, axis=0))
        invs.append(jnp.concatenate(inv_unit, axis=0))
    for (j, b), p_all, inv in zip(units, probs, invs):
        rows = slice(A_BLOCK * b, A_BLOCK * (b + 1))
        o = _dot(p_all, vcat[j][A_BLOCK * b:A_BLOCK * (b + 3)]) * inv
        for pr in range(A_GROUP // 2):
            lo = o[A_BLOCK * 2 * pr:A_BLOCK * (2 * pr + 1)]
            hi = o[A_BLOCK * (2 * pr + 1):A_BLOCK * (2 * pr + 2)]
            col = LANES * (2 * j + pr)
            o_ref[rows, col:col + LANES] = jnp.where(low_half, lo, hi).astype(BF16)


def _window_attn(sink, qa, ka2, va2, tq=512):
    s = qa.shape[0]
    nb = s // tq
    prev = lambda i: (jnp.maximum(i - 1, 0), 0)
    cur = lambda i: (i, 0)
    nxt = lambda i: (jnp.minimum(i + 1, nb - 1), 0)
    kv = lambda f: pl.BlockSpec((tq, 2 * LANES), f)
    return pl.pallas_call(
        functools.partial(_window_kernel, tq=tq),
        grid=(nb,),
        in_specs=[pl.BlockSpec(memory_space=pltpu.SMEM),
                  pl.BlockSpec((tq, 512), cur), kv(prev), kv(cur), kv(nxt), kv(prev), kv(cur), kv(nxt)],
        out_specs=pl.BlockSpec((tq, 512), cur),
        out_shape=jax.ShapeDtypeStruct((s, 512), BF16),
        compiler_params=_params("parallel"),
        name="window_gqa",
    )(sink, qa, ka2, ka2, ka2, va2, va2, va2)


NA_ROWS_PER_STEP = 8
NA_TOK = NA_ROWS_PER_STEP * GRID_W
NA_KEYS = NA_KH * GRID_W


def _na_kernel(q_ref, kp_ref, kc_ref, kn_ref, vp_ref, vc_ref, vn_ref, bias_ref, o_ref, kcat, vcat):
    i = pl.program_id(0)
    kcat[0:NA_TOK] = kp_ref[...]
    kcat[NA_TOK:2 * NA_TOK] = kc_ref[...]
    kcat[2 * NA_TOK:3 * NA_TOK] = kn_ref[...]
    vcat[0:NA_TOK] = vp_ref[...]
    vcat[NA_TOK:2 * NA_TOK] = vc_ref[...]
    vcat[2 * NA_TOK:3 * NA_TOK] = vn_ref[...]
    lane = lax.broadcasted_iota(jnp.int32, (1, LANES), 1)
    low_half = lane < HEAD_DIM

    def row_body(rr, carry):
        r = i * NA_ROWS_PER_STEP + rr
        rstart = jnp.clip(r - NA_KH // 2, 0, GRID_ROWS - NA_KH)
        start = pl.multiple_of((rstart - (i - 1) * NA_ROWS_PER_STEP) * GRID_W, GRID_W)
        ri0 = rstart - r + NA_KH - 1
        qrows = pl.ds(pl.multiple_of(rr * GRID_W, GRID_W), GRID_W)
        npair = B_HEADS // 2
        scores = []
        for p in range(npair):
            cols = slice(LANES * p, LANES * (p + 1))
            qp = q_ref[qrows, cols]
            lhs = jnp.concatenate([jnp.where(low_half, qp, jnp.zeros_like(qp)),
                                   jnp.where(low_half, jnp.zeros_like(qp), qp)], axis=0)
            scores.append(_dot_nt(lhs, kcat[pl.ds(start, NA_KEYS), cols]))
        probs, invs = [], []
        for p in range(npair):
            bias = jnp.concatenate(
                [jnp.concatenate([bias_ref[2 * p + e, ri0 + 2 * m] for m in range(NA_KH // 2)], axis=1)
                 for e in range(2)], axis=0)
            s = scores[p] + bias
            pexp = jnp.exp(s - jnp.max(s, axis=-1, keepdims=True))
            invs.append(1.0 / jnp.sum(pexp, axis=-1, keepdims=True))
            probs.append(pexp.astype(BF16))
        for p in range(npair):
            cols = slice(LANES * p, LANES * (p + 1))
            o = _dot(probs[p], vcat[pl.ds(start, NA_KEYS), cols]) * invs[p]
            o_ref[qrows, cols] = jnp.where(low_half, o[:GRID_W], o[GRID_W:]).astype(BF16)
        return carry

    lax.fori_loop(0, NA_ROWS_PER_STEP, row_body, 0, unroll=True)


def _na_bias_table(rpb):
    qc = np.arange(GRID_W)[:, None]
    kc = np.arange(GRID_W)[None, :]
    cstart = np.clip(qc - NA_KW // 2, 0, GRID_W - NA_KW)
    col_valid = (kc >= cstart) & (kc < cstart + NA_KW)
    col_idx = np.clip(kc - qc + NA_KW - 1, 0, 2 * NA_KW - 2)
    onehot = (col_idx.reshape(1, -1) == np.arange(2 * NA_KW - 1)[:, None]).astype(np.float32)
    n_ri = 2 * NA_KH - 1
    t = jnp.dot(rpb.astype(F32).reshape(B_HEADS * n_ri, 2 * NA_KW - 1), jnp.asarray(onehot),
                precision=lax.Precision.HIGHEST).reshape(B_HEADS, n_ri, GRID_W, GRID_W)
    t = jnp.where(col_valid[None, None], t, NEG)
    return jnp.concatenate([t[:, :-1], t[:, 1:]], axis=-1)


def _na_attn(qb, kb, vb, bias):
    s = qb.shape[0]
    nb = s // NA_TOK
    prev = lambda i: (jnp.maximum(i - 1, 0), 0)
    cur = lambda i: (i, 0)
    nxt = lambda i: (jnp.minimum(i + 1, nb - 1), 0)
    blk = lambda f: pl.BlockSpec((NA_TOK, 512), f)
    return pl.pallas_call(
        _na_kernel,
        grid=(nb,),
        in_specs=[blk(cur), blk(prev), blk(cur), blk(nxt), blk(prev), blk(cur), blk(nxt),
                  _const_spec(bias.shape)],
        out_specs=blk(cur),
        out_shape=jax.ShapeDtypeStruct((s, 512), BF16),
        scratch_shapes=[pltpu.VMEM((3 * NA_TOK, 512), BF16), pltpu.VMEM((3 * NA_TOK, 512), BF16)],
        compiler_params=_params("parallel"),
        name="neighbourhood_attn",
    )(qb, kb, kb, kb, vb, vb, vb, bias)


FF_CHUNK = 1024


def _outproj_mlp_kernel(*refs, n_mix, gated):
    x_ref = refs[0]
    mix_refs = refs[1:1 + n_mix]
    rest = refs[1 + n_mix:]
    if gated:
        (y_ref,), gate_ref, gn_ref = mix_refs, rest[0], rest[1]
        wout_ref, g_ref, w1_ref, w2_ref, o_ref = rest[2:]
        proj = None
        for h in range(RET_HEADS):
            cols = slice(h * RET_DV, (h + 1) * RET_DV)
            y = y_ref[:, cols].astype(F32)
            d = y - jnp.mean(y, axis=-1, keepdims=True)
            var = jnp.mean(d * d, axis=-1, keepdims=True)
            yn = d * lax.rsqrt(var + GN_EPS) * gn_ref[:, cols]
            mix_h = (yn * gate_ref[:, cols].astype(F32)).astype(BF16)
            part = _dot(mix_h, wout_ref[cols, :])
            proj = part if proj is None else proj + part
    else:
        wout_ref, g_ref, w1_ref, w2_ref, o_ref = rest
        proj = _dot(jnp.concatenate([m_ref[...] for m_ref in mix_refs], axis=1), wout_ref[...])
    x1 = x_ref[...] + proj
    hb = _rmsnorm_rows(x1, g_ref[...]).astype(BF16)
    acc = x1
    for c in range(0, D_FF, FF_CHUNK):
        a = jnp.maximum(_dot(hb, w1_ref[:, c:c + FF_CHUNK]), 0.0)
        acc = acc + _dot((a * a).astype(BF16), w2_ref[c:c + FF_CHUNK, :])
    o_ref[...] = acc


def _outproj_mlp(x, mixes, gate_gn, wout, g, w1, w2, tm=512, name="outproj_mlp"):
    s = x.shape[0]
    row = lambda i: (i, 0)
    gated = gate_gn is not None
    acts = tuple(mixes) + ((gate_gn[0],) if gated else ())
    consts = ((gate_gn[1],) if gated else ()) + (wout, g, w1, w2)
    in_specs = [pl.BlockSpec((tm, D_MODEL), row)]
    in_specs += [pl.BlockSpec((tm, m.shape[1]), row) for m in acts]
    in_specs += [_const_spec(c.shape) for c in consts]
    return pl.pallas_call(
        functools.partial(_outproj_mlp_kernel, n_mix=len(mixes), gated=gated),
        grid=(s // tm,),
        in_specs=in_specs,
        out_specs=pl.BlockSpec((tm, D_MODEL), row),
        out_shape=jax.ShapeDtypeStruct((s, D_MODEL), F32),
        compiler_params=_params("parallel"),
        name=name,
    )(x, *acts, *consts)


def _inproj_odd_kernel(x_ref, g_ref, wq_ref, wkt_ref, wv_ref, wg_ref, q_ref, kt_ref, v_ref, gate_ref):
    xb = _rmsnorm_rows(x_ref[...], g_ref[...]).astype(BF16)
    q_ref[...] = _dot(xb, wq_ref[...]).astype(BF16)
    kt_ref[...] = (_dot_nt(wkt_ref[...], xb) * RET_DK ** -0.5).astype(BF16)
    for off in range(0, ODD_MIX, PROJ_CHUNK):
        v_ref[:, off:off + PROJ_CHUNK] = _dot(xb, wv_ref[:, off:off + PROJ_CHUNK]).astype(BF16)
    for off in range(0, ODD_MIX, PROJ_CHUNK):
        gate_ref[:, off:off + PROJ_CHUNK] = _silu(_dot(xb, wg_ref[:, off:off + PROJ_CHUNK])).astype(BF16)


def _inproj_odd(x, g, wq, wkt, wv, wg, tm=512):
    s = x.shape[0]
    row = lambda i: (i, 0)
    return pl.pallas_call(
        _inproj_odd_kernel,
        grid=(s // tm,),
        in_specs=[pl.BlockSpec((tm, D_MODEL), row), _const_spec((1, D_MODEL)),
                  _const_spec(wq.shape), _const_spec(wkt.shape), _const_spec(wv.shape), _const_spec(wg.shape)],
        out_specs=[pl.BlockSpec((tm, D_MODEL), row), pl.BlockSpec((D_MODEL, tm), lambda i: (0, i)),
                   pl.BlockSpec((tm, ODD_MIX), row), pl.BlockSpec((tm, ODD_MIX), row)],
        out_shape=[jax.ShapeDtypeStruct((s, D_MODEL), BF16), jax.ShapeDtypeStruct((D_MODEL, s), BF16),
                   jax.ShapeDtypeStruct((s, ODD_MIX), BF16), jax.ShapeDtypeStruct((s, ODD_MIX), BF16)],
        compiler_params=_params("parallel"),
        name="inproj_odd",
    )(x, g, wq, wkt, wv, wg)


RET_CHUNKS_PER_STEP = 8
RET_TOK = RET_CHUNKS_PER_STEP * RET_CHUNK
RET_STEPS = SEQ // RET_TOK
RET_NCHUNK = SEQ // RET_CHUNK


def _retention_kernel(lg_ref, q_ref, kt_ref, v_ref, o_ref, rf_all, state):
    h = pl.program_id(0)
    t = pl.program_id(1)
    second = t >= RET_STEPS
    blk = jnp.where(second, 2 * RET_STEPS - 1 - t, t)
    lgf = lg_ref[0, h]
    lgb = lg_ref[1, h]
    tok_lane = lax.broadcasted_iota(jnp.int32, (1, RET_CHUNK), 1).astype(F32)
    tok_row = lax.broadcasted_iota(jnp.int32, (RET_CHUNK, 1), 0).astype(F32)
    chunk_len = jnp.full((1, 1), RET_CHUNK, F32)

    @pl.when((t == 0) | (t == RET_STEPS))
    def _():
        state[...] = jnp.zeros_like(state)

    @pl.when(jnp.logical_not(second))
    def _():
        zeta = jnp.exp(lgf * (RET_CHUNK - 1.0 - tok_lane))
        decay = jnp.exp(lgf * chunk_len)
        r = state[...]
        for c in range(RET_CHUNKS_PER_STEP):
            tok = slice(c * RET_CHUNK, (c + 1) * RET_CHUNK)
            kz = (kt_ref[:, tok].astype(F32) * zeta).astype(BF16)
            rf_all[blk * RET_CHUNKS_PER_STEP + c] = r.astype(BF16)
            r = r * decay + _dot(kz, v_ref[tok, :])
        state[...] = r

    @pl.when(second)
    def _():
        diff = (lax.broadcasted_iota(jnp.int32, (RET_CHUNK, RET_CHUNK), 0)
                - lax.broadcasted_iota(jnp.int32, (RET_CHUNK, RET_CHUNK), 1)).astype(F32)
        dsum = jnp.where(diff >= 0, jnp.exp(lgf * jnp.maximum(diff, 0.0)), jnp.exp(lgb * jnp.maximum(-diff, 0.0)))
        cross_f = jnp.exp(lgf * (tok_row + 1.0))
        cross_b = jnp.exp(lgb * (RET_CHUNK - tok_row))
        zeta = jnp.exp(lgb * tok_lane)
        decay = jnp.exp(lgb * chunk_len)
        r = state[...]
        for c in reversed(range(RET_CHUNKS_PER_STEP)):
            tok = slice(c * RET_CHUNK, (c + 1) * RET_CHUNK)
            q = q_ref[tok, :]
            kt = kt_ref[:, tok]
            v = v_ref[tok, :]
            a = (_dot(q, kt) * dsum).astype(BF16)
            qf = q.astype(F32)
            lhs = jnp.concatenate([a, (qf * cross_f).astype(BF16), (qf * cross_b).astype(BF16)], axis=1)
            rhs = jnp.concatenate([v, rf_all[blk * RET_CHUNKS_PER_STEP + c], r.astype(BF16)], axis=0)
            o_ref[tok, :] = _dot(lhs, rhs).astype(BF16)
            r = r * decay + _dot((kt.astype(F32) * zeta).astype(BF16), v)
        state[...] = r


def _retention(lg, q, kt, v):
    s = q.shape[0]

    def blk_of(t):
        return jnp.where(t >= RET_STEPS, 2 * RET_STEPS - 1 - t, t)

    def second_blk(t):
        return jnp.where(t >= RET_STEPS, 2 * RET_STEPS - 1 - t, RET_STEPS - 1)

    return pl.pallas_call(
        _retention_kernel,
        grid=(RET_HEADS, 2 * RET_STEPS),
        in_specs=[pl.BlockSpec(memory_space=pltpu.SMEM),
                  pl.BlockSpec((RET_TOK, RET_DK), lambda h, t: (second_blk(t), h)),
                  pl.BlockSpec((RET_DK, RET_TOK), lambda h, t: (h, blk_of(t))),
                  pl.BlockSpec((RET_TOK, RET_DV), lambda h, t: (blk_of(t), h))],
        out_specs=pl.BlockSpec((RET_TOK, RET_DV), lambda h, t: (second_blk(t), h)),
        out_shape=jax.ShapeDtypeStruct((s, ODD_MIX), BF16),
        scratch_shapes=[pltpu.VMEM((RET_NCHUNK, RET_DK, RET_DV), BF16), pltpu.VMEM((RET_DK, RET_DV), F32)],
        compiler_params=_params("arbitrary", "arbitrary"),
        name="retention",
    )(lg, q, kt, v)


def _even_proj_weights(w_in, qn_a, kn_a, qn_b, kn_b):
    qa, ka, va, qb, kb, vb = jnp.split(w_in, np.cumsum([512, 128, 128, 512, 512]).tolist(), axis=-1)
    dup = lambda w: jnp.concatenate([w[:, :HEAD_DIM]] * 2 + [w[:, HEAD_DIM:]] * 2, axis=-1)
    w = jnp.concatenate([qa, dup(ka), dup(va), qb, kb, vb], axis=-1).astype(BF16)
    scale = HEAD_DIM ** -0.5
    gain = jnp.concatenate([jnp.tile(qn_a, A_Q_HEADS) * scale, jnp.tile(kn_a, 4), jnp.ones((256,), F32),
                            jnp.tile(qn_b, B_HEADS) * scale, jnp.tile(kn_b, B_HEADS),
                            jnp.ones((512,), F32)]).astype(F32)
    return w, gain[None, :]


def kernel(x, attn_norm_e, w_in_e, q_norm_a, k_norm_a, sink_a, q_norm_b, k_norm_b, rpb_b, w_out_e,
           ret_norm_o, w_in_o, decay_fwd_o, decay_bwd_o, ret_gn_o, w_out_o,
           mlp_norm, w_mlp_in, w_mlp_out):
    b, s, d = x.shape
    xs = x.reshape(b * s, d)
    depth = mlp_norm.shape[0]
    for layer in range(depth):
        i = layer // 2
        w1 = w_mlp_in[layer].astype(BF16)
        w2 = w_mlp_out[layer].astype(BF16)
        g_mlp = mlp_norm[layer][None, :]
        if layer % 2 == 0:
            w, gain = _even_proj_weights(w_in_e[i], q_norm_a[i], k_norm_a[i], q_norm_b[i], k_norm_b[i])
            qa, ka2, va2, qb, kb, vb = _inproj_even(xs, attn_norm_e[i][None, :], w, gain)
            oa = _window_attn(sink_a[i].astype(F32), qa, ka2, va2)
            ob = _na_attn(qb, kb, vb, _na_bias_table(rpb_b[i]))
            xs = _outproj_mlp(xs, (oa, ob), None, w_out_e[i].astype(BF16), g_mlp, w1, w2, name="outproj_mlp_even")
        else:
            wq, wk, wv, wg = jnp.split(w_in_o[i].astype(BF16), [1024, 2048, 4096], axis=-1)
            q, kt, v, gate = _inproj_odd(xs, ret_norm_o[i][None, :], wq, wk.T, wv, wg)
            lg = jnp.stack([jax.nn.log_sigmoid(decay_fwd_o[i].astype(F32)),
                            jax.nn.log_sigmoid(decay_bwd_o[i].astype(F32))])
            y = _retention(lg, q, kt, v)
            xs = _outproj_mlp(xs, (y,), (gate, ret_gn_o[i].astype(F32)[None, :]), w_out_o[i].astype(BF16),
                              g_mlp, w1, w2, name="outproj_mlp_odd")
    return xs.reshape(b, s, d)
```

```python
import functools

import numpy as np
import jax
import jax.numpy as jnp
from jax import lax
from jax.experimental import pallas as pl
from jax.experimental.pallas import tpu as pltpu

D_MODEL = 1024
SEQ = 16384
HEAD_DIM = 64
A_Q_HEADS = 8
A_KV_HEADS = 2
A_GROUP = A_Q_HEADS // A_KV_HEADS
A_WINDOW = 128
A_BLOCK = 128
B_HEADS = 8
GRID_W = 64
GRID_ROWS = SEQ // GRID_W
NA_KH = 8
NA_KW = 16
RET_HEADS = 8
RET_DK = D_MODEL // RET_HEADS
RET_DV = 2 * D_MODEL // RET_HEADS
RET_CHUNK = 256
D_FF = 4 * D_MODEL
ODD_MIX = RET_HEADS * RET_DV
RMS_EPS = 1e-6
GN_EPS = 1e-5

LANES = 128
MXU_N = 256
PROJ_CHUNK = 1024
NEG = -1e30
LOG2E = 1.4426950408889634
VMEM_LIMIT = 60 * 1024 * 1024

F32 = jnp.float32
BF16 = jnp.bfloat16

E_QA, E_KA, E_VA, E_QB, E_KB, E_VB = 0, 512, 768, 1024, 1536, 2048
E_COLS = 2560


def _const_spec(shape):
    return pl.BlockSpec(shape, lambda *_: (0,) * len(shape), pipeline_mode=pl.Buffered(1))


def _params(*sem):
    return pltpu.CompilerParams(dimension_semantics=sem, vmem_limit_bytes=VMEM_LIMIT)


def _rmsnorm_rows(x, g):
    return x * lax.rsqrt(jnp.mean(x * x, axis=-1, keepdims=True) + RMS_EPS) * g


def _silu(g):
    return g * (1.0 / (1.0 + jnp.exp(-g)))


def _dot(a, b):
    return jnp.dot(a, b, preferred_element_type=F32)


def _dot_nt(a, b):
    return lax.dot_general(a, b, (((1,), (1,)), ((), ())), preferred_element_type=F32)


def _inproj_even_kernel(x_ref, g_ref, w_ref, gain_ref, qa_ref, ka_ref, va_ref, qb_ref, kb_ref, vb_ref):
    xb = _rmsnorm_rows(x_ref[...], g_ref[...]).astype(BF16)
    r = lax.broadcasted_iota(jnp.int32, (MXU_N, MXU_N), 0) // HEAD_DIM
    c = lax.broadcasted_iota(jnp.int32, (MXU_N, MXU_N), 1) // HEAD_DIM
    head_mean = jnp.where(r == c, 1.0 / HEAD_DIM, 0.0).astype(BF16)
    outs = ((qa_ref, E_QA, 512, True), (ka_ref, E_KA, 256, True), (va_ref, E_VA, 256, False),
            (qb_ref, E_QB, 512, True), (kb_ref, E_KB, 512, True), (vb_ref, E_VB, 512, False))
    chunks = [(ref, start + off, off, normed) for ref, start, width, normed in outs
              for off in range(0, width, MXU_N)]
    ys = [_dot(xb, w_ref[:, col:col + MXU_N]) for _, col, _, _ in chunks]
    ms = [_dot((y * y).astype(BF16), head_mean) if normed else None
          for y, (_, _, _, normed) in zip(ys, chunks)]
    for y, m, (ref, col, off, normed) in zip(ys, ms, chunks):
        if normed:
            y = y * lax.rsqrt(m + RMS_EPS) * gain_ref[:, col:col + MXU_N]
        ref[:, off:off + MXU_N] = y.astype(BF16)


def _inproj_even(x, g, w, gain, tm=512):
    s = x.shape[0]
    widths = (512, 256, 256, 512, 512, 512)
    return pl.pallas_call(
        _inproj_even_kernel,
        grid=(s // tm,),
        in_specs=[pl.BlockSpec((tm, D_MODEL), lambda i: (i, 0)),
                  _const_spec((1, D_MODEL)), _const_spec((D_MODEL, E_COLS)), _const_spec((1, E_COLS))],
        out_specs=[pl.BlockSpec((tm, wd), lambda i: (i, 0)) for wd in widths],
        out_shape=[jax.ShapeDtypeStruct((s, wd), BF16) for wd in widths],
        compiler_params=_params("parallel"),
        name="inproj_even",
    )(x, g, w, gain)


def _window_kernel(sink_ref, q_ref, kp_ref, kc_ref, kn_ref, vp_ref, vc_ref, vn_ref, o_ref, *, tq):
    i = pl.program_id(0)
    lane = lax.broadcasted_iota(jnp.int32, (1, LANES), 1)
    low_half = lane < HEAD_DIM
    qi = lax.broadcasted_iota(jnp.int32, (A_BLOCK, 3 * A_BLOCK), 0)
    kj = lax.broadcasted_iota(jnp.int32, (A_BLOCK, 3 * A_BLOCK), 1)
    absrel = jnp.abs(kj - A_BLOCK - qi)
    in_window = absrel <= A_WINDOW
    absrel_f = absrel.astype(F32)
    kcol = lax.broadcasted_iota(jnp.int32, (1, 3 * A_BLOCK), 1)
    edge_first = jnp.where((i == 0) & (kcol < A_BLOCK), NEG, 0.0)
    edge_last = jnp.where((i == pl.num_programs(0) - 1) & (kcol >= 2 * A_BLOCK), NEG, 0.0)
    nblk = tq // A_BLOCK
    penalty = []
    for h in range(A_Q_HEADS):
        slope = LOG2E * 2.0 ** (-(8.0 / A_Q_HEADS) * (h + 1.0))
        base = jnp.where(in_window, -slope * absrel_f, NEG)
        penalty.append({0: base + edge_first, nblk - 1: base + edge_last, None: base})
    units = [(j, b) for j in range(A_KV_HEADS) for b in range(nblk)]
    kcat, vcat = [], []
    for j in range(A_KV_HEADS):
        ks = slice(LANES * j, LANES * (j + 1))
        kcat.append(jnp.concatenate([kp_ref[tq - A_BLOCK:, ks], kc_ref[:, ks], kn_ref[:A_BLOCK, ks]], axis=0))
        vcat.append(jnp.concatenate([vp_ref[tq - A_BLOCK:, ks], vc_ref[:, ks], vn_ref[:A_BLOCK, ks]], axis=0))
    scores = []
    for j, b in units:
        rows = slice(A_BLOCK * b, A_BLOCK * (b + 1))
        lhs = []
        for e in range(A_GROUP):
            pair = q_ref[rows, LANES * (2 * j + e // 2):LANES * (2 * j + e // 2 + 1)]
            lhs.append(jnp.where(low_half if e % 2 == 0 else ~low_half, pair, jnp.zeros_like(pair)))
        scores.append(_dot_nt(jnp.concatenate(lhs, axis=0), kcat[j][A_BLOCK * b:A_BLOCK * (b + 3)]))
    probs, invs = [], []
    for (j, b), s_all in zip(units, scores):
        p_unit, inv_unit = [], []
        for e in range(A_GROUP):
            h = A_GROUP * j + e
            sk = sink_ref[h]
            pen = penalty[h]
            s = s_all[A_BLOCK * e:A_BLOCK * (e + 1)] + pen.get(b, pen[None])
            m = jnp.maximum(jnp.max(s, axis=-1, keepdims=True), sk)
            p = jnp.exp2(s - m)
            inv_unit.append(1.0 / (jnp.sum(p, axis=-1, keepdims=True) + jnp.exp2(sk - m)))
            p_unit.append(p.astype(BF16))
        probs.append(jnp.concatenate(p_unit, axis=0))
        invs.append(jnp.concatenate(inv_unit, axis=0))
    for (j, b), p_all, inv in zip(units, probs, invs):
        rows = slice(A_BLOCK * b, A_BLOCK * (b + 1))
        o = _dot(p_all, vcat[j][A_BLOCK * b:A_BLOCK * (b + 3)]) * inv
        for pr in range(A_GROUP // 2):
            lo = o[A_BLOCK * 2 * pr:A_BLOCK * (2 * pr + 1)]
            hi = o[A_BLOCK * (2 * pr + 1):A_BLOCK * (2 * pr + 2)]
            col = LANES * (2 * j + pr)
            o_ref[rows, col:col + LANES] = jnp.where(low_half, lo, hi).astype(BF16)


def _window_attn(sink, qa, ka2, va2, tq=512):
    s = qa.shape[0]
    nb = s // tq
    prev = lambda i: (jnp.maximum(i - 1, 0), 0)
    cur = lambda i: (i, 0)
    nxt = lambda i: (jnp.minimum(i + 1, nb - 1), 0)
    kv = lambda f: pl.BlockSpec((tq, 2 * LANES), f)
    return pl.pallas_call(
        functools.partial(_window_kernel, tq=tq),
        grid=(nb,),
        in_specs=[pl.BlockSpec(memory_space=pltpu.SMEM),
                  pl.BlockSpec((tq, 512), cur), kv(prev), kv(cur), kv(nxt), kv(prev), kv(cur), kv(nxt)],
        out_specs=pl.BlockSpec((tq, 512), cur),
        out_shape=jax.ShapeDtypeStruct((s, 512), BF16),
        compiler_params=_params("parallel"),
        name="window_gqa",
    )(sink, qa, ka2, ka2, ka2, va2, va2, va2)


NA_ROWS_PER_STEP = 8
NA_TOK = NA_ROWS_PER_STEP * GRID_W
NA_KEYS = NA_KH * GRID_W


def _na_kernel(q_ref, kp_ref, kc_ref, kn_ref, vp_ref, vc_ref, vn_ref, bias_ref, o_ref, kcat, vcat):
    i = pl.program_id(0)
    kcat[0:NA_TOK] = kp_ref[...]
    kcat[NA_TOK:2 * NA_TOK] = kc_ref[...]
    kcat[2 * NA_TOK:3 * NA_TOK] = kn_ref[...]
    vcat[0:NA_TOK] = vp_ref[...]
    vcat[NA_TOK:2 * NA_TOK] = vc_ref[...]
    vcat[2 * NA_TOK:3 * NA_TOK] = vn_ref[...]
    lane = lax.broadcasted_iota(jnp.int32, (1, LANES), 1)
    low_half = lane < HEAD_DIM

    def row_body(rr, carry):
        r = i * NA_ROWS_PER_STEP + rr
        rstart = jnp.clip(r - NA_KH // 2, 0, GRID_ROWS - NA_KH)
        start = pl.multiple_of((rstart - (i - 1) * NA_ROWS_PER_STEP) * GRID_W, GRID_W)
        ri0 = rstart - r + NA_KH - 1
        qrows = pl.ds(pl.multiple_of(rr * GRID_W, GRID_W), GRID_W)
        npair = B_HEADS // 2
        scores = []
        for p in range(npair):
            cols = slice(LANES * p, LANES * (p + 1))
            qp = q_ref[qrows, cols]
            lhs = jnp.concatenate([jnp.where(low_half, qp, jnp.zeros_like(qp)),
                                   jnp.where(low_half, jnp.zeros_like(qp), qp)], axis=0)
            scores.append(_dot_nt(lhs, kcat[pl.ds(start, NA_KEYS), cols]))
        probs, invs = [], []
        for p in range(npair):
            bias = jnp.concatenate(
                [jnp.concatenate([bias_ref[2 * p + e, ri0 + 2 * m] for m in range(NA_KH // 2)], axis=1)
                 for e in range(2)], axis=0)
            s = scores[p] + bias
            pexp = jnp.exp2(s - jnp.max(s, axis=-1, keepdims=True))
            invs.append(1.0 / jnp.sum(pexp, axis=-1, keepdims=True))
            probs.append(pexp.astype(BF16))
        for p in range(npair):
            cols = slice(LANES * p, LANES * (p + 1))
            o = _dot(probs[p], vcat[pl.ds(start, NA_KEYS), cols]) * invs[p]
            o_ref[qrows, cols] = jnp.where(low_half, o[:GRID_W], o[GRID_W:]).astype(BF16)
        return carry

    lax.fori_loop(0, NA_ROWS_PER_STEP, row_body, 0, unroll=True)


def _na_bias_table(rpb):
    qc = np.arange(GRID_W)[:, None]
    kc = np.arange(GRID_W)[None, :]
    cstart = np.clip(qc - NA_KW // 2, 0, GRID_W - NA_KW)
    col_valid = (kc >= cstart) & (kc < cstart + NA_KW)
    col_idx = np.clip(kc - qc + NA_KW - 1, 0, 2 * NA_KW - 2)
    onehot = (col_idx.reshape(1, -1) == np.arange(2 * NA_KW - 1)[:, None]).astype(np.float32)
    n_ri = 2 * NA_KH - 1
    t = jnp.dot(rpb.astype(F32).reshape(B_HEADS * n_ri, 2 * NA_KW - 1), jnp.asarray(onehot),
                precision=lax.Precision.HIGHEST).reshape(B_HEADS, n_ri, GRID_W, GRID_W)
    t = jnp.where(col_valid[None, None], t, NEG)
    return jnp.concatenate([t[:, :-1], t[:, 1:]], axis=-1)


def _na_attn(qb, kb, vb, bias):
    s = qb.shape[0]
    nb = s // NA_TOK
    prev = lambda i: (jnp.maximum(i - 1, 0), 0)
    cur = lambda i: (i, 0)
    nxt = lambda i: (jnp.minimum(i + 1, nb - 1), 0)
    blk = lambda f: pl.BlockSpec((NA_TOK, 512), f)
    return pl.pallas_call(
        _na_kernel,
        grid=(nb,),
        in_specs=[blk(cur), blk(prev), blk(cur), blk(nxt), blk(prev), blk(cur), blk(nxt),
                  _const_spec(bias.shape)],
        out_specs=blk(cur),
        out_shape=jax.ShapeDtypeStruct((s, 512), BF16),
        scratch_shapes=[pltpu.VMEM((3 * NA_TOK, 512), BF16), pltpu.VMEM((3 * NA_TOK, 512), BF16)],
        compiler_params=_params("parallel"),
        name="neighbourhood_attn",
    )(qb, kb, kb, kb, vb, vb, vb, bias)


FF_CHUNK = 1024


def _outproj_mlp_kernel(*refs, n_mix, gated):
    x_ref = refs[0]
    mix_refs = refs[1:1 + n_mix]
    rest = refs[1 + n_mix:]
    if gated:
        (y_ref,), gate_ref, gn_ref = mix_refs, rest[0], rest[1]
        wout_ref, g_ref, w1_ref, w2_ref, o_ref = rest[2:]
        proj = None
        for h in range(RET_HEADS):
            cols = slice(h * RET_DV, (h + 1) * RET_DV)
            y = y_ref[:, cols].astype(F32)
            d = y - jnp.mean(y, axis=-1, keepdims=True)
            var = jnp.mean(d * d, axis=-1, keepdims=True)
            yn = d * lax.rsqrt(var + GN_EPS) * gn_ref[:, cols]
            mix_h = (yn * gate_ref[:, cols].astype(F32)).astype(BF16)
            part = _dot(mix_h, wout_ref[cols, :])
            proj = part if proj is None else proj + part
    else:
        wout_ref, g_ref, w1_ref, w2_ref, o_ref = rest
        proj = _dot(jnp.concatenate([m_ref[...] for m_ref in mix_refs], axis=1), wout_ref[...])
    x1 = x_ref[...] + proj
    hb = _rmsnorm_rows(x1, g_ref[...]).astype(BF16)
    acc = x1
    for c in range(0, D_FF, FF_CHUNK):
        a = jnp.maximum(_dot(hb, w1_ref[:, c:c + FF_CHUNK]), 0.0)
        acc = acc + _dot((a * a).astype(BF16), w2_ref[c:c + FF_CHUNK, :])
    o_ref[...] = acc


def _outproj_mlp(x, mixes, gate_gn, wout, g, w1, w2, tm=512, name="outproj_mlp"):
    s = x.shape[0]
    row = lambda i: (i, 0)
    gated = gate_gn is not None
    acts = tuple(mixes) + ((gate_gn[0],) if gated else ())
    consts = ((gate_gn[1],) if gated else ()) + (wout, g, w1, w2)
    in_specs = [pl.BlockSpec((tm, D_MODEL), row)]
    in_specs += [pl.BlockSpec((tm, m.shape[1]), row) for m in acts]
    in_specs += [_const_spec(c.shape) for c in consts]
    return pl.pallas_call(
        functools.partial(_outproj_mlp_kernel, n_mix=len(mixes), gated=gated),
        grid=(s // tm,),
        in_specs=in_specs,
        out_specs=pl.BlockSpec((tm, D_MODEL), row),
        out_shape=jax.ShapeDtypeStruct((s, D_MODEL), F32),
        compiler_params=_params("parallel"),
        name=name,
    )(x, *acts, *consts)


def _inproj_odd_kernel(x_ref, g_ref, wq_ref, wkt_ref, wv_ref, wg_ref, q_ref, kt_ref, v_ref, gate_ref):
    xb = _rmsnorm_rows(x_ref[...], g_ref[...]).astype(BF16)
    q_ref[...] = _dot(xb, wq_ref[...]).astype(BF16)
    kt_ref[...] = (_dot_nt(wkt_ref[...], xb) * RET_DK ** -0.5).astype(BF16)
    for off in range(0, ODD_MIX, PROJ_CHUNK):
        v_ref[:, off:off + PROJ_CHUNK] = _dot(xb, wv_ref[:, off:off + PROJ_CHUNK]).astype(BF16)
    for off in range(0, ODD_MIX, PROJ_CHUNK):
        gate_ref[:, off:off + PROJ_CHUNK] = _silu(_dot(xb, wg_ref[:, off:off + PROJ_CHUNK])).astype(BF16)


def _inproj_odd(x, g, wq, wkt, wv, wg, tm=512):
    s = x.shape[0]
    row = lambda i: (i, 0)
    return pl.pallas_call(
        _inproj_odd_kernel,
        grid=(s // tm,),
        in_specs=[pl.BlockSpec((tm, D_MODEL), row), _const_spec((1, D_MODEL)),
                  _const_spec(wq.shape), _const_spec(wkt.shape), _const_spec(wv.shape), _const_spec(wg.shape)],
        out_specs=[pl.BlockSpec((tm, D_MODEL), row), pl.BlockSpec((D_MODEL, tm), lambda i: (0, i)),
                   pl.BlockSpec((tm, ODD_MIX), row), pl.BlockSpec((tm, ODD_MIX), row)],
        out_shape=[jax.ShapeDtypeStruct((s, D_MODEL), BF16), jax.ShapeDtypeStruct((D_MODEL, s), BF16),
                   jax.ShapeDtypeStruct((s, ODD_MIX), BF16), jax.ShapeDtypeStruct((s, ODD_MIX), BF16)],
        compiler_params=_params("parallel"),
        name="inproj_odd",
    )(x, g, wq, wkt, wv, wg)


RET_CHUNKS_PER_STEP = 16
RET_TOK = RET_CHUNKS_PER_STEP * RET_CHUNK
RET_STEPS = SEQ // RET_TOK
RET_NCHUNK = SEQ // RET_CHUNK


def _retention_kernel(lg_ref, q_ref, kt_ref, v_ref, o_ref, rf_all, state):
    h = pl.program_id(0)
    t = pl.program_id(1)
    second = t >= RET_STEPS
    blk = jnp.where(second, 2 * RET_STEPS - 1 - t, t)
    lgf = lg_ref[0, h]
    lgb = lg_ref[1, h]
    tok_lane = lax.broadcasted_iota(jnp.int32, (1, RET_CHUNK), 1).astype(F32)
    tok_row = lax.broadcasted_iota(jnp.int32, (RET_CHUNK, 1), 0).astype(F32)
    chunk_len = jnp.full((1, 1), RET_CHUNK, F32)

    @pl.when((t == 0) | (t == RET_STEPS))
    def _():
        state[...] = jnp.zeros_like(state)

    @pl.when(jnp.logical_not(second))
    def _():
        zeta = jnp.exp(lgf * (RET_CHUNK - 1.0 - tok_lane))
        decay = jnp.exp(lgf * chunk_len)
        r = state[...]
        for c in range(RET_CHUNKS_PER_STEP):
            tok = slice(c * RET_CHUNK, (c + 1) * RET_CHUNK)
            kz = (kt_ref[:, tok].astype(F32) * zeta).astype(BF16)
            rf_all[blk * RET_CHUNKS_PER_STEP + c] = r.astype(BF16)
            r = r * decay + _dot(kz, v_ref[tok, :])
        state[...] = r

    @pl.when(second)
    def _():
        diff = (lax.broadcasted_iota(jnp.int32, (RET_CHUNK, RET_CHUNK), 0)
                - lax.broadcasted_iota(jnp.int32, (RET_CHUNK, RET_CHUNK), 1)).astype(F32)
        dsum = jnp.where(diff >= 0, jnp.exp(lgf * jnp.maximum(diff, 0.0)), jnp.exp(lgb * jnp.maximum(-diff, 0.0)))
        cross_f = jnp.exp(lgf * (tok_row + 1.0))
        cross_b = jnp.exp(lgb * (RET_CHUNK - tok_row))
        zeta = jnp.exp(lgb * tok_lane)
        decay = jnp.exp(lgb * chunk_len)
        r = state[...]
        for c in reversed(range(RET_CHUNKS_PER_STEP)):
            tok = slice(c * RET_CHUNK, (c + 1) * RET_CHUNK)
            q = q_ref[tok, :]
            kt = kt_ref[:, tok]
            v = v_ref[tok, :]
            a = (_dot(q, kt) * dsum).astype(BF16)
            qf = q.astype(F32)
            lhs = jnp.concatenate([a, (qf * cross_f).astype(BF16), (qf * cross_b).astype(BF16)], axis=1)
            rhs = jnp.concatenate([v, rf_all[blk * RET_CHUNKS_PER_STEP + c], r.astype(BF16)], axis=0)
            o_ref[tok, :] = _dot(lhs, rhs).astype(BF16)
            r = r * decay + _dot((kt.astype(F32) * zeta).astype(BF16), v)
        state[...] = r


def _retention(lg, q, kt, v):
    s = q.shape[0]

    def blk_of(t):
        return jnp.where(t >= RET_STEPS, 2 * RET_STEPS - 1 - t, t)

    def second_blk(t):
        return jnp.where(t >= RET_STEPS, 2 * RET_STEPS - 1 - t, RET_STEPS - 1)

    return pl.pallas_call(
        _retention_kernel,
        grid=(RET_HEADS, 2 * RET_STEPS),
        in_specs=[pl.BlockSpec(memory_space=pltpu.SMEM),
                  pl.BlockSpec((RET_TOK, RET_DK), lambda h, t: (second_blk(t), h)),
                  pl.BlockSpec((RET_DK, RET_TOK), lambda h, t: (h, blk_of(t))),
                  pl.BlockSpec((RET_TOK, RET_DV), lambda h, t: (blk_of(t), h))],
        out_specs=pl.BlockSpec((RET_TOK, RET_DV), lambda h, t: (second_blk(t), h)),
        out_shape=jax.ShapeDtypeStruct((s, ODD_MIX), BF16),
        scratch_shapes=[pltpu.VMEM((RET_NCHUNK, RET_DK, RET_DV), BF16), pltpu.VMEM((RET_DK, RET_DV), F32)],
        compiler_params=_params("arbitrary", "arbitrary"),
        name="retention",
    )(lg, q, kt, v)


def _even_proj_weights(w_in, qn_a, kn_a, qn_b, kn_b):
    qa, ka, va, qb, kb, vb = jnp.split(w_in, np.cumsum([512, 128, 128, 512, 512]).tolist(), axis=-1)
    dup = lambda w: jnp.concatenate([w[:, :HEAD_DIM]] * 2 + [w[:, HEAD_DIM:]] * 2, axis=-1)
    w = jnp.concatenate([qa, dup(ka), dup(va), qb, kb, vb], axis=-1).astype(BF16)
    scale = HEAD_DIM ** -0.5 * LOG2E
    gain = jnp.concatenate([jnp.tile(qn_a, A_Q_HEADS) * scale, jnp.tile(kn_a, 4), jnp.ones((256,), F32),
                            jnp.tile(qn_b, B_HEADS) * scale, jnp.tile(kn_b, B_HEADS),
                            jnp.ones((512,), F32)]).astype(F32)
    return w, gain[None, :]


def kernel(x, attn_norm_e, w_in_e, q_norm_a, k_norm_a, sink_a, q_norm_b, k_norm_b, rpb_b, w_out_e,
           ret_norm_o, w_in_o, decay_fwd_o, decay_bwd_o, ret_gn_o, w_out_o,
           mlp_norm, w_mlp_in, w_mlp_out):
    b, s, d = x.shape
    xs = x.reshape(b * s, d)
    depth = mlp_norm.shape[0]
    for layer in range(depth):
        i = layer // 2
        w1 = w_mlp_in[layer].astype(BF16)
        w2 = w_mlp_out[layer].astype(BF16)
        g_mlp = mlp_norm[layer][None, :]
        if layer % 2 == 0:
            w, gain = _even_proj_weights(w_in_e[i], q_norm_a[i], k_norm_a[i], q_norm_b[i], k_norm_b[i])
            qa, ka2, va2, qb, kb, vb = _inproj_even(xs, attn_norm_e[i][None, :], w, gain)
            oa = _window_attn(sink_a[i].astype(F32) * LOG2E, qa, ka2, va2)
            ob = _na_attn(qb, kb, vb, _na_bias_table(rpb_b[i].astype(F32) * LOG2E))
            xs = _outproj_mlp(xs, (oa, ob), None, w_out_e[i].astype(BF16), g_mlp, w1, w2, name="outproj_mlp_even")
        else:
            wq, wk, wv, wg = (w.astype(BF16) for w in jnp.split(w_in_o[i], [1024, 2048, 4096], axis=-1))
            q, kt, v, gate = _inproj_odd(xs, ret_norm_o[i][None, :], wq, wk.T, wv, wg)
            lg = jnp.stack([jax.nn.log_sigmoid(decay_fwd_o[i].astype(F32)),
                            jax.nn.log_sigmoid(decay_bwd_o[i].astype(F32))])
            y = _retention(lg, q, kt, v)
            xs = _outproj_mlp(xs, (y,), (gate, ret_gn_o[i].astype(F32)[None, :]), w_out_o[i].astype(BF16),
                              g_mlp, w1, w2, name="outproj_mlp_odd")
    return xs.reshape(b, s, d)
```

```python
import functools

import numpy as np
import jax
import jax.numpy as jnp
from jax import lax
from jax.experimental import pallas as pl
from jax.experimental.pallas import tpu as pltpu

D_MODEL = 1024
SEQ = 16384
HEAD_DIM = 64
A_Q_HEADS = 8
A_KV_HEADS = 2
A_GROUP = A_Q_HEADS // A_KV_HEADS
A_WINDOW = 128
A_BLOCK = 128
B_HEADS = 8
GRID_W = 64
GRID_ROWS = SEQ // GRID_W
NA_KH = 8
NA_KW = 16
RET_HEADS = 8
RET_DK = D_MODEL // RET_HEADS
RET_DV = 2 * D_MODEL // RET_HEADS
RET_CHUNK = 256
D_FF = 4 * D_MODEL
ODD_MIX = RET_HEADS * RET_DV
RMS_EPS = 1e-6
GN_EPS = 1e-5

LANES = 128
MXU_N = 256
PROJ_CHUNK = 1024
NEG = -1e30
LOG2E = 1.4426950408889634
VMEM_LIMIT = 60 * 1024 * 1024

F32 = jnp.float32
BF16 = jnp.bfloat16

E_QA, E_KA, E_VA, E_QB, E_KB, E_VB = 0, 512, 768, 1024, 1536, 2048
E_COLS = 2560


def _const_spec(shape):
    return pl.BlockSpec(shape, lambda *_: (0,) * len(shape), pipeline_mode=pl.Buffered(1))


def _params(*sem):
    return pltpu.CompilerParams(dimension_semantics=sem, vmem_limit_bytes=VMEM_LIMIT)


def _rmsnorm_rows(x, g):
    return x * lax.rsqrt(jnp.mean(x * x, axis=-1, keepdims=True) + RMS_EPS) * g


def _silu(g):
    return g * (1.0 / (1.0 + jnp.exp(-g)))


def _dot(a, b):
    return jnp.dot(a, b, preferred_element_type=F32)


def _dot_nt(a, b):
    return lax.dot_general(a, b, (((1,), (1,)), ((), ())), preferred_element_type=F32)


def _inproj_even_kernel(x_ref, g_ref, w_ref, gain_ref, qa_ref, ka_ref, va_ref, qb_ref, kb_ref, vb_ref):
    xb = _rmsnorm_rows(x_ref[...], g_ref[...]).astype(BF16)
    r = lax.broadcasted_iota(jnp.int32, (MXU_N, MXU_N), 0) // HEAD_DIM
    c = lax.broadcasted_iota(jnp.int32, (MXU_N, MXU_N), 1) // HEAD_DIM
    head_mean = jnp.where(r == c, 1.0 / HEAD_DIM, 0.0).astype(BF16)
    outs = ((qa_ref, E_QA, 512, True), (ka_ref, E_KA, 256, True), (va_ref, E_VA, 256, False),
            (qb_ref, E_QB, 512, True), (kb_ref, E_KB, 512, True), (vb_ref, E_VB, 512, False))
    chunks = [(ref, start + off, off, normed) for ref, start, width, normed in outs
              for off in range(0, width, MXU_N)]
    ys = [_dot(xb, w_ref[:, col:col + MXU_N]) for _, col, _, _ in chunks]
    ms = [_dot((y * y).astype(BF16), head_mean) if normed else None
          for y, (_, _, _, normed) in zip(ys, chunks)]
    for y, m, (ref, col, off, normed) in zip(ys, ms, chunks):
        if normed:
            y = y * lax.rsqrt(m + RMS_EPS) * gain_ref[:, col:col + MXU_N]
        ref[:, off:off + MXU_N] = y.astype(BF16)


def _inproj_even(x, g, w, gain, tm=512):
    s = x.shape[0]
    widths = (512, 256, 256, 512, 512, 512)
    return pl.pallas_call(
        _inproj_even_kernel,
        grid=(s // tm,),
        in_specs=[pl.BlockSpec((tm, D_MODEL), lambda i: (i, 0)),
                  _const_spec((1, D_MODEL)), _const_spec((D_MODEL, E_COLS)), _const_spec((1, E_COLS))],
        out_specs=[pl.BlockSpec((tm, wd), lambda i: (i, 0)) for wd in widths],
        out_shape=[jax.ShapeDtypeStruct((s, wd), BF16) for wd in widths],
        compiler_params=_params("parallel"),
        name="inproj_even",
    )(x, g, w, gain)


def _window_kernel(sink_ref, q_ref, kp_ref, kc_ref, kn_ref, vp_ref, vc_ref, vn_ref, o_ref, *, tq):
    i = pl.program_id(0)
    lane = lax.broadcasted_iota(jnp.int32, (1, LANES), 1)
    low_half = lane < HEAD_DIM
    qi = lax.broadcasted_iota(jnp.int32, (A_BLOCK, 3 * A_BLOCK), 0)
    kj = lax.broadcasted_iota(jnp.int32, (A_BLOCK, 3 * A_BLOCK), 1)
    absrel = jnp.abs(kj - A_BLOCK - qi)
    in_window = absrel <= A_WINDOW
    absrel_f = absrel.astype(F32)
    kcol = lax.broadcasted_iota(jnp.int32, (1, 3 * A_BLOCK), 1)
    edge_first = jnp.where((i == 0) & (kcol < A_BLOCK), NEG, 0.0)
    edge_last = jnp.where((i == pl.num_programs(0) - 1) & (kcol >= 2 * A_BLOCK), NEG, 0.0)
    nblk = tq // A_BLOCK
    penalty = []
    for h in range(A_Q_HEADS):
        slope = LOG2E * 2.0 ** (-(8.0 / A_Q_HEADS) * (h + 1.0))
        base = jnp.where(in_window, -slope * absrel_f, NEG)
        penalty.append({0: base + edge_first, nblk - 1: base + edge_last, None: base})
    units = [(j, b) for j in range(A_KV_HEADS) for b in range(nblk)]
    kcat, vcat = [], []
    for j in range(A_KV_HEADS):
        ks = slice(LANES * j, LANES * (j + 1))
        kcat.append(jnp.concatenate([kp_ref[tq - A_BLOCK:, ks], kc_ref[:, ks], kn_ref[:A_BLOCK, ks]], axis=0))
        vcat.append(jnp.concatenate([vp_ref[tq - A_BLOCK:, ks], vc_ref[:, ks], vn_ref[:A_BLOCK, ks]], axis=0))
    scores = []
    for j, b in units:
        rows = slice(A_BLOCK * b, A_BLOCK * (b + 1))
        lhs = []
        for e in range(A_GROUP):
            pair = q_ref[rows, LANES * (2 * j + e // 2):LANES * (2 * j + e // 2 + 1)]
            lhs.append(jnp.where(low_half if e % 2 == 0 else ~low_half, pair, jnp.zeros_like(pair)))
        scores.append(_dot_nt(jnp.concatenate(lhs, axis=0), kcat[j][A_BLOCK * b:A_BLOCK * (b + 3)]))
    probs, invs = [], []
    for (j, b), s_all in zip(units, scores):
        p_unit, inv_unit = [], []
        for e in range(A_GROUP):
            h = A_GROUP * j + e
            sk = sink_ref[h]
            pen = penalty[h]
            s = s_all[A_BLOCK * e:A_BLOCK * (e + 1)] + pen.get(b, pen[None])
            m = jnp.maximum(jnp.max(s, axis=-1, keepdims=True), sk)
            p = jnp.exp2(s - m)
            inv_unit.append(1.0 / (jnp.sum(p, axis=-1, keepdims=True) + jnp.exp2(sk - m)))
            p_unit.append(p.astype(BF16))
        probs.append(jnp.concatenate(p_unit, axis=0))
        invs.append(jnp.concatenate(inv_unit, axis=0))
    for (j, b), p_all, inv in zip(units, probs, invs):
        rows = slice(A_BLOCK * b, A_BLOCK * (b + 1))
        o = _dot(p_all, vcat[j][A_BLOCK * b:A_BLOCK * (b + 3)]) * inv
        for pr in range(A_GROUP // 2):
            lo = o[A_BLOCK * 2 * pr:A_BLOCK * (2 * pr + 1)]
            hi = o[A_BLOCK * (2 * pr + 1):A_BLOCK * (2 * pr + 2)]
            col = LANES * (2 * j + pr)
            o_ref[rows, col:col + LANES] = jnp.where(low_half, lo, hi).astype(BF16)


def _window_attn(sink, qa, ka2, va2, tq=512):
    s = qa.shape[0]
    nb = s // tq
    prev = lambda i: (jnp.maximum(i - 1, 0), 0)
    cur = lambda i: (i, 0)
    nxt = lambda i: (jnp.minimum(i + 1, nb - 1), 0)
    kv = lambda f: pl.BlockSpec((tq, 2 * LANES), f)
    return pl.pallas_call(
        functools.partial(_window_kernel, tq=tq),
        grid=(nb,),
        in_specs=[pl.BlockSpec(memory_space=pltpu.SMEM),
                  pl.BlockSpec((tq, 512), cur), kv(prev), kv(cur), kv(nxt), kv(prev), kv(cur), kv(nxt)],
        out_specs=pl.BlockSpec((tq, 512), cur),
        out_shape=jax.ShapeDtypeStruct((s, 512), BF16),
        compiler_params=_params("parallel"),
        name="window_gqa",
    )(sink, qa, ka2, ka2, ka2, va2, va2, va2)


NA_ROWS_PER_STEP = 8
NA_TOK = NA_ROWS_PER_STEP * GRID_W
NA_KEYS = NA_KH * GRID_W


def _na_kernel(q_ref, kp_ref, kc_ref, kn_ref, vp_ref, vc_ref, vn_ref, bias_ref, o_ref, kcat, vcat):
    i = pl.program_id(0)
    kcat[0:NA_TOK] = kp_ref[...]
    kcat[NA_TOK:2 * NA_TOK] = kc_ref[...]
    kcat[2 * NA_TOK:3 * NA_TOK] = kn_ref[...]
    vcat[0:NA_TOK] = vp_ref[...]
    vcat[NA_TOK:2 * NA_TOK] = vc_ref[...]
    vcat[2 * NA_TOK:3 * NA_TOK] = vn_ref[...]
    lane = lax.broadcasted_iota(jnp.int32, (1, LANES), 1)
    low_half = lane < HEAD_DIM

    def row_body(rr, carry):
        r = i * NA_ROWS_PER_STEP + rr
        rstart = jnp.clip(r - NA_KH // 2, 0, GRID_ROWS - NA_KH)
        start = pl.multiple_of((rstart - (i - 1) * NA_ROWS_PER_STEP) * GRID_W, GRID_W)
        ri0 = rstart - r + NA_KH - 1
        qrows = pl.ds(pl.multiple_of(rr * GRID_W, GRID_W), GRID_W)
        npair = B_HEADS // 2
        scores = []
        for p in range(npair):
            cols = slice(LANES * p, LANES * (p + 1))
            qp = q_ref[qrows, cols]
            lhs = jnp.concatenate([jnp.where(low_half, qp, jnp.zeros_like(qp)),
                                   jnp.where(low_half, jnp.zeros_like(qp), qp)], axis=0)
            scores.append(_dot_nt(lhs, kcat[pl.ds(start, NA_KEYS), cols]))
        probs, invs = [], []
        for p in range(npair):
            bias = jnp.concatenate(
                [jnp.concatenate([bias_ref[2 * p + e, ri0 + 2 * m] for m in range(NA_KH // 2)], axis=1)
                 for e in range(2)], axis=0)
            s = scores[p] + bias
            pexp = jnp.exp2(s - jnp.max(s, axis=-1, keepdims=True))
            invs.append(1.0 / jnp.sum(pexp, axis=-1, keepdims=True))
            probs.append(pexp.astype(BF16))
        for p in range(npair):
            cols = slice(LANES * p, LANES * (p + 1))
            o = _dot(probs[p], vcat[pl.ds(start, NA_KEYS), cols]) * invs[p]
            o_ref[qrows, cols] = jnp.where(low_half, o[:GRID_W], o[GRID_W:]).astype(BF16)
        return carry

    lax.fori_loop(0, NA_ROWS_PER_STEP, row_body, 0, unroll=True)


def _na_bias_table(rpb):
    qc = np.arange(GRID_W)[:, None]
    kc = np.arange(GRID_W)[None, :]
    cstart = np.clip(qc - NA_KW // 2, 0, GRID_W - NA_KW)
    col_valid = (kc >= cstart) & (kc < cstart + NA_KW)
    col_idx = np.clip(kc - qc + NA_KW - 1, 0, 2 * NA_KW - 2)
    onehot = (col_idx.reshape(1, -1) == np.arange(2 * NA_KW - 1)[:, None]).astype(np.float32)
    n_ri = 2 * NA_KH - 1
    t = jnp.dot(rpb.astype(F32).reshape(B_HEADS * n_ri, 2 * NA_KW - 1), jnp.asarray(onehot),
                precision=lax.Precision.HIGHEST).reshape(B_HEADS, n_ri, GRID_W, GRID_W)
    t = jnp.where(col_valid[None, None], t, NEG)
    return jnp.concatenate([t[:, :-1], t[:, 1:]], axis=-1)


def _na_attn(qb, kb, vb, bias):
    s = qb.shape[0]
    nb = s // NA_TOK
    prev = lambda i: (jnp.maximum(i - 1, 0), 0)
    cur = lambda i: (i, 0)
    nxt = lambda i: (jnp.minimum(i + 1, nb - 1), 0)
    blk = lambda f: pl.BlockSpec((NA_TOK, 512), f)
    return pl.pallas_call(
        _na_kernel,
        grid=(nb,),
        in_specs=[blk(cur), blk(prev), blk(cur), blk(nxt), blk(prev), blk(cur), blk(nxt),
                  _const_spec(bias.shape)],
        out_specs=blk(cur),
        out_shape=jax.ShapeDtypeStruct((s, 512), BF16),
        scratch_shapes=[pltpu.VMEM((3 * NA_TOK, 512), BF16), pltpu.VMEM((3 * NA_TOK, 512), BF16)],
        compiler_params=_params("parallel"),
        name="neighbourhood_attn",
    )(qb, kb, kb, kb, vb, vb, vb, bias)


FF_CHUNK = 1024


def _outproj_mlp_kernel(*refs, n_mix, gated):
    x_ref = refs[0]
    mix_refs = refs[1:1 + n_mix]
    rest = refs[1 + n_mix:]
    if gated:
        (y_ref,), gate_ref, gn_ref = mix_refs, rest[0], rest[1]
        wout_ref, g_ref, w1_ref, w2_ref, o_ref = rest[2:]
        proj = None
        for h in range(RET_HEADS):
            cols = slice(h * RET_DV, (h + 1) * RET_DV)
            y = y_ref[:, cols].astype(F32)
            d = y - jnp.mean(y, axis=-1, keepdims=True)
            var = jnp.mean(d * d, axis=-1, keepdims=True)
            yn = d * lax.rsqrt(var + GN_EPS) * gn_ref[:, cols]
            mix_h = (yn * gate_ref[:, cols].astype(F32)).astype(BF16)
            part = _dot(mix_h, wout_ref[cols, :])
            proj = part if proj is None else proj + part
    else:
        wout_ref, g_ref, w1_ref, w2_ref, o_ref = rest
        proj = _dot(jnp.concatenate([m_ref[...] for m_ref in mix_refs], axis=1), wout_ref[...])
    x1 = x_ref[...] + proj
    hb = _rmsnorm_rows(x1, g_ref[...]).astype(BF16)
    acc = x1
    for c in range(0, D_FF, FF_CHUNK):
        a = jnp.maximum(_dot(hb, w1_ref[:, c:c + FF_CHUNK]), 0.0)
        acc = acc + _dot((a * a).astype(BF16), w2_ref[c:c + FF_CHUNK, :])
    o_ref[...] = acc


def _outproj_mlp(x, mixes, gate_gn, wout, g, w1_all, w2_all, layer, tm=512, name="outproj_mlp"):
    s = x.shape[0]
    row = lambda i: (i, 0)
    gated = gate_gn is not None
    acts = tuple(mixes) + ((gate_gn[0],) if gated else ())
    consts = ((gate_gn[1],) if gated else ()) + (wout, g)
    in_specs = [pl.BlockSpec((tm, D_MODEL), row)]
    in_specs += [pl.BlockSpec((tm, m.shape[1]), row) for m in acts]
    in_specs += [_const_spec(c.shape) for c in consts]
    in_specs += [pl.BlockSpec((None,) + w.shape[1:], lambda i: (layer, 0, 0), pipeline_mode=pl.Buffered(1))
                 for w in (w1_all, w2_all)]
    consts += (w1_all, w2_all)
    return pl.pallas_call(
        functools.partial(_outproj_mlp_kernel, n_mix=len(mixes), gated=gated),
        grid=(s // tm,),
        in_specs=in_specs,
        out_specs=pl.BlockSpec((tm, D_MODEL), row),
        out_shape=jax.ShapeDtypeStruct((s, D_MODEL), F32),
        compiler_params=_params("parallel"),
        name=name,
    )(x, *acts, *consts)


def _inproj_odd_kernel(x_ref, g_ref, wq_ref, wkt_ref, wv_ref, wg_ref, q_ref, kt_ref, v_ref, gate_ref):
    xb = _rmsnorm_rows(x_ref[...], g_ref[...]).astype(BF16)
    q_ref[...] = _dot(xb, wq_ref[...]).astype(BF16)
    kt_ref[...] = (_dot_nt(wkt_ref[...], xb) * RET_DK ** -0.5).astype(BF16)
    for off in range(0, ODD_MIX, PROJ_CHUNK):
        v_ref[:, off:off + PROJ_CHUNK] = _dot(xb, wv_ref[:, off:off + PROJ_CHUNK]).astype(BF16)
    for off in range(0, ODD_MIX, PROJ_CHUNK):
        gate_ref[:, off:off + PROJ_CHUNK] = _silu(_dot(xb, wg_ref[:, off:off + PROJ_CHUNK])).astype(BF16)


def _inproj_odd(x, g, w_all, layer, wkt, tm=512):
    s = x.shape[0]
    row = lambda i: (i, 0)
    cols = lambda width, blk: pl.BlockSpec((None, D_MODEL, width), lambda i: (layer, 0, blk),
                                           pipeline_mode=pl.Buffered(1))
    return pl.pallas_call(
        _inproj_odd_kernel,
        grid=(s // tm,),
        in_specs=[pl.BlockSpec((tm, D_MODEL), row), _const_spec((1, D_MODEL)),
                  cols(D_MODEL, 0), _const_spec(wkt.shape), cols(ODD_MIX, 1), cols(ODD_MIX, 2)],
        out_specs=[pl.BlockSpec((tm, D_MODEL), row), pl.BlockSpec((D_MODEL, tm), lambda i: (0, i)),
                   pl.BlockSpec((tm, ODD_MIX), row), pl.BlockSpec((tm, ODD_MIX), row)],
        out_shape=[jax.ShapeDtypeStruct((s, D_MODEL), BF16), jax.ShapeDtypeStruct((D_MODEL, s), BF16),
                   jax.ShapeDtypeStruct((s, ODD_MIX), BF16), jax.ShapeDtypeStruct((s, ODD_MIX), BF16)],
        compiler_params=_params("parallel"),
        name="inproj_odd",
    )(x, g, w_all, wkt, w_all, w_all)


RET_CHUNKS_PER_STEP = 16
RET_TOK = RET_CHUNKS_PER_STEP * RET_CHUNK
RET_STEPS = SEQ // RET_TOK
RET_NCHUNK = SEQ // RET_CHUNK


def _retention_kernel(lg_ref, q_ref, kt_ref, v_ref, o_ref, rf_all, state):
    h = pl.program_id(0)
    t = pl.program_id(1)
    second = t >= RET_STEPS
    blk = jnp.where(second, 2 * RET_STEPS - 1 - t, t)
    lgf = lg_ref[0, h]
    lgb = lg_ref[1, h]
    tok_lane = lax.broadcasted_iota(jnp.int32, (1, RET_CHUNK), 1).astype(F32)
    tok_row = lax.broadcasted_iota(jnp.int32, (RET_CHUNK, 1), 0).astype(F32)
    chunk_len = jnp.full((1, 1), RET_CHUNK, F32)

    @pl.when((t == 0) | (t == RET_STEPS))
    def _():
        state[...] = jnp.zeros_like(state)

    @pl.when(jnp.logical_not(second))
    def _():
        zeta = jnp.exp(lgf * (RET_CHUNK - 1.0 - tok_lane))
        decay = jnp.exp(lgf * chunk_len)
        r = state[...]
        for c in range(RET_CHUNKS_PER_STEP):
            tok = slice(c * RET_CHUNK, (c + 1) * RET_CHUNK)
            kz = (kt_ref[:, tok].astype(F32) * zeta).astype(BF16)
            rf_all[blk * RET_CHUNKS_PER_STEP + c] = r.astype(BF16)
            r = r * decay + _dot(kz, v_ref[tok, :])
        state[...] = r

    @pl.when(second)
    def _():
        diff = (lax.broadcasted_iota(jnp.int32, (RET_CHUNK, RET_CHUNK), 0)
                - lax.broadcasted_iota(jnp.int32, (RET_CHUNK, RET_CHUNK), 1)).astype(F32)
        dsum = jnp.where(diff >= 0, jnp.exp(lgf * jnp.maximum(diff, 0.0)), jnp.exp(lgb * jnp.maximum(-diff, 0.0)))
        cross_f = jnp.exp(lgf * (tok_row + 1.0))
        cross_b = jnp.exp(lgb * (RET_CHUNK - tok_row))
        zeta = jnp.exp(lgb * tok_lane)
        decay = jnp.exp(lgb * chunk_len)
        r = state[...]
        for c in reversed(range(RET_CHUNKS_PER_STEP)):
            tok = slice(c * RET_CHUNK, (c + 1) * RET_CHUNK)
            q = q_ref[tok, :]
            kt = kt_ref[:, tok]
            v = v_ref[tok, :]
            a = (_dot(q, kt) * dsum).astype(BF16)
            qf = q.astype(F32)
            lhs = jnp.concatenate([a, (qf * cross_f).astype(BF16), (qf * cross_b).astype(BF16)], axis=1)
            rhs = jnp.concatenate([v, rf_all[blk * RET_CHUNKS_PER_STEP + c], r.astype(BF16)], axis=0)
            o_ref[tok, :] = _dot(lhs, rhs).astype(BF16)
            r = r * decay + _dot((kt.astype(F32) * zeta).astype(BF16), v)
        state[...] = r


def _retention(lg, q, kt, v):
    s = q.shape[0]

    def blk_of(t):
        return jnp.where(t >= RET_STEPS, 2 * RET_STEPS - 1 - t, t)

    def second_blk(t):
        return jnp.where(t >= RET_STEPS, 2 * RET_STEPS - 1 - t, RET_STEPS - 1)

    return pl.pallas_call(
        _retention_kernel,
        grid=(RET_HEADS, 2 * RET_STEPS),
        in_specs=[pl.BlockSpec(memory_space=pltpu.SMEM),
                  pl.BlockSpec((RET_TOK, RET_DK), lambda h, t: (second_blk(t), h)),
                  pl.BlockSpec((RET_DK, RET_TOK), lambda h, t: (h, blk_of(t))),
                  pl.BlockSpec((RET_TOK, RET_DV), lambda h, t: (blk_of(t), h))],
        out_specs=pl.BlockSpec((RET_TOK, RET_DV), lambda h, t: (second_blk(t), h)),
        out_shape=jax.ShapeDtypeStruct((s, ODD_MIX), BF16),
        scratch_shapes=[pltpu.VMEM((RET_NCHUNK, RET_DK, RET_DV), BF16), pltpu.VMEM((RET_DK, RET_DV), F32)],
        compiler_params=_params("arbitrary", "arbitrary"),
        name="retention",
    )(lg, q, kt, v)


def _even_proj_weights(w_in, qn_a, kn_a, qn_b, kn_b):
    qa, ka, va, qb, kb, vb = jnp.split(w_in, np.cumsum([512, 128, 128, 512, 512]).tolist(), axis=-1)
    dup = lambda w: jnp.concatenate([w[:, :HEAD_DIM]] * 2 + [w[:, HEAD_DIM:]] * 2, axis=-1)
    w = jnp.concatenate([qa, dup(ka), dup(va), qb, kb, vb], axis=-1).astype(BF16)
    scale = HEAD_DIM ** -0.5 * LOG2E
    gain = jnp.concatenate([jnp.tile(qn_a, A_Q_HEADS) * scale, jnp.tile(kn_a, 4), jnp.ones((256,), F32),
                            jnp.tile(qn_b, B_HEADS) * scale, jnp.tile(kn_b, B_HEADS),
                            jnp.ones((512,), F32)]).astype(F32)
    return w, gain[None, :]


def kernel(x, attn_norm_e, w_in_e, q_norm_a, k_norm_a, sink_a, q_norm_b, k_norm_b, rpb_b, w_out_e,
           ret_norm_o, w_in_o, decay_fwd_o, decay_bwd_o, ret_gn_o, w_out_o,
           mlp_norm, w_mlp_in, w_mlp_out):
    b, s, d = x.shape
    xs = x.reshape(b * s, d)
    depth = mlp_norm.shape[0]
    w1 = w_mlp_in.astype(BF16)
    w2 = w_mlp_out.astype(BF16)
    for layer in range(depth):
        i = layer // 2
        g_mlp = mlp_norm[layer][None, :]
        if layer % 2 == 0:
            w, gain = _even_proj_weights(w_in_e[i], q_norm_a[i], k_norm_a[i], q_norm_b[i], k_norm_b[i])
            qa, ka2, va2, qb, kb, vb = _inproj_even(xs, attn_norm_e[i][None, :], w, gain)
            oa = _window_attn(sink_a[i].astype(F32) * LOG2E, qa, ka2, va2)
            ob = _na_attn(qb, kb, vb, _na_bias_table(rpb_b[i].astype(F32) * LOG2E))
            xs = _outproj_mlp(xs, (oa, ob), None, w_out_e[i].astype(BF16), g_mlp, w1, w2, layer,
                              name="outproj_mlp_even")
        else:
            wkt = w_in_o[i, :, D_MODEL:2 * D_MODEL].T.astype(BF16)
            q, kt, v, gate = _inproj_odd(xs, ret_norm_o[i][None, :], w_in_o.astype(BF16), i, wkt)
            lg = jnp.stack([jax.nn.log_sigmoid(decay_fwd_o[i].astype(F32)),
                            jax.nn.log_sigmoid(decay_bwd_o[i].astype(F32))])
            y = _retention(lg, q, kt, v)
            xs = _outproj_mlp(xs, (y,), (gate, ret_gn_o[i].astype(F32)[None, :]), w_out_o[i].astype(BF16),
                              g_mlp, w1, w2, layer, name="outproj_mlp_odd")
    return xs.reshape(b, s, d)
```

```python
import functools

import numpy as np
import jax
import jax.numpy as jnp
from jax import lax
from jax.experimental import pallas as pl
from jax.experimental.pallas import tpu as pltpu

D_MODEL = 1024
SEQ = 16384
HEAD_DIM = 64
A_Q_HEADS = 8
A_KV_HEADS = 2
A_GROUP = A_Q_HEADS // A_KV_HEADS
A_WINDOW = 128
A_BLOCK = 128
B_HEADS = 8
GRID_W = 64
GRID_ROWS = SEQ // GRID_W
NA_KH = 8
NA_KW = 16
RET_HEADS = 8
RET_DK = D_MODEL // RET_HEADS
RET_DV = 2 * D_MODEL // RET_HEADS
RET_CHUNK = 256
D_FF = 4 * D_MODEL
ODD_MIX = RET_HEADS * RET_DV
RMS_EPS = 1e-6
GN_EPS = 1e-5

LANES = 128
MXU_N = 256
PROJ_CHUNK = 1024
NEG = -1e30
LOG2E = 1.4426950408889634
VMEM_LIMIT = 60 * 1024 * 1024

F32 = jnp.float32
BF16 = jnp.bfloat16

E_QA, E_KA, E_VA, E_QB, E_KB, E_VB = 0, 512, 768, 1024, 1536, 2048
E_COLS = 2560


def _const_spec(shape):
    return pl.BlockSpec(shape, lambda *_: (0,) * len(shape), pipeline_mode=pl.Buffered(1))


def _params(*sem):
    return pltpu.CompilerParams(dimension_semantics=sem, vmem_limit_bytes=VMEM_LIMIT)


def _rmsnorm_rows(x, g):
    return x * lax.rsqrt(jnp.mean(x * x, axis=-1, keepdims=True) + RMS_EPS) * g


def _silu(g):
    return g * (1.0 / (1.0 + jnp.exp(-g)))


def _dot(a, b):
    return jnp.dot(a, b, preferred_element_type=F32)


CAST_STEPS = 8


def _slab_spec(w, layer):
    return pl.BlockSpec((None, w.shape[1] // CAST_STEPS, w.shape[2]),
                        lambda t: (layer, jnp.minimum(t, CAST_STEPS - 1), 0))


def _cast_slab(t, src_ref, dst_ref):
    rows = src_ref.shape[0]
    dst_ref[pl.ds(pl.multiple_of(t * rows, rows), rows), :] = src_ref[...].astype(BF16)


def _after_cast(index):
    return lambda t: index(jnp.maximum(t - CAST_STEPS, 0))


def _dot_nt(a, b):
    return lax.dot_general(a, b, (((1,), (1,)), ((), ())), preferred_element_type=F32)


def _inproj_even_kernel(x_ref, g_ref, w_ref, gain_ref, qa_ref, ka_ref, va_ref, qb_ref, kb_ref, vb_ref):
    xb = _rmsnorm_rows(x_ref[...], g_ref[...]).astype(BF16)
    r = lax.broadcasted_iota(jnp.int32, (MXU_N, MXU_N), 0) // HEAD_DIM
    c = lax.broadcasted_iota(jnp.int32, (MXU_N, MXU_N), 1) // HEAD_DIM
    head_mean = jnp.where(r == c, 1.0 / HEAD_DIM, 0.0).astype(BF16)
    outs = ((qa_ref, E_QA, 512, True), (ka_ref, E_KA, 256, True), (va_ref, E_VA, 256, False),
            (qb_ref, E_QB, 512, True), (kb_ref, E_KB, 512, True), (vb_ref, E_VB, 512, False))
    chunks = [(ref, start + off, off, normed) for ref, start, width, normed in outs
              for off in range(0, width, MXU_N)]
    ys = [_dot(xb, w_ref[:, col:col + MXU_N]) for _, col, _, _ in chunks]
    ms = [_dot((y * y).astype(BF16), head_mean) if normed else None
          for y, (_, _, _, normed) in zip(ys, chunks)]
    for y, m, (ref, col, off, normed) in zip(ys, ms, chunks):
        if normed:
            y = y * lax.rsqrt(m + RMS_EPS) * gain_ref[:, col:col + MXU_N]
        ref[:, off:off + MXU_N] = y.astype(BF16)


def _inproj_even(x, g, w, gain, tm=1024):
    s = x.shape[0]
    widths = (512, 256, 256, 512, 512, 512)
    return pl.pallas_call(
        _inproj_even_kernel,
        grid=(s // tm,),
        in_specs=[pl.BlockSpec((tm, D_MODEL), lambda i: (i, 0)),
                  _const_spec((1, D_MODEL)), _const_spec((D_MODEL, E_COLS)), _const_spec((1, E_COLS))],
        out_specs=[pl.BlockSpec((tm, wd), lambda i: (i, 0)) for wd in widths],
        out_shape=[jax.ShapeDtypeStruct((s, wd), BF16) for wd in widths],
        compiler_params=_params("parallel"),
        name="inproj_even",
    )(x, g, w, gain)


def _window_kernel(sink_ref, q_ref, kp_ref, kc_ref, kn_ref, vp_ref, vc_ref, vn_ref, o_ref, *, tq):
    i = pl.program_id(0)
    lane = lax.broadcasted_iota(jnp.int32, (1, LANES), 1)
    low_half = lane < HEAD_DIM
    qi = lax.broadcasted_iota(jnp.int32, (A_BLOCK, 3 * A_BLOCK), 0)
    kj = lax.broadcasted_iota(jnp.int32, (A_BLOCK, 3 * A_BLOCK), 1)
    absrel = jnp.abs(kj - A_BLOCK - qi)
    in_window = absrel <= A_WINDOW
    absrel_f = absrel.astype(F32)
    kcol = lax.broadcasted_iota(jnp.int32, (1, 3 * A_BLOCK), 1)
    edge_first = jnp.where((i == 0) & (kcol < A_BLOCK), NEG, 0.0)
    edge_last = jnp.where((i == pl.num_programs(0) - 1) & (kcol >= 2 * A_BLOCK), NEG, 0.0)
    nblk = tq // A_BLOCK
    penalty = []
    for h in range(A_Q_HEADS):
        slope = LOG2E * 2.0 ** (-(8.0 / A_Q_HEADS) * (h + 1.0))
        base = jnp.where(in_window, -slope * absrel_f, NEG)
        penalty.append({0: base + edge_first, nblk - 1: base + edge_last, None: base})
    units = [(j, b) for j in range(A_KV_HEADS) for b in range(nblk)]
    kcat, vcat = [], []
    for j in range(A_KV_HEADS):
        ks = slice(LANES * j, LANES * (j + 1))
        kcat.append(jnp.concatenate([kp_ref[tq - A_BLOCK:, ks], kc_ref[:, ks], kn_ref[:A_BLOCK, ks]], axis=0))
        vcat.append(jnp.concatenate([vp_ref[tq - A_BLOCK:, ks], vc_ref[:, ks], vn_ref[:A_BLOCK, ks]], axis=0))
    scores = []
    for j, b in units:
        rows = slice(A_BLOCK * b, A_BLOCK * (b + 1))
        lhs = []
        for e in range(A_GROUP):
            pair = q_ref[rows, LANES * (2 * j + e // 2):LANES * (2 * j + e // 2 + 1)]
            lhs.append(jnp.where(low_half if e % 2 == 0 else ~low_half, pair, jnp.zeros_like(pair)))
        scores.append(_dot_nt(jnp.concatenate(lhs, axis=0), kcat[j][A_BLOCK * b:A_BLOCK * (b + 3)]))
    probs, invs = [], []
    for (j, b), s_all in zip(units, scores):
        p_unit, inv_unit = [], []
        for e in range(A_GROUP):
            h = A_GROUP * j + e
            sk = sink_ref[h]
            pen = penalty[h]
            s = s_all[A_BLOCK * e:A_BLOCK * (e + 1)] + pen.get(b, pen[None])
            m = jnp.maximum(jnp.max(s, axis=-1, keepdims=True), sk)
            p = jnp.exp2(s - m)
            inv_unit.append(1.0 / (jnp.sum(p, axis=-1, keepdims=True) + jnp.exp2(sk - m)))
            p_unit.append(p.astype(BF16))
        probs.append(jnp.concatenate(p_unit, axis=0))
        invs.append(jnp.concatenate(inv_unit, axis=0))
    for (j, b), p_all, inv in zip(units, probs, invs):
        rows = slice(A_BLOCK * b, A_BLOCK * (b + 1))
        o = _dot(p_all, vcat[j][A_BLOCK * b:A_BLOCK * (b + 3)]) * inv
        for pr in range(A_GROUP // 2):
            lo = o[A_BLOCK * 2 * pr:A_BLOCK * (2 * pr + 1)]
            hi = o[A_BLOCK * (2 * pr + 1):A_BLOCK * (2 * pr + 2)]
            col = LANES * (2 * j + pr)
            o_ref[rows, col:col + LANES] = jnp.where(low_half, lo, hi).astype(BF16)


def _window_attn(sink, qa, ka2, va2, tq=512):
    s = qa.shape[0]
    nb = s // tq
    prev = lambda i: (jnp.maximum(i - 1, 0), 0)
    cur = lambda i: (i, 0)
    nxt = lambda i: (jnp.minimum(i + 1, nb - 1), 0)
    kv = lambda f: pl.BlockSpec((tq, 2 * LANES), f)
    return pl.pallas_call(
        functools.partial(_window_kernel, tq=tq),
        grid=(nb,),
        in_specs=[pl.BlockSpec(memory_space=pltpu.SMEM),
                  pl.BlockSpec((tq, 512), cur), kv(prev), kv(cur), kv(nxt), kv(prev), kv(cur), kv(nxt)],
        out_specs=pl.BlockSpec((tq, 512), cur),
        out_shape=jax.ShapeDtypeStruct((s, 512), BF16),
        compiler_params=_params("parallel"),
        name="window_gqa",
    )(sink, qa, ka2, ka2, ka2, va2, va2, va2)


NA_ROWS_PER_STEP = 8
NA_TOK = NA_ROWS_PER_STEP * GRID_W
NA_KEYS = NA_KH * GRID_W


def _na_kernel(q_ref, kp_ref, kc_ref, kn_ref, vp_ref, vc_ref, vn_ref, bias_ref, o_ref, kcat, vcat):
    i = pl.program_id(0)
    kcat[0:NA_TOK] = kp_ref[...]
    kcat[NA_TOK:2 * NA_TOK] = kc_ref[...]
    kcat[2 * NA_TOK:3 * NA_TOK] = kn_ref[...]
    vcat[0:NA_TOK] = vp_ref[...]
    vcat[NA_TOK:2 * NA_TOK] = vc_ref[...]
    vcat[2 * NA_TOK:3 * NA_TOK] = vn_ref[...]
    lane = lax.broadcasted_iota(jnp.int32, (1, LANES), 1)
    low_half = lane < HEAD_DIM

    def row_body(rr, carry):
        r = i * NA_ROWS_PER_STEP + rr
        rstart = jnp.clip(r - NA_KH // 2, 0, GRID_ROWS - NA_KH)
        start = pl.multiple_of((rstart - (i - 1) * NA_ROWS_PER_STEP) * GRID_W, GRID_W)
        ri0 = rstart - r + NA_KH - 1
        qrows = pl.ds(pl.multiple_of(rr * GRID_W, GRID_W), GRID_W)
        npair = B_HEADS // 2
        scores = []
        for p in range(npair):
            cols = slice(LANES * p, LANES * (p + 1))
            qp = q_ref[qrows, cols]
            lhs = jnp.concatenate([jnp.where(low_half, qp, jnp.zeros_like(qp)),
                                   jnp.where(low_half, jnp.zeros_like(qp), qp)], axis=0)
            scores.append(_dot_nt(lhs, kcat[pl.ds(start, NA_KEYS), cols]))
        probs, invs = [], []
        for p in range(npair):
            bias = jnp.concatenate(
                [jnp.concatenate([bias_ref[2 * p + e, ri0 + 2 * m] for m in range(NA_KH // 2)], axis=1)
                 for e in range(2)], axis=0)
            s = scores[p] + bias
            pexp = jnp.exp2(s - jnp.max(s, axis=-1, keepdims=True))
            invs.append(1.0 / jnp.sum(pexp, axis=-1, keepdims=True))
            probs.append(pexp.astype(BF16))
        for p in range(npair):
            cols = slice(LANES * p, LANES * (p + 1))
            o = _dot(probs[p], vcat[pl.ds(start, NA_KEYS), cols]) * invs[p]
            o_ref[qrows, cols] = jnp.where(low_half, o[:GRID_W], o[GRID_W:]).astype(BF16)
        return carry

    lax.fori_loop(0, NA_ROWS_PER_STEP, row_body, 0, unroll=True)


def _na_bias_table(rpb):
    qc = np.arange(GRID_W)[:, None]
    kc = np.arange(GRID_W)[None, :]
    cstart = np.clip(qc - NA_KW // 2, 0, GRID_W - NA_KW)
    col_valid = (kc >= cstart) & (kc < cstart + NA_KW)
    col_idx = np.clip(kc - qc + NA_KW - 1, 0, 2 * NA_KW - 2)
    onehot = (col_idx.reshape(1, -1) == np.arange(2 * NA_KW - 1)[:, None]).astype(np.float32)
    n_ri = 2 * NA_KH - 1
    t = jnp.dot(rpb.astype(F32).reshape(B_HEADS * n_ri, 2 * NA_KW - 1), jnp.asarray(onehot),
                precision=lax.Precision.HIGHEST).reshape(B_HEADS, n_ri, GRID_W, GRID_W)
    t = jnp.where(col_valid[None, None], t, NEG)
    return jnp.concatenate([t[:, :-1], t[:, 1:]], axis=-1)


def _na_attn(qb, kb, vb, bias):
    s = qb.shape[0]
    nb = s // NA_TOK
    prev = lambda i: (jnp.maximum(i - 1, 0), 0)
    cur = lambda i: (i, 0)
    nxt = lambda i: (jnp.minimum(i + 1, nb - 1), 0)
    blk = lambda f: pl.BlockSpec((NA_TOK, 512), f)
    return pl.pallas_call(
        _na_kernel,
        grid=(nb,),
        in_specs=[blk(cur), blk(prev), blk(cur), blk(nxt), blk(prev), blk(cur), blk(nxt),
                  _const_spec(bias.shape)],
        out_specs=blk(cur),
        out_shape=jax.ShapeDtypeStruct((s, 512), BF16),
        scratch_shapes=[pltpu.VMEM((3 * NA_TOK, 512), BF16), pltpu.VMEM((3 * NA_TOK, 512), BF16)],
        compiler_params=_params("parallel"),
        name="neighbourhood_attn",
    )(qb, kb, kb, kb, vb, vb, vb, bias)


FF_CHUNK = 1024


def _outproj_mlp_kernel(*refs, n_mix, gated):
    t = pl.program_id(0)
    wout_f32, g_ref, w1_f32, w2_f32, o_ref, wout_ref, w1_ref, w2_ref = refs[-8:]

    @pl.when(t < CAST_STEPS)
    def _():
        for src, dst in ((wout_f32, wout_ref), (w1_f32, w1_ref), (w2_f32, w2_ref)):
            _cast_slab(t, src, dst)

    @pl.when(t >= CAST_STEPS)
    def _():
        _outproj_mlp_tile(refs[:-8], wout_ref, g_ref, w1_ref, w2_ref, o_ref, n_mix=n_mix, gated=gated)


def _outproj_mlp_tile(act_refs, wout_ref, g_ref, w1_ref, w2_ref, o_ref, *, n_mix, gated):
    x_ref = act_refs[0]
    mix_refs = act_refs[1:1 + n_mix]
    if gated:
        (y_ref,), gate_ref, gn_ref = mix_refs, act_refs[1 + n_mix], act_refs[2 + n_mix]
        proj = None
        for h in range(RET_HEADS):
            cols = slice(h * RET_DV, (h + 1) * RET_DV)
            y = y_ref[:, cols].astype(F32)
            d = y - jnp.mean(y, axis=-1, keepdims=True)
            var = jnp.mean(d * d, axis=-1, keepdims=True)
            yn = d * lax.rsqrt(var + GN_EPS) * gn_ref[:, cols]
            mix_h = (yn * gate_ref[:, cols].astype(F32)).astype(BF16)
            part = _dot(mix_h, wout_ref[cols, :])
            proj = part if proj is None else proj + part
    else:
        proj = _dot(jnp.concatenate([m_ref[...] for m_ref in mix_refs], axis=1), wout_ref[...])
    x1 = x_ref[...] + proj
    hb = _rmsnorm_rows(x1, g_ref[...]).astype(BF16)
    acc = x1
    for c in range(0, D_FF, FF_CHUNK):
        a = jnp.maximum(_dot(hb, w1_ref[:, c:c + FF_CHUNK]), 0.0)
        acc = acc + _dot((a * a).astype(BF16), w2_ref[c:c + FF_CHUNK, :])
    o_ref[...] = acc


def _outproj_mlp(x, mixes, gate_gn, wout_all, mix_layer, g, w1_all, w2_all, layer, tm=512, name="outproj_mlp"):
    s = x.shape[0]
    row = _after_cast(lambda i: (i, 0))
    gated = gate_gn is not None
    acts = tuple(mixes) + ((gate_gn[0],) if gated else ())
    in_specs = [pl.BlockSpec((tm, D_MODEL), row)]
    in_specs += [pl.BlockSpec((tm, m.shape[1]), row) for m in acts]
    in_specs += [_const_spec(gate_gn[1].shape)] if gated else []
    in_specs += [_slab_spec(wout_all, mix_layer), _const_spec(g.shape),
                 _slab_spec(w1_all, layer), _slab_spec(w2_all, layer)]
    consts = ((gate_gn[1],) if gated else ()) + (wout_all, g, w1_all, w2_all)
    return pl.pallas_call(
        functools.partial(_outproj_mlp_kernel, n_mix=len(mixes), gated=gated),
        grid=(CAST_STEPS + s // tm,),
        in_specs=in_specs,
        out_specs=pl.BlockSpec((tm, D_MODEL), row),
        out_shape=jax.ShapeDtypeStruct((s, D_MODEL), F32),
        scratch_shapes=[pltpu.VMEM(w.shape[1:], BF16) for w in (wout_all, w1_all, w2_all)],
        compiler_params=_params("arbitrary"),
        name=name,
    )(x, *acts, *consts)


O_Q, O_V, O_G = 0, 2 * D_MODEL, 2 * D_MODEL + ODD_MIX


def _inproj_odd_kernel(x_ref, g_ref, w_f32, wkt_ref, q_ref, kt_ref, v_ref, gate_ref, w_ref):
    t = pl.program_id(0)

    @pl.when(t < CAST_STEPS)
    def _():
        _cast_slab(t, w_f32, w_ref)

    @pl.when(t >= CAST_STEPS)
    def _():
        xb = _rmsnorm_rows(x_ref[...], g_ref[...]).astype(BF16)
        q_ref[...] = _dot(xb, w_ref[:, O_Q:O_Q + D_MODEL]).astype(BF16)
        kt_ref[...] = (_dot_nt(wkt_ref[...], xb) * RET_DK ** -0.5).astype(BF16)
        for off in range(0, ODD_MIX, PROJ_CHUNK):
            v_ref[:, off:off + PROJ_CHUNK] = _dot(xb, w_ref[:, O_V + off:O_V + off + PROJ_CHUNK]).astype(BF16)
        for off in range(0, ODD_MIX, PROJ_CHUNK):
            gate_ref[:, off:off + PROJ_CHUNK] = _silu(
                _dot(xb, w_ref[:, O_G + off:O_G + off + PROJ_CHUNK])).astype(BF16)


def _inproj_odd(x, g, w_all, layer, wkt, tm=1024):
    s = x.shape[0]
    row = _after_cast(lambda i: (i, 0))
    return pl.pallas_call(
        _inproj_odd_kernel,
        grid=(CAST_STEPS + s // tm,),
        in_specs=[pl.BlockSpec((tm, D_MODEL), row), _const_spec((1, D_MODEL)),
                  _slab_spec(w_all, layer), _const_spec(wkt.shape)],
        out_specs=[pl.BlockSpec((tm, D_MODEL), row), pl.BlockSpec((D_MODEL, tm), _after_cast(lambda i: (0, i))),
                   pl.BlockSpec((tm, ODD_MIX), row), pl.BlockSpec((tm, ODD_MIX), row)],
        out_shape=[jax.ShapeDtypeStruct((s, D_MODEL), BF16), jax.ShapeDtypeStruct((D_MODEL, s), BF16),
                   jax.ShapeDtypeStruct((s, ODD_MIX), BF16), jax.ShapeDtypeStruct((s, ODD_MIX), BF16)],
        scratch_shapes=[pltpu.VMEM(w_all.shape[1:], BF16)],
        compiler_params=_params("arbitrary"),
        name="inproj_odd",
    )(x, g, w_all, wkt)


RET_CHUNKS_PER_STEP = 16
RET_TOK = RET_CHUNKS_PER_STEP * RET_CHUNK
RET_STEPS = SEQ // RET_TOK
RET_NCHUNK = SEQ // RET_CHUNK


def _retention_kernel(lg_ref, q_ref, kt_ref, v_ref, o_ref, rf_all, state):
    h = pl.program_id(0)
    t = pl.program_id(1)
    second = t >= RET_STEPS
    blk = jnp.where(second, 2 * RET_STEPS - 1 - t, t)
    lgf = lg_ref[0, h]
    lgb = lg_ref[1, h]
    tok_lane = lax.broadcasted_iota(jnp.int32, (1, RET_CHUNK), 1).astype(F32)
    tok_row = lax.broadcasted_iota(jnp.int32, (RET_CHUNK, 1), 0).astype(F32)
    chunk_len = jnp.full((1, 1), RET_CHUNK, F32)

    @pl.when((t == 0) | (t == RET_STEPS))
    def _():
        state[...] = jnp.zeros_like(state)

    @pl.when(jnp.logical_not(second))
    def _():
        zeta = jnp.exp(lgf * (RET_CHUNK - 1.0 - tok_lane))
        decay = jnp.exp(lgf * chunk_len)
        r = state[...]
        for c in range(RET_CHUNKS_PER_STEP):
            tok = slice(c * RET_CHUNK, (c + 1) * RET_CHUNK)
            kz = (kt_ref[:, tok].astype(F32) * zeta).astype(BF16)
            rf_all[blk * RET_CHUNKS_PER_STEP + c] = r.astype(BF16)
            r = r * decay + _dot(kz, v_ref[tok, :])
        state[...] = r

    @pl.when(second)
    def _():
        diff = (lax.broadcasted_iota(jnp.int32, (RET_CHUNK, RET_CHUNK), 0)
                - lax.broadcasted_iota(jnp.int32, (RET_CHUNK, RET_CHUNK), 1)).astype(F32)
        dsum = jnp.where(diff >= 0, jnp.exp(lgf * jnp.maximum(diff, 0.0)), jnp.exp(lgb * jnp.maximum(-diff, 0.0)))
        cross_f = jnp.exp(lgf * (tok_row + 1.0))
        cross_b = jnp.exp(lgb * (RET_CHUNK - tok_row))
        zeta = jnp.exp(lgb * tok_lane)
        decay = jnp.exp(lgb * chunk_len)
        r = state[...]
        for c in reversed(range(RET_CHUNKS_PER_STEP)):
            tok = slice(c * RET_CHUNK, (c + 1) * RET_CHUNK)
            q = q_ref[tok, :]
            kt = kt_ref[:, tok]
            v = v_ref[tok, :]
            a = (_dot(q, kt) * dsum).astype(BF16)
            qf = q.astype(F32)
            lhs = jnp.concatenate([a, (qf * cross_f).astype(BF16), (qf * cross_b).astype(BF16)], axis=1)
            rhs = jnp.concatenate([v, rf_all[blk * RET_CHUNKS_PER_STEP + c], r.astype(BF16)], axis=0)
            o_ref[tok, :] = _dot(lhs, rhs).astype(BF16)
            r = r * decay + _dot((kt.astype(F32) * zeta).astype(BF16), v)
        state[...] = r


def _retention(lg, q, kt, v):
    s = q.shape[0]

    def blk_of(t):
        return jnp.where(t >= RET_STEPS, 2 * RET_STEPS - 1 - t, t)

    def second_blk(t):
        return jnp.where(t >= RET_STEPS, 2 * RET_STEPS - 1 - t, RET_STEPS - 1)

    return pl.pallas_call(
        _retention_kernel,
        grid=(RET_HEADS, 2 * RET_STEPS),
        in_specs=[pl.BlockSpec(memory_space=pltpu.SMEM),
                  pl.BlockSpec((RET_TOK, RET_DK), lambda h, t: (second_blk(t), h)),
                  pl.BlockSpec((RET_DK, RET_TOK), lambda h, t: (h, blk_of(t))),
                  pl.BlockSpec((RET_TOK, RET_DV), lambda h, t: (blk_of(t), h))],
        out_specs=pl.BlockSpec((RET_TOK, RET_DV), lambda h, t: (second_blk(t), h)),
        out_shape=jax.ShapeDtypeStruct((s, ODD_MIX), BF16),
        scratch_shapes=[pltpu.VMEM((RET_NCHUNK, RET_DK, RET_DV), BF16), pltpu.VMEM((RET_DK, RET_DV), F32)],
        compiler_params=_params("arbitrary", "arbitrary"),
        name="retention",
    )(lg, q, kt, v)


def _even_proj_weights(w_in, qn_a, kn_a, qn_b, kn_b):
    qa, ka, va, qb, kb, vb = jnp.split(w_in, np.cumsum([512, 128, 128, 512, 512]).tolist(), axis=-1)
    dup = lambda w: jnp.concatenate([w[:, :HEAD_DIM]] * 2 + [w[:, HEAD_DIM:]] * 2, axis=-1)
    w = jnp.concatenate([qa, dup(ka), dup(va), qb, kb, vb], axis=-1).astype(BF16)
    scale = HEAD_DIM ** -0.5 * LOG2E
    gain = jnp.concatenate([jnp.tile(qn_a, A_Q_HEADS) * scale, jnp.tile(kn_a, 4), jnp.ones((256,), F32),
                            jnp.tile(qn_b, B_HEADS) * scale, jnp.tile(kn_b, B_HEADS),
                            jnp.ones((512,), F32)]).astype(F32)
    return w, gain[None, :]


def kernel(x, attn_norm_e, w_in_e, q_norm_a, k_norm_a, sink_a, q_norm_b, k_norm_b, rpb_b, w_out_e,
           ret_norm_o, w_in_o, decay_fwd_o, decay_bwd_o, ret_gn_o, w_out_o,
           mlp_norm, w_mlp_in, w_mlp_out):
    b, s, d = x.shape
    xs = x.reshape(b * s, d)
    depth = mlp_norm.shape[0]
    for layer in range(depth):
        i = layer // 2
        g_mlp = mlp_norm[layer][None, :]
        if layer % 2 == 0:
            w, gain = _even_proj_weights(w_in_e[i], q_norm_a[i], k_norm_a[i], q_norm_b[i], k_norm_b[i])
            qa, ka2, va2, qb, kb, vb = _inproj_even(xs, attn_norm_e[i][None, :], w, gain)
            oa = _window_attn(sink_a[i].astype(F32) * LOG2E, qa, ka2, va2)
            ob = _na_attn(qb, kb, vb, _na_bias_table(rpb_b[i].astype(F32) * LOG2E))
            xs = _outproj_mlp(xs, (oa, ob), None, w_out_e, i, g_mlp, w_mlp_in, w_mlp_out, layer,
                              name="outproj_mlp_even")
        else:
            wkt = w_in_o[i, :, D_MODEL:2 * D_MODEL].T.astype(BF16)
            q, kt, v, gate = _inproj_odd(xs, ret_norm_o[i][None, :], w_in_o, i, wkt)
            lg = jnp.stack([jax.nn.log_sigmoid(decay_fwd_o[i].astype(F32)),
                            jax.nn.log_sigmoid(decay_bwd_o[i].astype(F32))])
            y = _retention(lg, q, kt, v)
            xs = _outproj_mlp(xs, (y,), (gate, ret_gn_o[i].astype(F32)[None, :]), w_out_o, i,
                              g_mlp, w_mlp_in, w_mlp_out, layer, name="outproj_mlp_odd")
    return xs.reshape(b, s, d)
```

```python
import functools

import numpy as np
import jax
import jax.numpy as jnp
from jax import lax
from jax.experimental import pallas as pl
from jax.experimental.pallas import tpu as pltpu

D_MODEL = 1024
SEQ = 16384
HEAD_DIM = 64
A_Q_HEADS = 8
A_KV_HEADS = 2
A_GROUP = A_Q_HEADS // A_KV_HEADS
A_WINDOW = 128
A_BLOCK = 128
B_HEADS = 8
GRID_W = 64
GRID_ROWS = SEQ // GRID_W
NA_KH = 8
NA_KW = 16
RET_HEADS = 8
RET_DK = D_MODEL // RET_HEADS
RET_DV = 2 * D_MODEL // RET_HEADS
RET_CHUNK = 256
D_FF = 4 * D_MODEL
ODD_MIX = RET_HEADS * RET_DV
RMS_EPS = 1e-6
GN_EPS = 1e-5

LANES = 128
MXU_N = 256
PROJ_CHUNK = 1024
NEG = -1e30
LOG2E = 1.4426950408889634
VMEM_LIMIT = 60 * 1024 * 1024

F32 = jnp.float32
BF16 = jnp.bfloat16

E_QA, E_KA, E_VA, E_QB, E_KB, E_VB = 0, 512, 640, 768, 1280, 1792
E_COLS = 2304


def _const_spec(shape):
    return pl.BlockSpec(shape, lambda *_: (0,) * len(shape), pipeline_mode=pl.Buffered(1))


def _params(*sem):
    return pltpu.CompilerParams(dimension_semantics=sem, vmem_limit_bytes=VMEM_LIMIT)


def _rmsnorm_rows(x, g):
    return x * lax.rsqrt(jnp.mean(x * x, axis=-1, keepdims=True) + RMS_EPS) * g


def _silu(g):
    return g * (1.0 / (1.0 + jnp.exp(-g)))


def _dot(a, b):
    return jnp.dot(a, b, preferred_element_type=F32)


CAST_STEPS = 8


def _slab_spec(w, layer):
    return pl.BlockSpec((None, w.shape[1] // CAST_STEPS, w.shape[2]),
                        lambda t: (layer, jnp.minimum(t, CAST_STEPS - 1), 0))


def _cast_slab(t, src_ref, dst_ref):
    rows = src_ref.shape[0]
    dst_ref[pl.ds(pl.multiple_of(t * rows, rows), rows), :] = src_ref[...].astype(BF16)


def _after_cast(index):
    return lambda t: index(jnp.maximum(t - CAST_STEPS, 0))


def _dot_nt(a, b):
    return lax.dot_general(a, b, (((1,), (1,)), ((), ())), preferred_element_type=F32)


def _inproj_even_kernel(x_ref, g_ref, w_f32, gain_ref, qa_ref, kva_ref, qb_ref, kb_ref, vb_ref, w_ref):
    t = pl.program_id(0)

    @pl.when(t < CAST_STEPS)
    def _():
        _cast_slab(t, w_f32, w_ref)

    @pl.when(t >= CAST_STEPS)
    def _():
        xb = _rmsnorm_rows(x_ref[...], g_ref[...]).astype(BF16)
        r = lax.broadcasted_iota(jnp.int32, (MXU_N, MXU_N), 0) // HEAD_DIM
        c = lax.broadcasted_iota(jnp.int32, (MXU_N, MXU_N), 1) // HEAD_DIM
        head_mean = jnp.where(r == c, 1.0 / HEAD_DIM, 0.0).astype(BF16)
        is_ka = lax.broadcasted_iota(jnp.int32, (1, MXU_N), 1) < E_VA - E_KA
        outs = ((qa_ref, E_QA, 512, "all"), (kva_ref, E_KA, 256, "ka"), (qb_ref, E_QB, 512, "all"),
                (kb_ref, E_KB, 512, "all"), (vb_ref, E_VB, 512, "none"))
        chunks = [(ref, start + off, off, normed) for ref, start, width, normed in outs
                  for off in range(0, width, MXU_N)]
        ys = [_dot(xb, w_ref[:, col:col + MXU_N]) for _, col, _, _ in chunks]
        ms = [_dot((y * y).astype(BF16), head_mean) if normed != "none" else None
              for y, (_, _, _, normed) in zip(ys, chunks)]
        for y, m, (ref, col, off, normed) in zip(ys, ms, chunks):
            if normed != "none":
                scale = lax.rsqrt(m + RMS_EPS) * gain_ref[:, col:col + MXU_N]
                y = y * (scale if normed == "all" else jnp.where(is_ka, scale, 1.0))
            ref[:, off:off + MXU_N] = y.astype(BF16)


def _inproj_even(x, g, w_all, layer, gain, tm=1024):
    s = x.shape[0]
    row = _after_cast(lambda i: (i, 0))
    widths = (512, 256, 512, 512, 512)
    return pl.pallas_call(
        _inproj_even_kernel,
        grid=(CAST_STEPS + s // tm,),
        in_specs=[pl.BlockSpec((tm, D_MODEL), row), _const_spec((1, D_MODEL)),
                  _slab_spec(w_all, layer), _const_spec((1, E_COLS))],
        out_specs=[pl.BlockSpec((tm, wd), row) for wd in widths],
        out_shape=[jax.ShapeDtypeStruct((s, wd), BF16) for wd in widths],
        scratch_shapes=[pltpu.VMEM(w_all.shape[1:], BF16)],
        compiler_params=_params("arbitrary"),
        name="inproj_even",
    )(x, g, w_all, gain)


def _window_kernel(sink_ref, q_ref, kvp_ref, kvc_ref, kvn_ref, o_ref, *, tq):
    i = pl.program_id(0)
    lane = lax.broadcasted_iota(jnp.int32, (1, LANES), 1)
    low_half = lane < HEAD_DIM
    qi = lax.broadcasted_iota(jnp.int32, (A_BLOCK, 3 * A_BLOCK), 0)
    kj = lax.broadcasted_iota(jnp.int32, (A_BLOCK, 3 * A_BLOCK), 1)
    absrel = jnp.abs(kj - A_BLOCK - qi)
    in_window = absrel <= A_WINDOW
    absrel_f = absrel.astype(F32)
    kcol = lax.broadcasted_iota(jnp.int32, (1, 3 * A_BLOCK), 1)
    edge_first = jnp.where((i == 0) & (kcol < A_BLOCK), NEG, 0.0)
    edge_last = jnp.where((i == pl.num_programs(0) - 1) & (kcol >= 2 * A_BLOCK), NEG, 0.0)
    nblk = tq // A_BLOCK
    penalty = []
    for h in range(A_Q_HEADS):
        slope = LOG2E * 2.0 ** (-(8.0 / A_Q_HEADS) * (h + 1.0))
        base = jnp.where(in_window, -slope * absrel_f, NEG)
        penalty.append({0: base + edge_first, nblk - 1: base + edge_last, None: base})
    units = [(j, b) for j in range(A_KV_HEADS) for b in range(nblk)]
    kv_win = jnp.concatenate([kvp_ref[tq - A_BLOCK:, :], kvc_ref[...], kvn_ref[:A_BLOCK, :]], axis=0)
    kcat, vcat = [], []
    for pair, out in ((kv_win[:, :LANES], kcat), (kv_win[:, LANES:], vcat)):
        swapped = pltpu.roll(pair, HEAD_DIM, axis=1)
        out.append(jnp.where(low_half, pair, swapped))
        out.append(jnp.where(low_half, swapped, pair))
    scores = []
    for j, b in units:
        rows = slice(A_BLOCK * b, A_BLOCK * (b + 1))
        lhs = []
        for e in range(A_GROUP):
            pair = q_ref[rows, LANES * (2 * j + e // 2):LANES * (2 * j + e // 2 + 1)]
            lhs.append(jnp.where(low_half if e % 2 == 0 else ~low_half, pair, jnp.zeros_like(pair)))
        scores.append(_dot_nt(jnp.concatenate(lhs, axis=0), kcat[j][A_BLOCK * b:A_BLOCK * (b + 3)]))
    probs, invs = [], []
    for (j, b), s_all in zip(units, scores):
        p_unit, inv_unit = [], []
        for e in range(A_GROUP):
            h = A_GROUP * j + e
            sk = sink_ref[h]
            pen = penalty[h]
            s = s_all[A_BLOCK * e:A_BLOCK * (e + 1)] + pen.get(b, pen[None])
            m = jnp.maximum(jnp.max(s, axis=-1, keepdims=True), sk)
            p = jnp.exp2(s - m)
            inv_unit.append(1.0 / (jnp.sum(p, axis=-1, keepdims=True) + jnp.exp2(sk - m)))
            p_unit.append(p.astype(BF16))
        probs.append(jnp.concatenate(p_unit, axis=0))
        invs.append(jnp.concatenate(inv_unit, axis=0))
    for (j, b), p_all, inv in zip(units, probs, invs):
        rows = slice(A_BLOCK * b, A_BLOCK * (b + 1))
        o = _dot(p_all, vcat[j][A_BLOCK * b:A_BLOCK * (b + 3)]) * inv
        for pr in range(A_GROUP // 2):
            lo = o[A_BLOCK * 2 * pr:A_BLOCK * (2 * pr + 1)]
            hi = o[A_BLOCK * (2 * pr + 1):A_BLOCK * (2 * pr + 2)]
            col = LANES * (2 * j + pr)
            o_ref[rows, col:col + LANES] = jnp.where(low_half, lo, hi).astype(BF16)


def _window_attn(sink, qa, kva, tq=512):
    s = qa.shape[0]
    nb = s // tq
    prev = lambda i: (jnp.maximum(i - 1, 0), 0)
    cur = lambda i: (i, 0)
    nxt = lambda i: (jnp.minimum(i + 1, nb - 1), 0)
    kv = lambda f: pl.BlockSpec((tq, 2 * LANES), f)
    return pl.pallas_call(
        functools.partial(_window_kernel, tq=tq),
        grid=(nb,),
        in_specs=[pl.BlockSpec(memory_space=pltpu.SMEM),
                  pl.BlockSpec((tq, 512), cur), kv(prev), kv(cur), kv(nxt)],
        out_specs=pl.BlockSpec((tq, 512), cur),
        out_shape=jax.ShapeDtypeStruct((s, 512), BF16),
        compiler_params=_params("parallel"),
        name="window_gqa",
    )(sink, qa, kva, kva, kva)


NA_ROWS_PER_STEP = 8
NA_TOK = NA_ROWS_PER_STEP * GRID_W
NA_KEYS = NA_KH * GRID_W


def _na_kernel(q_ref, kp_ref, kc_ref, kn_ref, vp_ref, vc_ref, vn_ref, bias_ref, o_ref, kcat, vcat):
    i = pl.program_id(0)
    kcat[0:NA_TOK] = kp_ref[...]
    kcat[NA_TOK:2 * NA_TOK] = kc_ref[...]
    kcat[2 * NA_TOK:3 * NA_TOK] = kn_ref[...]
    vcat[0:NA_TOK] = vp_ref[...]
    vcat[NA_TOK:2 * NA_TOK] = vc_ref[...]
    vcat[2 * NA_TOK:3 * NA_TOK] = vn_ref[...]
    lane = lax.broadcasted_iota(jnp.int32, (1, LANES), 1)
    low_half = lane < HEAD_DIM

    def row_body(rr, carry):
        r = i * NA_ROWS_PER_STEP + rr
        rstart = jnp.clip(r - NA_KH // 2, 0, GRID_ROWS - NA_KH)
        start = pl.multiple_of((rstart - (i - 1) * NA_ROWS_PER_STEP) * GRID_W, GRID_W)
        ri0 = rstart - r + NA_KH - 1
        qrows = pl.ds(pl.multiple_of(rr * GRID_W, GRID_W), GRID_W)
        npair = B_HEADS // 2
        scores = []
        for p in range(npair):
            cols = slice(LANES * p, LANES * (p + 1))
            qp = q_ref[qrows, cols]
            lhs = jnp.concatenate([jnp.where(low_half, qp, jnp.zeros_like(qp)),
                                   jnp.where(low_half, jnp.zeros_like(qp), qp)], axis=0)
            scores.append(_dot_nt(lhs, kcat[pl.ds(start, NA_KEYS), cols]))
        probs, invs = [], []
        for p in range(npair):
            bias = jnp.concatenate(
                [jnp.concatenate([bias_ref[2 * p + e, ri0 + 2 * m] for m in range(NA_KH // 2)], axis=1)
                 for e in range(2)], axis=0)
            s = scores[p] + bias
            pexp = jnp.exp2(s - jnp.max(s, axis=-1, keepdims=True))
            invs.append(1.0 / jnp.sum(pexp, axis=-1, keepdims=True))
            probs.append(pexp.astype(BF16))
        for p in range(npair):
            cols = slice(LANES * p, LANES * (p + 1))
            o = _dot(probs[p], vcat[pl.ds(start, NA_KEYS), cols]) * invs[p]
            o_ref[qrows, cols] = jnp.where(low_half, o[:GRID_W], o[GRID_W:]).astype(BF16)
        return carry

    lax.fori_loop(0, NA_ROWS_PER_STEP, row_body, 0, unroll=True)


def _na_bias_table(rpb):
    qc = np.arange(GRID_W)[:, None]
    kc = np.arange(GRID_W)[None, :]
    cstart = np.clip(qc - NA_KW // 2, 0, GRID_W - NA_KW)
    col_valid = (kc >= cstart) & (kc < cstart + NA_KW)
    col_idx = np.clip(kc - qc + NA_KW - 1, 0, 2 * NA_KW - 2)
    onehot = (col_idx.reshape(1, -1) == np.arange(2 * NA_KW - 1)[:, None]).astype(np.float32)
    n_ri = 2 * NA_KH - 1
    t = jnp.dot(rpb.astype(F32).reshape(B_HEADS * n_ri, 2 * NA_KW - 1), jnp.asarray(onehot),
                precision=lax.Precision.HIGHEST).reshape(B_HEADS, n_ri, GRID_W, GRID_W)
    t = jnp.where(col_valid[None, None], t, NEG)
    return jnp.concatenate([t[:, :-1], t[:, 1:]], axis=-1)


def _na_attn(qb, kb, vb, bias):
    s = qb.shape[0]
    nb = s // NA_TOK
    prev = lambda i: (jnp.maximum(i - 1, 0), 0)
    cur = lambda i: (i, 0)
    nxt = lambda i: (jnp.minimum(i + 1, nb - 1), 0)
    blk = lambda f: pl.BlockSpec((NA_TOK, 512), f)
    return pl.pallas_call(
        _na_kernel,
        grid=(nb,),
        in_specs=[blk(cur), blk(prev), blk(cur), blk(nxt), blk(prev), blk(cur), blk(nxt),
                  _const_spec(bias.shape)],
        out_specs=blk(cur),
        out_shape=jax.ShapeDtypeStruct((s, 512), BF16),
        scratch_shapes=[pltpu.VMEM((3 * NA_TOK, 512), BF16), pltpu.VMEM((3 * NA_TOK, 512), BF16)],
        compiler_params=_params("parallel"),
        name="neighbourhood_attn",
    )(qb, kb, kb, kb, vb, vb, vb, bias)


FF_CHUNK = 1024


def _outproj_mlp_kernel(*refs, n_mix, gated):
    t = pl.program_id(0)
    wout_f32, g_ref, w1_f32, w2_f32, o_ref, wout_ref, w1_ref, w2_ref = refs[-8:]

    @pl.when(t < CAST_STEPS)
    def _():
        for src, dst in ((wout_f32, wout_ref), (w1_f32, w1_ref), (w2_f32, w2_ref)):
            _cast_slab(t, src, dst)

    @pl.when(t >= CAST_STEPS)
    def _():
        _outproj_mlp_tile(refs[:-8], wout_ref, g_ref, w1_ref, w2_ref, o_ref, n_mix=n_mix, gated=gated)


def _outproj_mlp_tile(act_refs, wout_ref, g_ref, w1_ref, w2_ref, o_ref, *, n_mix, gated):
    x_ref = act_refs[0]
    mix_refs = act_refs[1:1 + n_mix]
    if gated:
        (y_ref,), gate_ref, gn_ref = mix_refs, act_refs[1 + n_mix], act_refs[2 + n_mix]
        proj = None
        for h in range(RET_HEADS):
            cols = slice(h * RET_DV, (h + 1) * RET_DV)
            y = y_ref[:, cols].astype(F32)
            d = y - jnp.mean(y, axis=-1, keepdims=True)
            var = jnp.mean(d * d, axis=-1, keepdims=True)
            yn = d * lax.rsqrt(var + GN_EPS) * gn_ref[:, cols]
            mix_h = (yn * gate_ref[:, cols].astype(F32)).astype(BF16)
            part = _dot(mix_h, wout_ref[cols, :])
            proj = part if proj is None else proj + part
    else:
        proj = _dot(jnp.concatenate([m_ref[...] for m_ref in mix_refs], axis=1), wout_ref[...])
    x1 = x_ref[...] + proj
    hb = _rmsnorm_rows(x1, g_ref[...]).astype(BF16)
    acc = x1
    for c in range(0, D_FF, FF_CHUNK):
        a = jnp.maximum(_dot(hb, w1_ref[:, c:c + FF_CHUNK]), 0.0)
        acc = acc + _dot((a * a).astype(BF16), w2_ref[c:c + FF_CHUNK, :])
    o_ref[...] = acc


def _outproj_mlp(x, mixes, gate_gn, wout_all, mix_layer, g, w1_all, w2_all, layer, tm=512, name="outproj_mlp"):
    s = x.shape[0]
    row = _after_cast(lambda i: (i, 0))
    gated = gate_gn is not None
    acts = tuple(mixes) + ((gate_gn[0],) if gated else ())
    in_specs = [pl.BlockSpec((tm, D_MODEL), row)]
    in_specs += [pl.BlockSpec((tm, m.shape[1]), row) for m in acts]
    in_specs += [_const_spec(gate_gn[1].shape)] if gated else []
    in_specs += [_slab_spec(wout_all, mix_layer), _const_spec(g.shape),
                 _slab_spec(w1_all, layer), _slab_spec(w2_all, layer)]
    consts = ((gate_gn[1],) if gated else ()) + (wout_all, g, w1_all, w2_all)
    return pl.pallas_call(
        functools.partial(_outproj_mlp_kernel, n_mix=len(mixes), gated=gated),
        grid=(CAST_STEPS + s // tm,),
        in_specs=in_specs,
        out_specs=pl.BlockSpec((tm, D_MODEL), row),
        out_shape=jax.ShapeDtypeStruct((s, D_MODEL), F32),
        scratch_shapes=[pltpu.VMEM(w.shape[1:], BF16) for w in (wout_all, w1_all, w2_all)],
        compiler_params=_params("arbitrary"),
        name=name,
    )(x, *acts, *consts)


O_Q, O_V, O_G = 0, 2 * D_MODEL, 2 * D_MODEL + ODD_MIX


def _inproj_odd_kernel(x_ref, g_ref, w_f32, wkt_ref, q_ref, kt_ref, v_ref, gate_ref, w_ref):
    t = pl.program_id(0)

    @pl.when(t < CAST_STEPS)
    def _():
        _cast_slab(t, w_f32, w_ref)

    @pl.when(t >= CAST_STEPS)
    def _():
        xb = _rmsnorm_rows(x_ref[...], g_ref[...]).astype(BF16)
        q_ref[...] = _dot(xb, w_ref[:, O_Q:O_Q + D_MODEL]).astype(BF16)
        kt_ref[...] = (_dot_nt(wkt_ref[...], xb) * RET_DK ** -0.5).astype(BF16)
        for off in range(0, ODD_MIX, PROJ_CHUNK):
            v_ref[:, off:off + PROJ_CHUNK] = _dot(xb, w_ref[:, O_V + off:O_V + off + PROJ_CHUNK]).astype(BF16)
        for off in range(0, ODD_MIX, PROJ_CHUNK):
            gate_ref[:, off:off + PROJ_CHUNK] = _silu(
                _dot(xb, w_ref[:, O_G + off:O_G + off + PROJ_CHUNK])).astype(BF16)


def _inproj_odd(x, g, w_all, layer, wkt, tm=1024):
    s = x.shape[0]
    row = _after_cast(lambda i: (i, 0))
    return pl.pallas_call(
        _inproj_odd_kernel,
        grid=(CAST_STEPS + s // tm,),
        in_specs=[pl.BlockSpec((tm, D_MODEL), row), _const_spec((1, D_MODEL)),
                  _slab_spec(w_all, layer), _const_spec(wkt.shape)],
        out_specs=[pl.BlockSpec((tm, D_MODEL), row), pl.BlockSpec((D_MODEL, tm), _after_cast(lambda i: (0, i))),
                   pl.BlockSpec((tm, ODD_MIX), row), pl.BlockSpec((tm, ODD_MIX), row)],
        out_shape=[jax.ShapeDtypeStruct((s, D_MODEL), BF16), jax.ShapeDtypeStruct((D_MODEL, s), BF16),
                   jax.ShapeDtypeStruct((s, ODD_MIX), BF16), jax.ShapeDtypeStruct((s, ODD_MIX), BF16)],
        scratch_shapes=[pltpu.VMEM(w_all.shape[1:], BF16)],
        compiler_params=_params("arbitrary"),
        name="inproj_odd",
    )(x, g, w_all, wkt)


RET_CHUNKS_PER_STEP = 16
RET_TOK = RET_CHUNKS_PER_STEP * RET_CHUNK
RET_STEPS = SEQ // RET_TOK
RET_NCHUNK = SEQ // RET_CHUNK


def _retention_kernel(lg_ref, q_ref, kt_ref, v_ref, o_ref, rf_all, ub_all, state):
    h = pl.program_id(0)
    t = pl.program_id(1)
    second = t >= RET_STEPS
    blk = jnp.where(second, 2 * RET_STEPS - 1 - t, t)
    lgf = lg_ref[0, h]
    lgb = lg_ref[1, h]
    tok_lane = lax.broadcasted_iota(jnp.int32, (1, RET_CHUNK), 1).astype(F32)
    tok_row = lax.broadcasted_iota(jnp.int32, (RET_CHUNK, 1), 0).astype(F32)
    chunk_len = jnp.full((1, 1), RET_CHUNK, F32)

    @pl.when((t == 0) | (t == RET_STEPS))
    def _():
        state[...] = jnp.zeros_like(state)

    @pl.when(jnp.logical_not(second))
    def _():
        zeta_f = jnp.exp(lgf * (RET_CHUNK - 1.0 - tok_lane))
        zeta_b = jnp.exp(lgb * tok_lane)
        decay = jnp.exp(lgf * chunk_len)
        r = state[...]
        for c in range(RET_CHUNKS_PER_STEP):
            tok = slice(c * RET_CHUNK, (c + 1) * RET_CHUNK)
            ktf = kt_ref[:, tok].astype(F32)
            kz = jnp.concatenate([(ktf * zeta_f).astype(BF16), (ktf * zeta_b).astype(BF16)], axis=0)
            u = _dot(kz, v_ref[tok, :])
            rf_all[blk * RET_CHUNKS_PER_STEP + c] = r.astype(BF16)
            ub_all[blk * RET_CHUNKS_PER_STEP + c] = u[RET_DK:]
            r = r * decay + u[:RET_DK]
        state[...] = r

    @pl.when(second)
    def _():
        diff = (lax.broadcasted_iota(jnp.int32, (RET_CHUNK, RET_CHUNK), 0)
                - lax.broadcasted_iota(jnp.int32, (RET_CHUNK, RET_CHUNK), 1)).astype(F32)
        dsum = jnp.where(diff >= 0, jnp.exp(lgf * jnp.maximum(diff, 0.0)), jnp.exp(lgb * jnp.maximum(-diff, 0.0)))
        cross_f = jnp.exp(lgf * (tok_row + 1.0))
        cross_b = jnp.exp(lgb * (RET_CHUNK - tok_row))
        decay = jnp.exp(lgb * chunk_len)
        r = state[...]
        for c in reversed(range(RET_CHUNKS_PER_STEP)):
            tok = slice(c * RET_CHUNK, (c + 1) * RET_CHUNK)
            q = q_ref[tok, :]
            kt = kt_ref[:, tok]
            v = v_ref[tok, :]
            a = (_dot(q, kt) * dsum).astype(BF16)
            qf = q.astype(F32)
            lhs = jnp.concatenate([a, (qf * cross_f).astype(BF16), (qf * cross_b).astype(BF16)], axis=1)
            rhs = jnp.concatenate([v, rf_all[blk * RET_CHUNKS_PER_STEP + c], r.astype(BF16)], axis=0)
            o_ref[tok, :] = _dot(lhs, rhs).astype(BF16)
            r = r * decay + ub_all[blk * RET_CHUNKS_PER_STEP + c]
        state[...] = r


def _retention(lg, q, kt, v):
    s = q.shape[0]

    def blk_of(t):
        return jnp.where(t >= RET_STEPS, 2 * RET_STEPS - 1 - t, t)

    def second_blk(t):
        return jnp.where(t >= RET_STEPS, 2 * RET_STEPS - 1 - t, RET_STEPS - 1)

    return pl.pallas_call(
        _retention_kernel,
        grid=(RET_HEADS, 2 * RET_STEPS),
        in_specs=[pl.BlockSpec(memory_space=pltpu.SMEM),
                  pl.BlockSpec((RET_TOK, RET_DK), lambda h, t: (second_blk(t), h)),
                  pl.BlockSpec((RET_DK, RET_TOK), lambda h, t: (h, blk_of(t))),
                  pl.BlockSpec((RET_TOK, RET_DV), lambda h, t: (blk_of(t), h))],
        out_specs=pl.BlockSpec((RET_TOK, RET_DV), lambda h, t: (second_blk(t), h)),
        out_shape=jax.ShapeDtypeStruct((s, ODD_MIX), BF16),
        scratch_shapes=[pltpu.VMEM((RET_NCHUNK, RET_DK, RET_DV), BF16), pltpu.VMEM((RET_NCHUNK, RET_DK, RET_DV), F32),
                        pltpu.VMEM((RET_DK, RET_DV), F32)],
        compiler_params=_params("arbitrary", "arbitrary"),
        name="retention",
    )(lg, q, kt, v)


def _even_qk_gain(qn_a, kn_a, qn_b, kn_b):
    scale = HEAD_DIM ** -0.5 * LOG2E
    gain = jnp.concatenate([jnp.tile(qn_a, A_Q_HEADS) * scale, jnp.tile(kn_a, A_KV_HEADS),
                            jnp.ones((A_KV_HEADS * HEAD_DIM,), F32),
                            jnp.tile(qn_b, B_HEADS) * scale, jnp.tile(kn_b, B_HEADS),
                            jnp.ones((B_HEADS * HEAD_DIM,), F32)]).astype(F32)
    return gain[None, :]


def kernel(x, attn_norm_e, w_in_e, q_norm_a, k_norm_a, sink_a, q_norm_b, k_norm_b, rpb_b, w_out_e,
           ret_norm_o, w_in_o, decay_fwd_o, decay_bwd_o, ret_gn_o, w_out_o,
           mlp_norm, w_mlp_in, w_mlp_out):
    b, s, d = x.shape
    xs = x.reshape(b * s, d)
    depth = mlp_norm.shape[0]
    for layer in range(depth):
        i = layer // 2
        g_mlp = mlp_norm[layer][None, :]
        if layer % 2 == 0:
            gain = _even_qk_gain(q_norm_a[i], k_norm_a[i], q_norm_b[i], k_norm_b[i])
            qa, kva, qb, kb, vb = _inproj_even(xs, attn_norm_e[i][None, :], w_in_e, i, gain)
            oa = _window_attn(sink_a[i].astype(F32) * LOG2E, qa, kva)
            ob = _na_attn(qb, kb, vb, _na_bias_table(rpb_b[i].astype(F32) * LOG2E))
            xs = _outproj_mlp(xs, (oa, ob), None, w_out_e, i, g_mlp, w_mlp_in, w_mlp_out, layer,
                              tm=1024, name="outproj_mlp_even")
        else:
            wkt = w_in_o[i, :, D_MODEL:2 * D_MODEL].T.astype(BF16)
            q, kt, v, gate = _inproj_odd(xs, ret_norm_o[i][None, :], w_in_o, i, wkt)
            lg = jnp.stack([jax.nn.log_sigmoid(decay_fwd_o[i].astype(F32)),
                            jax.nn.log_sigmoid(decay_bwd_o[i].astype(F32))])
            y = _retention(lg, q, kt, v)
            xs = _outproj_mlp(xs, (y,), (gate, ret_gn_o[i].astype(F32)[None, :]), w_out_o, i,
                              g_mlp, w_mlp_in, w_mlp_out, layer, name="outproj_mlp_odd")
    return xs.reshape(b, s, d)
```

```python
import functools

import numpy as np
import jax
import jax.numpy as jnp
from jax import lax
from jax.experimental import pallas as pl
from jax.experimental.pallas import tpu as pltpu

D_MODEL = 1024
SEQ = 16384
HEAD_DIM = 64
A_Q_HEADS = 8
A_KV_HEADS = 2
A_GROUP = A_Q_HEADS // A_KV_HEADS
A_WINDOW = 128
A_BLOCK = 128
B_HEADS = 8
GRID_W = 64
GRID_ROWS = SEQ // GRID_W
NA_KH = 8
NA_KW = 16
RET_HEADS = 8
RET_DK = D_MODEL // RET_HEADS
RET_DV = 2 * D_MODEL // RET_HEADS
RET_CHUNK = 256
D_FF = 4 * D_MODEL
ODD_MIX = RET_HEADS * RET_DV
RMS_EPS = 1e-6
GN_EPS = 1e-5

LANES = 128
MXU_N = 256
PROJ_CHUNK = 1024
NEG = -1e30
LOG2E = 1.4426950408889634
VMEM_LIMIT = 60 * 1024 * 1024

F32 = jnp.float32
BF16 = jnp.bfloat16

E_QA, E_KA, E_VA, E_QB, E_KB, E_VB = 0, 512, 640, 768, 1280, 1792
E_COLS = 2304


def _const_spec(shape):
    return pl.BlockSpec(shape, lambda *_: (0,) * len(shape), pipeline_mode=pl.Buffered(1))


def _params(*sem):
    return pltpu.CompilerParams(dimension_semantics=sem, vmem_limit_bytes=VMEM_LIMIT)


def _rmsnorm_rows(x, g):
    return x * lax.rsqrt(jnp.mean(x * x, axis=-1, keepdims=True) + RMS_EPS) * g


def _silu(g):
    return g * (1.0 / (1.0 + jnp.exp(-g)))


def _dot(a, b):
    return jnp.dot(a, b, preferred_element_type=F32)


CAST_STEPS = 8


def _slab_spec(w, layer):
    return pl.BlockSpec((None, w.shape[1] // CAST_STEPS, w.shape[2]),
                        lambda t: (layer, jnp.minimum(t, CAST_STEPS - 1), 0))


def _cast_slab(t, src_ref, dst_ref):
    rows = src_ref.shape[0]
    dst_ref[pl.ds(pl.multiple_of(t * rows, rows), rows), :] = src_ref[...].astype(BF16)


def _after_cast(index):
    return lambda t: index(jnp.maximum(t - CAST_STEPS, 0))


def _dot_nt(a, b):
    return lax.dot_general(a, b, (((1,), (1,)), ((), ())), preferred_element_type=F32)


def _inproj_even_kernel(x_ref, g_ref, w_f32, gain_ref, qa_ref, ka2_ref, va2_ref, qb_ref, kb_ref, vb_ref, w_ref):
    t = pl.program_id(0)

    @pl.when(t < CAST_STEPS)
    def _():
        _cast_slab(t, w_f32, w_ref)

    @pl.when(t >= CAST_STEPS)
    def _():
        xb = _rmsnorm_rows(x_ref[...], g_ref[...]).astype(BF16)
        r = lax.broadcasted_iota(jnp.int32, (MXU_N, MXU_N), 0) // HEAD_DIM
        c = lax.broadcasted_iota(jnp.int32, (MXU_N, MXU_N), 1) // HEAD_DIM
        head_mean = jnp.where(r == c, 1.0 / HEAD_DIM, 0.0).astype(BF16)
        is_ka = lax.broadcasted_iota(jnp.int32, (1, MXU_N), 1) < E_VA - E_KA
        low_half = lax.broadcasted_iota(jnp.int32, (1, LANES), 1) < HEAD_DIM
        outs = ((qa_ref, E_QA, 512, "all"), (None, E_KA, 256, "ka"), (qb_ref, E_QB, 512, "all"),
                (kb_ref, E_KB, 512, "all"), (vb_ref, E_VB, 512, "none"))
        chunks = [(ref, start + off, off, normed) for ref, start, width, normed in outs
                  for off in range(0, width, MXU_N)]
        ys = [_dot(xb, w_ref[:, col:col + MXU_N]) for _, col, _, _ in chunks]
        ms = [_dot((y * y).astype(BF16), head_mean) if normed != "none" else None
              for y, (_, _, _, normed) in zip(ys, chunks)]
        for y, m, (ref, col, off, normed) in zip(ys, ms, chunks):
            if normed != "none":
                scale = lax.rsqrt(m + RMS_EPS) * gain_ref[:, col:col + MXU_N]
                y = y * (scale if normed == "all" else jnp.where(is_ka, scale, 1.0))
            if ref is not None:
                ref[:, off:off + MXU_N] = y.astype(BF16)
                continue
            for pair, dup_ref in ((y[:, :LANES], ka2_ref), (y[:, LANES:], va2_ref)):
                swapped = pltpu.roll(pair, HEAD_DIM, axis=1)
                dup_ref[:, :LANES] = jnp.where(low_half, pair, swapped).astype(BF16)
                dup_ref[:, LANES:] = jnp.where(low_half, swapped, pair).astype(BF16)


def _inproj_even(x, g, w_all, layer, gain, tm=1024):
    s = x.shape[0]
    row = _after_cast(lambda i: (i, 0))
    widths = (512, 256, 256, 512, 512, 512)
    return pl.pallas_call(
        _inproj_even_kernel,
        grid=(CAST_STEPS + s // tm,),
        in_specs=[pl.BlockSpec((tm, D_MODEL), row), _const_spec((1, D_MODEL)),
                  _slab_spec(w_all, layer), _const_spec((1, E_COLS))],
        out_specs=[pl.BlockSpec((tm, wd), row) for wd in widths],
        out_shape=[jax.ShapeDtypeStruct((s, wd), BF16) for wd in widths],
        scratch_shapes=[pltpu.VMEM(w_all.shape[1:], BF16)],
        compiler_params=_params("arbitrary"),
        name="inproj_even",
    )(x, g, w_all, gain)


def _window_kernel(sink_ref, q_ref, kp_ref, kc_ref, kn_ref, vp_ref, vc_ref, vn_ref, o_ref, *, tq):
    i = pl.program_id(0)
    lane = lax.broadcasted_iota(jnp.int32, (1, LANES), 1)
    low_half = lane < HEAD_DIM
    qi = lax.broadcasted_iota(jnp.int32, (A_BLOCK, 3 * A_BLOCK), 0)
    kj = lax.broadcasted_iota(jnp.int32, (A_BLOCK, 3 * A_BLOCK), 1)
    absrel = jnp.abs(kj - A_BLOCK - qi)
    in_window = absrel <= A_WINDOW
    absrel_f = absrel.astype(F32)
    kcol = lax.broadcasted_iota(jnp.int32, (1, 3 * A_BLOCK), 1)
    edge_first = jnp.where((i == 0) & (kcol < A_BLOCK), NEG, 0.0)
    edge_last = jnp.where((i == pl.num_programs(0) - 1) & (kcol >= 2 * A_BLOCK), NEG, 0.0)
    nblk = tq // A_BLOCK
    penalty = []
    for h in range(A_Q_HEADS):
        slope = LOG2E * 2.0 ** (-(8.0 / A_Q_HEADS) * (h + 1.0))
        base = jnp.where(in_window, -slope * absrel_f, NEG)
        penalty.append({0: base + edge_first, nblk - 1: base + edge_last, None: base})
    units = [(j, b) for j in range(A_KV_HEADS) for b in range(nblk)]
    kcat, vcat = [], []
    for j in range(A_KV_HEADS):
        ks = slice(LANES * j, LANES * (j + 1))
        kcat.append(jnp.concatenate([kp_ref[tq - A_BLOCK:, ks], kc_ref[:, ks], kn_ref[:A_BLOCK, ks]], axis=0))
        vcat.append(jnp.concatenate([vp_ref[tq - A_BLOCK:, ks], vc_ref[:, ks], vn_ref[:A_BLOCK, ks]], axis=0))
    scores = []
    for j, b in units:
        rows = slice(A_BLOCK * b, A_BLOCK * (b + 1))
        lhs = []
        for e in range(A_GROUP):
            pair = q_ref[rows, LANES * (2 * j + e // 2):LANES * (2 * j + e // 2 + 1)]
            lhs.append(jnp.where(low_half if e % 2 == 0 else ~low_half, pair, jnp.zeros_like(pair)))
        scores.append(_dot_nt(jnp.concatenate(lhs, axis=0), kcat[j][A_BLOCK * b:A_BLOCK * (b + 3)]))
    probs, invs = [], []
    for (j, b), s_all in zip(units, scores):
        p_unit, inv_unit = [], []
        for e in range(A_GROUP):
            h = A_GROUP * j + e
            sk = sink_ref[h]
            pen = penalty[h]
            s = s_all[A_BLOCK * e:A_BLOCK * (e + 1)] + pen.get(b, pen[None])
            m = jnp.maximum(jnp.max(s, axis=-1, keepdims=True), sk)
            p = jnp.exp2(s - m)
            inv_unit.append(1.0 / (jnp.sum(p, axis=-1, keepdims=True) + jnp.exp2(sk - m)))
            p_unit.append(p.astype(BF16))
        probs.append(jnp.concatenate(p_unit, axis=0))
        invs.append(jnp.concatenate(inv_unit, axis=0))
    for (j, b), p_all, inv in zip(units, probs, invs):
        rows = slice(A_BLOCK * b, A_BLOCK * (b + 1))
        o = _dot(p_all, vcat[j][A_BLOCK * b:A_BLOCK * (b + 3)]) * inv
        for pr in range(A_GROUP // 2):
            lo = o[A_BLOCK * 2 * pr:A_BLOCK * (2 * pr + 1)]
            hi = o[A_BLOCK * (2 * pr + 1):A_BLOCK * (2 * pr + 2)]
            col = LANES * (2 * j + pr)
            o_ref[rows, col:col + LANES] = jnp.where(low_half, lo, hi).astype(BF16)


def _window_attn(sink, qa, ka2, va2, tq=512):
    s = qa.shape[0]
    nb = s // tq
    prev = lambda i: (jnp.maximum(i - 1, 0), 0)
    cur = lambda i: (i, 0)
    nxt = lambda i: (jnp.minimum(i + 1, nb - 1), 0)
    kv = lambda f: pl.BlockSpec((tq, 2 * LANES), f)
    return pl.pallas_call(
        functools.partial(_window_kernel, tq=tq),
        grid=(nb,),
        in_specs=[pl.BlockSpec(memory_space=pltpu.SMEM),
                  pl.BlockSpec((tq, 512), cur), kv(prev), kv(cur), kv(nxt), kv(prev), kv(cur), kv(nxt)],
        out_specs=pl.BlockSpec((tq, 512), cur),
        out_shape=jax.ShapeDtypeStruct((s, 512), BF16),
        compiler_params=_params("parallel"),
        name="window_gqa",
    )(sink, qa, ka2, ka2, ka2, va2, va2, va2)


NA_ROWS_PER_STEP = 8
NA_TOK = NA_ROWS_PER_STEP * GRID_W
NA_KEYS = NA_KH * GRID_W


def _na_kernel(q_ref, kp_ref, kc_ref, kn_ref, vp_ref, vc_ref, vn_ref, bias_ref, o_ref, kcat, vcat):
    i = pl.program_id(0)
    kcat[0:NA_TOK] = kp_ref[...]
    kcat[NA_TOK:2 * NA_TOK] = kc_ref[...]
    kcat[2 * NA_TOK:3 * NA_TOK] = kn_ref[...]
    vcat[0:NA_TOK] = vp_ref[...]
    vcat[NA_TOK:2 * NA_TOK] = vc_ref[...]
    vcat[2 * NA_TOK:3 * NA_TOK] = vn_ref[...]
    lane = lax.broadcasted_iota(jnp.int32, (1, LANES), 1)
    low_half = lane < HEAD_DIM

    def row_body(rr, carry):
        r = i * NA_ROWS_PER_STEP + rr
        rstart = jnp.clip(r - NA_KH // 2, 0, GRID_ROWS - NA_KH)
        start = pl.multiple_of((rstart - (i - 1) * NA_ROWS_PER_STEP) * GRID_W, GRID_W)
        ri0 = rstart - r + NA_KH - 1
        qrows = pl.ds(pl.multiple_of(rr * GRID_W, GRID_W), GRID_W)
        npair = B_HEADS // 2
        scores = []
        for p in range(npair):
            cols = slice(LANES * p, LANES * (p + 1))
            qp = q_ref[qrows, cols]
            lhs = jnp.concatenate([jnp.where(low_half, qp, jnp.zeros_like(qp)),
                                   jnp.where(low_half, jnp.zeros_like(qp), qp)], axis=0)
            scores.append(_dot_nt(lhs, kcat[pl.ds(start, NA_KEYS), cols]))
        probs, invs = [], []
        for p in range(npair):
            bias = jnp.concatenate(
                [jnp.concatenate([bias_ref[2 * p + e, ri0 + 2 * m] for m in range(NA_KH // 2)], axis=1)
                 for e in range(2)], axis=0)
            s = scores[p] + bias
            pexp = jnp.exp2(s - jnp.max(s, axis=-1, keepdims=True))
            invs.append(1.0 / jnp.sum(pexp, axis=-1, keepdims=True))
            probs.append(pexp.astype(BF16))
        for p in range(npair):
            cols = slice(LANES * p, LANES * (p + 1))
            o = _dot(probs[p], vcat[pl.ds(start, NA_KEYS), cols]) * invs[p]
            o_ref[qrows, cols] = jnp.where(low_half, o[:GRID_W], o[GRID_W:]).astype(BF16)
        return carry

    lax.fori_loop(0, NA_ROWS_PER_STEP, row_body, 0, unroll=True)


def _na_bias_table(rpb):
    qc = np.arange(GRID_W)[:, None]
    kc = np.arange(GRID_W)[None, :]
    cstart = np.clip(qc - NA_KW // 2, 0, GRID_W - NA_KW)
    col_valid = (kc >= cstart) & (kc < cstart + NA_KW)
    col_idx = np.clip(kc - qc + NA_KW - 1, 0, 2 * NA_KW - 2)
    onehot = (col_idx.reshape(1, -1) == np.arange(2 * NA_KW - 1)[:, None]).astype(np.float32)
    n_ri = 2 * NA_KH - 1
    t = jnp.dot(rpb.astype(F32).reshape(B_HEADS * n_ri, 2 * NA_KW - 1), jnp.asarray(onehot),
                precision=lax.Precision.HIGHEST).reshape(B_HEADS, n_ri, GRID_W, GRID_W)
    t = jnp.where(col_valid[None, None], t, NEG)
    return jnp.concatenate([t[:, :-1], t[:, 1:]], axis=-1)


def _na_attn(qb, kb, vb, bias):
    s = qb.shape[0]
    nb = s // NA_TOK
    prev = lambda i: (jnp.maximum(i - 1, 0), 0)
    cur = lambda i: (i, 0)
    nxt = lambda i: (jnp.minimum(i + 1, nb - 1), 0)
    blk = lambda f: pl.BlockSpec((NA_TOK, 512), f)
    return pl.pallas_call(
        _na_kernel,
        grid=(nb,),
        in_specs=[blk(cur), blk(prev), blk(cur), blk(nxt), blk(prev), blk(cur), blk(nxt),
                  _const_spec(bias.shape)],
        out_specs=blk(cur),
        out_shape=jax.ShapeDtypeStruct((s, 512), BF16),
        scratch_shapes=[pltpu.VMEM((3 * NA_TOK, 512), BF16), pltpu.VMEM((3 * NA_TOK, 512), BF16)],
        compiler_params=_params("parallel"),
        name="neighbourhood_attn",
    )(qb, kb, kb, kb, vb, vb, vb, bias)


FF_CHUNK = 1024


def _outproj_mlp_kernel(*refs, n_mix, gated):
    t = pl.program_id(0)
    wout_f32, g_ref, w1_f32, w2_f32, o_ref, wout_ref, w1_ref, w2_ref = refs[-8:]

    @pl.when(t < CAST_STEPS)
    def _():
        for src, dst in ((wout_f32, wout_ref), (w1_f32, w1_ref), (w2_f32, w2_ref)):
            _cast_slab(t, src, dst)

    @pl.when(t >= CAST_STEPS)
    def _():
        _outproj_mlp_tile(refs[:-8], wout_ref, g_ref, w1_ref, w2_ref, o_ref, n_mix=n_mix, gated=gated)


def _outproj_mlp_tile(act_refs, wout_ref, g_ref, w1_ref, w2_ref, o_ref, *, n_mix, gated):
    x_ref = act_refs[0]
    mix_refs = act_refs[1:1 + n_mix]
    if gated:
        (y_ref,), gate_ref, gn_ref = mix_refs, act_refs[1 + n_mix], act_refs[2 + n_mix]
        proj = None
        for h in range(RET_HEADS):
            cols = slice(h * RET_DV, (h + 1) * RET_DV)
            y = y_ref[:, cols].astype(F32)
            d = y - jnp.mean(y, axis=-1, keepdims=True)
            var = jnp.mean(d * d, axis=-1, keepdims=True)
            yn = d * lax.rsqrt(var + GN_EPS) * gn_ref[:, cols]
            mix_h = (yn * gate_ref[:, cols].astype(F32)).astype(BF16)
            part = _dot(mix_h, wout_ref[cols, :])
            proj = part if proj is None else proj + part
    else:
        proj = _dot(jnp.concatenate([m_ref[...] for m_ref in mix_refs], axis=1), wout_ref[...])
    x1 = x_ref[...] + proj
    hb = _rmsnorm_rows(x1, g_ref[...]).astype(BF16)
    acc = x1
    for c in range(0, D_FF, FF_CHUNK):
        a = jnp.maximum(_dot(hb, w1_ref[:, c:c + FF_CHUNK]), 0.0)
        acc = acc + _dot((a * a).astype(BF16), w2_ref[c:c + FF_CHUNK, :])
    o_ref[...] = acc


def _outproj_mlp(x, mixes, gate_gn, wout_all, mix_layer, g, w1_all, w2_all, layer, tm=512, name="outproj_mlp"):
    s = x.shape[0]
    row = _after_cast(lambda i: (i, 0))
    gated = gate_gn is not None
    acts = tuple(mixes) + ((gate_gn[0],) if gated else ())
    in_specs = [pl.BlockSpec((tm, D_MODEL), row)]
    in_specs += [pl.BlockSpec((tm, m.shape[1]), row) for m in acts]
    in_specs += [_const_spec(gate_gn[1].shape)] if gated else []
    in_specs += [_slab_spec(wout_all, mix_layer), _const_spec(g.shape),
                 _slab_spec(w1_all, layer), _slab_spec(w2_all, layer)]
    consts = ((gate_gn[1],) if gated else ()) + (wout_all, g, w1_all, w2_all)
    return pl.pallas_call(
        functools.partial(_outproj_mlp_kernel, n_mix=len(mixes), gated=gated),
        grid=(CAST_STEPS + s // tm,),
        in_specs=in_specs,
        out_specs=pl.BlockSpec((tm, D_MODEL), row),
        out_shape=jax.ShapeDtypeStruct((s, D_MODEL), F32),
        scratch_shapes=[pltpu.VMEM(w.shape[1:], BF16) for w in (wout_all, w1_all, w2_all)],
        compiler_params=_params("arbitrary"),
        name=name,
    )(x, *acts, *consts)


O_Q, O_V, O_G = 0, 2 * D_MODEL, 2 * D_MODEL + ODD_MIX


def _inproj_odd_kernel(x_ref, g_ref, w_f32, wkt_ref, q_ref, kt_ref, v_ref, gate_ref, w_ref):
    t = pl.program_id(0)

    @pl.when(t < CAST_STEPS)
    def _():
        _cast_slab(t, w_f32, w_ref)

    @pl.when(t >= CAST_STEPS)
    def _():
        xb = _rmsnorm_rows(x_ref[...], g_ref[...]).astype(BF16)
        q_ref[...] = _dot(xb, w_ref[:, O_Q:O_Q + D_MODEL]).astype(BF16)
        kt_ref[...] = (_dot_nt(wkt_ref[...], xb) * RET_DK ** -0.5).astype(BF16)
        for off in range(0, ODD_MIX, PROJ_CHUNK):
            v_ref[:, off:off + PROJ_CHUNK] = _dot(xb, w_ref[:, O_V + off:O_V + off + PROJ_CHUNK]).astype(BF16)
        for off in range(0, ODD_MIX, PROJ_CHUNK):
            gate_ref[:, off:off + PROJ_CHUNK] = _silu(
                _dot(xb, w_ref[:, O_G + off:O_G + off + PROJ_CHUNK])).astype(BF16)


def _inproj_odd(x, g, w_all, layer, wkt, tm=1024):
    s = x.shape[0]
    row = _after_cast(lambda i: (i, 0))
    return pl.pallas_call(
        _inproj_odd_kernel,
        grid=(CAST_STEPS + s // tm,),
        in_specs=[pl.BlockSpec((tm, D_MODEL), row), _const_spec((1, D_MODEL)),
                  _slab_spec(w_all, layer), _const_spec(wkt.shape)],
        out_specs=[pl.BlockSpec((tm, D_MODEL), row), pl.BlockSpec((D_MODEL, tm), _after_cast(lambda i: (0, i))),
                   pl.BlockSpec((tm, ODD_MIX), row), pl.BlockSpec((tm, ODD_MIX), row)],
        out_shape=[jax.ShapeDtypeStruct((s, D_MODEL), BF16), jax.ShapeDtypeStruct((D_MODEL, s), BF16),
                   jax.ShapeDtypeStruct((s, ODD_MIX), BF16), jax.ShapeDtypeStruct((s, ODD_MIX), BF16)],
        scratch_shapes=[pltpu.VMEM(w_all.shape[1:], BF16)],
        compiler_params=_params("arbitrary"),
        name="inproj_odd",
    )(x, g, w_all, wkt)


RET_CHUNKS_PER_STEP = 16
RET_TOK = RET_CHUNKS_PER_STEP * RET_CHUNK
RET_STEPS = SEQ // RET_TOK
RET_NCHUNK = SEQ // RET_CHUNK


def _retention_kernel(lg_ref, q_ref, kt_ref, v_ref, o_ref, rf_all, state):
    h = pl.program_id(0)
    t = pl.program_id(1)
    second = t >= RET_STEPS
    blk = jnp.where(second, 2 * RET_STEPS - 1 - t, t)
    lgf = lg_ref[0, h]
    lgb = lg_ref[1, h]
    tok_lane = lax.broadcasted_iota(jnp.int32, (1, RET_CHUNK), 1).astype(F32)
    tok_row = lax.broadcasted_iota(jnp.int32, (RET_CHUNK, 1), 0).astype(F32)
    chunk_len = jnp.full((1, 1), RET_CHUNK, F32)

    @pl.when((t == 0) | (t == RET_STEPS))
    def _():
        state[...] = jnp.zeros_like(state)

    @pl.when(jnp.logical_not(second))
    def _():
        zeta = jnp.exp(lgf * (RET_CHUNK - 1.0 - tok_lane))
        decay = jnp.exp(lgf * chunk_len)
        r = state[...]
        for c in range(RET_CHUNKS_PER_STEP):
            tok = slice(c * RET_CHUNK, (c + 1) * RET_CHUNK)
            kz = (kt_ref[:, tok].astype(F32) * zeta).astype(BF16)
            rf_all[blk * RET_CHUNKS_PER_STEP + c] = r.astype(BF16)
            r = r * decay + _dot(kz, v_ref[tok, :])
        state[...] = r

    @pl.when(second)
    def _():
        diff = (lax.broadcasted_iota(jnp.int32, (RET_CHUNK, RET_CHUNK), 0)
                - lax.broadcasted_iota(jnp.int32, (RET_CHUNK, RET_CHUNK), 1)).astype(F32)
        dsum = jnp.where(diff >= 0, jnp.exp(lgf * jnp.maximum(diff, 0.0)), jnp.exp(lgb * jnp.maximum(-diff, 0.0)))
        cross_f = jnp.exp(lgf * (tok_row + 1.0))
        cross_b = jnp.exp(lgb * (RET_CHUNK - tok_row))
        zeta = jnp.exp(lgb * tok_lane)
        decay = jnp.exp(lgb * chunk_len)
        r = state[...]
        for c in reversed(range(RET_CHUNKS_PER_STEP)):
            tok = slice(c * RET_CHUNK, (c + 1) * RET_CHUNK)
            q = q_ref[tok, :]
            kt = kt_ref[:, tok]
            v = v_ref[tok, :]
            a = (_dot(q, kt) * dsum).astype(BF16)
            qf = q.astype(F32)
            lhs = jnp.concatenate([a, (qf * cross_f).astype(BF16), (qf * cross_b).astype(BF16)], axis=1)
            rhs = jnp.concatenate([v, rf_all[blk * RET_CHUNKS_PER_STEP + c], r.astype(BF16)], axis=0)
            o_ref[tok, :] = _dot(lhs, rhs).astype(BF16)
            r = r * decay + _dot((kt.astype(F32) * zeta).astype(BF16), v)
        state[...] = r


def _retention(lg, q, kt, v):
    s = q.shape[0]

    def blk_of(t):
        return jnp.where(t >= RET_STEPS, 2 * RET_STEPS - 1 - t, t)

    def second_blk(t):
        return jnp.where(t >= RET_STEPS, 2 * RET_STEPS - 1 - t, RET_STEPS - 1)

    return pl.pallas_call(
        _retention_kernel,
        grid=(RET_HEADS, 2 * RET_STEPS),
        in_specs=[pl.BlockSpec(memory_space=pltpu.SMEM),
                  pl.BlockSpec((RET_TOK, RET_DK), lambda h, t: (second_blk(t), h)),
                  pl.BlockSpec((RET_DK, RET_TOK), lambda h, t: (h, blk_of(t))),
                  pl.BlockSpec((RET_TOK, RET_DV), lambda h, t: (blk_of(t), h))],
        out_specs=pl.BlockSpec((RET_TOK, RET_DV), lambda h, t: (second_blk(t), h)),
        out_shape=jax.ShapeDtypeStruct((s, ODD_MIX), BF16),
        scratch_shapes=[pltpu.VMEM((RET_NCHUNK, RET_DK, RET_DV), BF16), pltpu.VMEM((RET_DK, RET_DV), F32)],
        compiler_params=_params("arbitrary", "arbitrary"),
        name="retention",
    )(lg, q, kt, v)


def _even_qk_gain(qn_a, kn_a, qn_b, kn_b):
    scale = HEAD_DIM ** -0.5 * LOG2E
    gain = jnp.concatenate([jnp.tile(qn_a, A_Q_HEADS) * scale, jnp.tile(kn_a, A_KV_HEADS),
                            jnp.ones((A_KV_HEADS * HEAD_DIM,), F32),
                            jnp.tile(qn_b, B_HEADS) * scale, jnp.tile(kn_b, B_HEADS),
                            jnp.ones((B_HEADS * HEAD_DIM,), F32)]).astype(F32)
    return gain[None, :]


def kernel(x, attn_norm_e, w_in_e, q_norm_a, k_norm_a, sink_a, q_norm_b, k_norm_b, rpb_b, w_out_e,
           ret_norm_o, w_in_o, decay_fwd_o, decay_bwd_o, ret_gn_o, w_out_o,
           mlp_norm, w_mlp_in, w_mlp_out):
    b, s, d = x.shape
    xs = x.reshape(b * s, d)
    depth = mlp_norm.shape[0]
    for layer in range(depth):
        i = layer // 2
        g_mlp = mlp_norm[layer][None, :]
        if layer % 2 == 0:
            gain = _even_qk_gain(q_norm_a[i], k_norm_a[i], q_norm_b[i], k_norm_b[i])
            qa, ka2, va2, qb, kb, vb = _inproj_even(xs, attn_norm_e[i][None, :], w_in_e, i, gain)
            oa = _window_attn(sink_a[i].astype(F32) * LOG2E, qa, ka2, va2)
            ob = _na_attn(qb, kb, vb, _na_bias_table(rpb_b[i].astype(F32) * LOG2E))
            xs = _outproj_mlp(xs, (oa, ob), None, w_out_e, i, g_mlp, w_mlp_in, w_mlp_out, layer,
                              tm=1024, name="outproj_mlp_even")
        else:
            wkt = w_in_o[i, :, D_MODEL:2 * D_MODEL].T.astype(BF16)
            q, kt, v, gate = _inproj_odd(xs, ret_norm_o[i][None, :], w_in_o, i, wkt)
            lg = jnp.stack([jax.nn.log_sigmoid(decay_fwd_o[i].astype(F32)),
                            jax.nn.log_sigmoid(decay_bwd_o[i].astype(F32))])
            y = _retention(lg, q, kt, v)
            xs = _outproj_mlp(xs, (y,), (gate, ret_gn_o[i].astype(F32)[None, :]), w_out_o, i,
                              g_mlp, w_mlp_in, w_mlp_out, layer, name="outproj_mlp_odd")
    return xs.reshape(b, s, d)
```

```python
import functools

import numpy as np
import jax
import jax.numpy as jnp
from jax import lax
from jax.experimental import pallas as pl
from jax.experimental.pallas import tpu as pltpu

D_MODEL = 1024
SEQ = 16384
HEAD_DIM = 64
A_Q_HEADS = 8
A_KV_HEADS = 2
A_GROUP = A_Q_HEADS // A_KV_HEADS
A_WINDOW = 128
A_BLOCK = 128
B_HEADS = 8
GRID_W = 64
GRID_ROWS = SEQ // GRID_W
NA_KH = 8
NA_KW = 16
RET_HEADS = 8
RET_DK = D_MODEL // RET_HEADS
RET_DV = 2 * D_MODEL // RET_HEADS
RET_CHUNK = 256
D_FF = 4 * D_MODEL
ODD_MIX = RET_HEADS * RET_DV
RMS_EPS = 1e-6
GN_EPS = 1e-5

LANES = 128
MXU_N = 256
PROJ_CHUNK = 1024
NEG = -1e30
LOG2E = 1.4426950408889634
VMEM_LIMIT = 60 * 1024 * 1024

F32 = jnp.float32
BF16 = jnp.bfloat16

E_QA, E_KA, E_VA, E_QB, E_KB, E_VB = 0, 512, 640, 768, 1280, 1792
E_COLS = 2304


def _const_spec(shape):
    return pl.BlockSpec(shape, lambda *_: (0,) * len(shape), pipeline_mode=pl.Buffered(1))


def _params(*sem):
    return pltpu.CompilerParams(dimension_semantics=sem, vmem_limit_bytes=VMEM_LIMIT)


def _rmsnorm_rows(x, g):
    return x * lax.rsqrt(jnp.mean(x * x, axis=-1, keepdims=True) + RMS_EPS) * g


def _silu(g):
    return g * (1.0 / (1.0 + jnp.exp(-g)))


def _dot(a, b):
    return jnp.dot(a, b, preferred_element_type=F32)


CAST_STEPS = 8


def _slab_spec(w, layer):
    return pl.BlockSpec((None, w.shape[1] // CAST_STEPS, w.shape[2]),
                        lambda t: (layer, jnp.minimum(t, CAST_STEPS - 1), 0))


def _cast_slab(t, src_ref, dst_ref):
    rows = src_ref.shape[0]
    dst_ref[pl.ds(pl.multiple_of(t * rows, rows), rows), :] = src_ref[...].astype(BF16)


def _after_cast(index):
    return lambda t: index(jnp.maximum(t - CAST_STEPS, 0))


def _dot_nt(a, b):
    return lax.dot_general(a, b, (((1,), (1,)), ((), ())), preferred_element_type=F32)


def _inproj_even_kernel(x_ref, g_ref, w_f32, gain_ref, qa_ref, ka2_ref, va2_ref, qb_ref, kb_ref, vb_ref, w_ref):
    t = pl.program_id(0)

    @pl.when(t < CAST_STEPS)
    def _():
        _cast_slab(t, w_f32, w_ref)

    @pl.when(t >= CAST_STEPS)
    def _():
        xb = _rmsnorm_rows(x_ref[...], g_ref[...]).astype(BF16)
        r = lax.broadcasted_iota(jnp.int32, (MXU_N, MXU_N), 0) // HEAD_DIM
        c = lax.broadcasted_iota(jnp.int32, (MXU_N, MXU_N), 1) // HEAD_DIM
        head_mean = jnp.where(r == c, 1.0 / HEAD_DIM, 0.0).astype(BF16)
        is_ka = lax.broadcasted_iota(jnp.int32, (1, MXU_N), 1) < E_VA - E_KA
        low_half = lax.broadcasted_iota(jnp.int32, (1, LANES), 1) < HEAD_DIM
        outs = ((qa_ref, E_QA, 512, "all"), (None, E_KA, 256, "ka"), (qb_ref, E_QB, 512, "all"),
                (kb_ref, E_KB, 512, "all"), (vb_ref, E_VB, 512, "none"))
        chunks = [(ref, start + off, off, normed) for ref, start, width, normed in outs
                  for off in range(0, width, MXU_N)]
        ys = [_dot(xb, w_ref[:, col:col + MXU_N]) for _, col, _, _ in chunks]
        ms = [_dot((y * y).astype(BF16), head_mean) if normed != "none" else None
              for y, (_, _, _, normed) in zip(ys, chunks)]
        for y, m, (ref, col, off, normed) in zip(ys, ms, chunks):
            if normed != "none":
                scale = lax.rsqrt(m + RMS_EPS) * gain_ref[:, col:col + MXU_N]
                y = y * (scale if normed == "all" else jnp.where(is_ka, scale, 1.0))
            if ref is not None:
                ref[:, off:off + MXU_N] = y.astype(BF16)
                continue
            for pair, dup_ref in ((y[:, :LANES], ka2_ref), (y[:, LANES:], va2_ref)):
                swapped = pltpu.roll(pair, HEAD_DIM, axis=1)
                dup_ref[:, :LANES] = jnp.where(low_half, pair, swapped).astype(BF16)
                dup_ref[:, LANES:] = jnp.where(low_half, swapped, pair).astype(BF16)


def _inproj_even(x, g, w_all, layer, gain, tm=1024):
    s = x.shape[0]
    row = _after_cast(lambda i: (i, 0))
    widths = (512, 256, 256, 512, 512, 512)
    return pl.pallas_call(
        _inproj_even_kernel,
        grid=(CAST_STEPS + s // tm,),
        in_specs=[pl.BlockSpec((tm, D_MODEL), row), _const_spec((1, D_MODEL)),
                  _slab_spec(w_all, layer), _const_spec((1, E_COLS))],
        out_specs=[pl.BlockSpec((tm, wd), row) for wd in widths],
        out_shape=[jax.ShapeDtypeStruct((s, wd), BF16) for wd in widths],
        scratch_shapes=[pltpu.VMEM(w_all.shape[1:], BF16)],
        compiler_params=_params("arbitrary"),
        name="inproj_even",
    )(x, g, w_all, gain)


def _window_kernel(sink_ref, q_ref, kp_ref, kc_ref, kn_ref, vp_ref, vc_ref, vn_ref, o_ref, *, tq):
    i = pl.program_id(0)
    lane = lax.broadcasted_iota(jnp.int32, (1, LANES), 1)
    low_half = lane < HEAD_DIM
    qi = lax.broadcasted_iota(jnp.int32, (A_BLOCK, 3 * A_BLOCK), 0)
    kj = lax.broadcasted_iota(jnp.int32, (A_BLOCK, 3 * A_BLOCK), 1)
    absrel = jnp.abs(kj - A_BLOCK - qi)
    in_window = absrel <= A_WINDOW
    absrel_f = absrel.astype(F32)
    kcol = lax.broadcasted_iota(jnp.int32, (1, 3 * A_BLOCK), 1)
    edge_first = jnp.where((i == 0) & (kcol < A_BLOCK), NEG, 0.0)
    edge_last = jnp.where((i == pl.num_programs(0) - 1) & (kcol >= 2 * A_BLOCK), NEG, 0.0)
    nblk = tq // A_BLOCK
    penalty = []
    for h in range(A_Q_HEADS):
        slope = LOG2E * 2.0 ** (-(8.0 / A_Q_HEADS) * (h + 1.0))
        base = jnp.where(in_window, -slope * absrel_f, NEG)
        penalty.append({0: base + edge_first, nblk - 1: base + edge_last, None: base})
    units = [(j, b) for j in range(A_KV_HEADS) for b in range(nblk)]
    kcat, vcat = [], []
    for j in range(A_KV_HEADS):
        ks = slice(LANES * j, LANES * (j + 1))
        kcat.append(jnp.concatenate([kp_ref[tq - A_BLOCK:, ks], kc_ref[:, ks], kn_ref[:A_BLOCK, ks]], axis=0))
        vcat.append(jnp.concatenate([vp_ref[tq - A_BLOCK:, ks], vc_ref[:, ks], vn_ref[:A_BLOCK, ks]], axis=0))
    def scores_of(j, b):
        rows = slice(A_BLOCK * b, A_BLOCK * (b + 1))
        lhs = []
        for e in range(A_GROUP):
            pair = q_ref[rows, LANES * (2 * j + e // 2):LANES * (2 * j + e // 2 + 1)]
            lhs.append(jnp.where(low_half if e % 2 == 0 else ~low_half, pair, jnp.zeros_like(pair)))
        return _dot_nt(jnp.concatenate(lhs, axis=0), kcat[j][A_BLOCK * b:A_BLOCK * (b + 3)])

    def softmax_of(j, b, s_all):
        p_unit, inv_unit = [], []
        for e in range(A_GROUP):
            h = A_GROUP * j + e
            sk = sink_ref[h]
            pen = penalty[h]
            s = s_all[A_BLOCK * e:A_BLOCK * (e + 1)] + pen.get(b, pen[None])
            m = jnp.maximum(jnp.max(s, axis=-1, keepdims=True), sk)
            p = jnp.exp2(s - m)
            inv_unit.append(1.0 / (jnp.sum(p, axis=-1, keepdims=True) + jnp.exp2(sk - m)))
            p_unit.append(p.astype(BF16))
        return jnp.concatenate(p_unit, axis=0), jnp.concatenate(inv_unit, axis=0)

    def emit(j, b, p_all, inv):
        rows = slice(A_BLOCK * b, A_BLOCK * (b + 1))
        o = _dot(p_all, vcat[j][A_BLOCK * b:A_BLOCK * (b + 3)]) * inv
        for pr in range(A_GROUP // 2):
            lo = o[A_BLOCK * 2 * pr:A_BLOCK * (2 * pr + 1)]
            hi = o[A_BLOCK * (2 * pr + 1):A_BLOCK * (2 * pr + 2)]
            col = LANES * (2 * j + pr)
            o_ref[rows, col:col + LANES] = jnp.where(low_half, lo, hi).astype(BF16)

    scores = [scores_of(j, b) for j, b in units]
    probs = [softmax_of(j, b, s_all) for (j, b), s_all in zip(units, scores)]
    for (j, b), (p_all, inv) in zip(units, probs):
        emit(j, b, p_all, inv)


def _window_attn(sink, qa, ka2, va2, tq=512):
    s = qa.shape[0]
    nb = s // tq
    prev = lambda i: (jnp.maximum(i - 1, 0), 0)
    cur = lambda i: (i, 0)
    nxt = lambda i: (jnp.minimum(i + 1, nb - 1), 0)
    kv = lambda f: pl.BlockSpec((tq, 2 * LANES), f)
    return pl.pallas_call(
        functools.partial(_window_kernel, tq=tq),
        grid=(nb,),
        in_specs=[pl.BlockSpec(memory_space=pltpu.SMEM),
                  pl.BlockSpec((tq, 512), cur), kv(prev), kv(cur), kv(nxt), kv(prev), kv(cur), kv(nxt)],
        out_specs=pl.BlockSpec((tq, 512), cur),
        out_shape=jax.ShapeDtypeStruct((s, 512), BF16),
        compiler_params=_params("parallel"),
        name="window_gqa",
    )(sink, qa, ka2, ka2, ka2, va2, va2, va2)


NA_ROWS_PER_STEP = 8
NA_TOK = NA_ROWS_PER_STEP * GRID_W
NA_KEYS = NA_KH * GRID_W


def _na_kernel(q_ref, kp_ref, kc_ref, kn_ref, vp_ref, vc_ref, vn_ref, bias_ref, o_ref, kcat, vcat):
    i = pl.program_id(0)
    kcat[0:NA_TOK] = kp_ref[...]
    kcat[NA_TOK:2 * NA_TOK] = kc_ref[...]
    kcat[2 * NA_TOK:3 * NA_TOK] = kn_ref[...]
    vcat[0:NA_TOK] = vp_ref[...]
    vcat[NA_TOK:2 * NA_TOK] = vc_ref[...]
    vcat[2 * NA_TOK:3 * NA_TOK] = vn_ref[...]
    lane = lax.broadcasted_iota(jnp.int32, (1, LANES), 1)
    low_half = lane < HEAD_DIM

    def row_body(rr, carry):
        r = i * NA_ROWS_PER_STEP + rr
        rstart = jnp.clip(r - NA_KH // 2, 0, GRID_ROWS - NA_KH)
        start = pl.multiple_of((rstart - (i - 1) * NA_ROWS_PER_STEP) * GRID_W, GRID_W)
        ri0 = rstart - r + NA_KH - 1
        qrows = pl.ds(pl.multiple_of(rr * GRID_W, GRID_W), GRID_W)
        npair = B_HEADS // 2
        scores = []
        for p in range(npair):
            cols = slice(LANES * p, LANES * (p + 1))
            qp = q_ref[qrows, cols]
            lhs = jnp.concatenate([jnp.where(low_half, qp, jnp.zeros_like(qp)),
                                   jnp.where(low_half, jnp.zeros_like(qp), qp)], axis=0)
            scores.append(_dot_nt(lhs, kcat[pl.ds(start, NA_KEYS), cols]))
        probs, invs = [], []
        for p in range(npair):
            bias = jnp.concatenate(
                [jnp.concatenate([bias_ref[2 * p + e, ri0 + 2 * m] for m in range(NA_KH // 2)], axis=1)
                 for e in range(2)], axis=0)
            s = scores[p] + bias
            pexp = jnp.exp2(s - jnp.max(s, axis=-1, keepdims=True))
            invs.append(1.0 / jnp.sum(pexp, axis=-1, keepdims=True))
            probs.append(pexp.astype(BF16))
        for p in range(npair):
            cols = slice(LANES * p, LANES * (p + 1))
            o = _dot(probs[p], vcat[pl.ds(start, NA_KEYS), cols]) * invs[p]
            o_ref[qrows, cols] = jnp.where(low_half, o[:GRID_W], o[GRID_W:]).astype(BF16)
        return carry

    lax.fori_loop(0, NA_ROWS_PER_STEP, row_body, 0, unroll=True)


def _na_bias_table(rpb):
    qc = np.arange(GRID_W)[:, None]
    kc = np.arange(GRID_W)[None, :]
    cstart = np.clip(qc - NA_KW // 2, 0, GRID_W - NA_KW)
    col_valid = (kc >= cstart) & (kc < cstart + NA_KW)
    col_idx = np.clip(kc - qc + NA_KW - 1, 0, 2 * NA_KW - 2)
    onehot = (col_idx.reshape(1, -1) == np.arange(2 * NA_KW - 1)[:, None]).astype(np.float32)
    n_ri = 2 * NA_KH - 1
    t = jnp.dot(rpb.astype(F32).reshape(B_HEADS * n_ri, 2 * NA_KW - 1), jnp.asarray(onehot),
                precision=lax.Precision.HIGHEST).reshape(B_HEADS, n_ri, GRID_W, GRID_W)
    t = jnp.where(col_valid[None, None], t, NEG)
    return jnp.concatenate([t[:, :-1], t[:, 1:]], axis=-1)


def _na_attn(qb, kb, vb, bias):
    s = qb.shape[0]
    nb = s // NA_TOK
    prev = lambda i: (jnp.maximum(i - 1, 0), 0)
    cur = lambda i: (i, 0)
    nxt = lambda i: (jnp.minimum(i + 1, nb - 1), 0)
    blk = lambda f: pl.BlockSpec((NA_TOK, 512), f)
    return pl.pallas_call(
        _na_kernel,
        grid=(nb,),
        in_specs=[blk(cur), blk(prev), blk(cur), blk(nxt), blk(prev), blk(cur), blk(nxt),
                  _const_spec(bias.shape)],
        out_specs=blk(cur),
        out_shape=jax.ShapeDtypeStruct((s, 512), BF16),
        scratch_shapes=[pltpu.VMEM((3 * NA_TOK, 512), BF16), pltpu.VMEM((3 * NA_TOK, 512), BF16)],
        compiler_params=_params("parallel"),
        name="neighbourhood_attn",
    )(qb, kb, kb, kb, vb, vb, vb, bias)


FF_CHUNK = 1024


def _outproj_mlp_kernel(*refs, n_mix, gated):
    t = pl.program_id(0)
    wout_f32, g_ref, w1_f32, w2_f32, o_ref, wout_ref, w1_ref, w2_ref = refs[-8:]

    @pl.when(t < CAST_STEPS)
    def _():
        for src, dst in ((wout_f32, wout_ref), (w1_f32, w1_ref), (w2_f32, w2_ref)):
            _cast_slab(t, src, dst)

    @pl.when(t >= CAST_STEPS)
    def _():
        _outproj_mlp_tile(refs[:-8], wout_ref, g_ref, w1_ref, w2_ref, o_ref, n_mix=n_mix, gated=gated)


def _outproj_mlp_tile(act_refs, wout_ref, g_ref, w1_ref, w2_ref, o_ref, *, n_mix, gated):
    x_ref = act_refs[0]
    mix_refs = act_refs[1:1 + n_mix]
    if gated:
        (y_ref,), gate_ref, gn_ref = mix_refs, act_refs[1 + n_mix], act_refs[2 + n_mix]
        proj = None
        for h in range(RET_HEADS):
            cols = slice(h * RET_DV, (h + 1) * RET_DV)
            y = y_ref[:, cols].astype(F32)
            d = y - jnp.mean(y, axis=-1, keepdims=True)
            var = jnp.mean(d * d, axis=-1, keepdims=True)
            yn = d * lax.rsqrt(var + GN_EPS) * gn_ref[:, cols]
            mix_h = (yn * gate_ref[:, cols].astype(F32)).astype(BF16)
            part = _dot(mix_h, wout_ref[cols, :])
            proj = part if proj is None else proj + part
    else:
        proj = _dot(jnp.concatenate([m_ref[...] for m_ref in mix_refs], axis=1), wout_ref[...])
    x1 = x_ref[...] + proj
    hb = _rmsnorm_rows(x1, g_ref[...]).astype(BF16)
    acc = x1
    for c in range(0, D_FF, FF_CHUNK):
        a = jnp.maximum(_dot(hb, w1_ref[:, c:c + FF_CHUNK]), 0.0)
        acc = acc + _dot((a * a).astype(BF16), w2_ref[c:c + FF_CHUNK, :])
    o_ref[...] = acc


def _outproj_mlp(x, mixes, gate_gn, wout_all, mix_layer, g, w1_all, w2_all, layer, tm=512, name="outproj_mlp"):
    s = x.shape[0]
    row = _after_cast(lambda i: (i, 0))
    gated = gate_gn is not None
    acts = tuple(mixes) + ((gate_gn[0],) if gated else ())
    in_specs = [pl.BlockSpec((tm, D_MODEL), row)]
    in_specs += [pl.BlockSpec((tm, m.shape[1]), row) for m in acts]
    in_specs += [_const_spec(gate_gn[1].shape)] if gated else []
    in_specs += [_slab_spec(wout_all, mix_layer), _const_spec(g.shape),
                 _slab_spec(w1_all, layer), _slab_spec(w2_all, layer)]
    consts = ((gate_gn[1],) if gated else ()) + (wout_all, g, w1_all, w2_all)
    return pl.pallas_call(
        functools.partial(_outproj_mlp_kernel, n_mix=len(mixes), gated=gated),
        grid=(CAST_STEPS + s // tm,),
        in_specs=in_specs,
        out_specs=pl.BlockSpec((tm, D_MODEL), row),
        out_shape=jax.ShapeDtypeStruct((s, D_MODEL), F32),
        scratch_shapes=[pltpu.VMEM(w.shape[1:], BF16) for w in (wout_all, w1_all, w2_all)],
        compiler_params=_params("arbitrary"),
        name=name,
    )(x, *acts, *consts)


O_Q, O_V, O_G = 0, 2 * D_MODEL, 2 * D_MODEL + ODD_MIX


def _inproj_odd_kernel(x_ref, g_ref, w_f32, wkt_ref, q_ref, kt_ref, v_ref, gate_ref, w_ref):
    t = pl.program_id(0)

    @pl.when(t < CAST_STEPS)
    def _():
        _cast_slab(t, w_f32, w_ref)

    @pl.when(t >= CAST_STEPS)
    def _():
        xb = _rmsnorm_rows(x_ref[...], g_ref[...]).astype(BF16)
        q_ref[...] = _dot(xb, w_ref[:, O_Q:O_Q + D_MODEL]).astype(BF16)
        kt_ref[...] = (_dot_nt(wkt_ref[...], xb) * RET_DK ** -0.5).astype(BF16)
        for off in range(0, ODD_MIX, PROJ_CHUNK):
            v_ref[:, off:off + PROJ_CHUNK] = _dot(xb, w_ref[:, O_V + off:O_V + off + PROJ_CHUNK]).astype(BF16)
        for off in range(0, ODD_MIX, PROJ_CHUNK):
            gate_ref[:, off:off + PROJ_CHUNK] = _silu(
                _dot(xb, w_ref[:, O_G + off:O_G + off + PROJ_CHUNK])).astype(BF16)


def _inproj_odd(x, g, w_all, layer, wkt, tm=1024):
    s = x.shape[0]
    row = _after_cast(lambda i: (i, 0))
    return pl.pallas_call(
        _inproj_odd_kernel,
        grid=(CAST_STEPS + s // tm,),
        in_specs=[pl.BlockSpec((tm, D_MODEL), row), _const_spec((1, D_MODEL)),
                  _slab_spec(w_all, layer), _const_spec(wkt.shape)],
        out_specs=[pl.BlockSpec((tm, D_MODEL), row), pl.BlockSpec((D_MODEL, tm), _after_cast(lambda i: (0, i))),
                   pl.BlockSpec((tm, ODD_MIX), row), pl.BlockSpec((tm, ODD_MIX), row)],
        out_shape=[jax.ShapeDtypeStruct((s, D_MODEL), BF16), jax.ShapeDtypeStruct((D_MODEL, s), BF16),
                   jax.ShapeDtypeStruct((s, ODD_MIX), BF16), jax.ShapeDtypeStruct((s, ODD_MIX), BF16)],
        scratch_shapes=[pltpu.VMEM(w_all.shape[1:], BF16)],
        compiler_params=_params("arbitrary"),
        name="inproj_odd",
    )(x, g, w_all, wkt)


RET_CHUNKS_PER_STEP = 32
RET_TOK = RET_CHUNKS_PER_STEP * RET_CHUNK
RET_STEPS = SEQ // RET_TOK
RET_NCHUNK = SEQ // RET_CHUNK


def _retention_kernel(lg_ref, q_ref, kt_ref, v_ref, o_ref, rf_all, state):
    h = pl.program_id(0)
    t = pl.program_id(1)
    second = t >= RET_STEPS
    blk = jnp.where(second, 2 * RET_STEPS - 1 - t, t)
    lgf = lg_ref[0, h]
    lgb = lg_ref[1, h]
    tok_lane = lax.broadcasted_iota(jnp.int32, (1, RET_CHUNK), 1).astype(F32)
    tok_row = lax.broadcasted_iota(jnp.int32, (RET_CHUNK, 1), 0).astype(F32)
    chunk_len = jnp.full((1, 1), RET_CHUNK, F32)

    @pl.when((t == 0) | (t == RET_STEPS))
    def _():
        state[...] = jnp.zeros_like(state)

    @pl.when(jnp.logical_not(second))
    def _():
        zeta = jnp.exp(lgf * (RET_CHUNK - 1.0 - tok_lane))
        decay = jnp.exp(lgf * chunk_len)
        r = state[...]
        for c in range(RET_CHUNKS_PER_STEP):
            tok = slice(c * RET_CHUNK, (c + 1) * RET_CHUNK)
            kz = (kt_ref[:, tok].astype(F32) * zeta).astype(BF16)
            rf_all[blk * RET_CHUNKS_PER_STEP + c] = r.astype(BF16)
            r = r * decay + _dot(kz, v_ref[tok, :])
        state[...] = r

    @pl.when(second)
    def _():
        diff = (lax.broadcasted_iota(jnp.int32, (RET_CHUNK, RET_CHUNK), 0)
                - lax.broadcasted_iota(jnp.int32, (RET_CHUNK, RET_CHUNK), 1)).astype(F32)
        dsum = jnp.where(diff >= 0, jnp.exp(lgf * jnp.maximum(diff, 0.0)), jnp.exp(lgb * jnp.maximum(-diff, 0.0)))
        cross_f = jnp.exp(lgf * (tok_row + 1.0))
        cross_b = jnp.exp(lgb * (RET_CHUNK - tok_row))
        zeta = jnp.exp(lgb * tok_lane)
        decay = jnp.exp(lgb * chunk_len)
        r = state[...]
        for c in reversed(range(RET_CHUNKS_PER_STEP)):
            tok = slice(c * RET_CHUNK, (c + 1) * RET_CHUNK)
            q = q_ref[tok, :]
            kt = kt_ref[:, tok]
            v = v_ref[tok, :]
            a = (_dot(q, kt) * dsum).astype(BF16)
            qf = q.astype(F32)
            lhs = jnp.concatenate([a, (qf * cross_f).astype(BF16), (qf * cross_b).astype(BF16)], axis=1)
            rhs = jnp.concatenate([v, rf_all[blk * RET_CHUNKS_PER_STEP + c], r.astype(BF16)], axis=0)
            o_ref[tok, :] = _dot(lhs, rhs).astype(BF16)
            r = r * decay + _dot((kt.astype(F32) * zeta).astype(BF16), v)
        state[...] = r


def _retention(lg, q, kt, v):
    s = q.shape[0]

    def blk_of(t):
        return jnp.where(t >= RET_STEPS, 2 * RET_STEPS - 1 - t, t)

    def second_blk(t):
        return jnp.where(t >= RET_STEPS, 2 * RET_STEPS - 1 - t, RET_STEPS - 1)

    return pl.pallas_call(
        _retention_kernel,
        grid=(RET_HEADS, 2 * RET_STEPS),
        in_specs=[pl.BlockSpec(memory_space=pltpu.SMEM),
                  pl.BlockSpec((RET_TOK, RET_DK), lambda h, t: (second_blk(t), h)),
                  pl.BlockSpec((RET_DK, RET_TOK), lambda h, t: (h, blk_of(t))),
                  pl.BlockSpec((RET_TOK, RET_DV), lambda h, t: (blk_of(t), h))],
        out_specs=pl.BlockSpec((RET_TOK, RET_DV), lambda h, t: (second_blk(t), h)),
        out_shape=jax.ShapeDtypeStruct((s, ODD_MIX), BF16),
        scratch_shapes=[pltpu.VMEM((RET_NCHUNK, RET_DK, RET_DV), BF16), pltpu.VMEM((RET_DK, RET_DV), F32)],
        compiler_params=_params("arbitrary", "arbitrary"),
        name="retention",
    )(lg, q, kt, v)


def _even_qk_gain(qn_a, kn_a, qn_b, kn_b):
    scale = HEAD_DIM ** -0.5 * LOG2E
    gain = jnp.concatenate([jnp.tile(qn_a, A_Q_HEADS) * scale, jnp.tile(kn_a, A_KV_HEADS),
                            jnp.ones((A_KV_HEADS * HEAD_DIM,), F32),
                            jnp.tile(qn_b, B_HEADS) * scale, jnp.tile(kn_b, B_HEADS),
                            jnp.ones((B_HEADS * HEAD_DIM,), F32)]).astype(F32)
    return gain[None, :]


def kernel(x, attn_norm_e, w_in_e, q_norm_a, k_norm_a, sink_a, q_norm_b, k_norm_b, rpb_b, w_out_e,
           ret_norm_o, w_in_o, decay_fwd_o, decay_bwd_o, ret_gn_o, w_out_o,
           mlp_norm, w_mlp_in, w_mlp_out):
    b, s, d = x.shape
    xs = x.reshape(b * s, d)
    depth = mlp_norm.shape[0]
    for layer in range(depth):
        i = layer // 2
        g_mlp = mlp_norm[layer][None, :]
        if layer % 2 == 0:
            gain = _even_qk_gain(q_norm_a[i], k_norm_a[i], q_norm_b[i], k_norm_b[i])
            qa, ka2, va2, qb, kb, vb = _inproj_even(xs, attn_norm_e[i][None, :], w_in_e, i, gain)
            oa = _window_attn(sink_a[i].astype(F32) * LOG2E, qa, ka2, va2)
            ob = _na_attn(qb, kb, vb, _na_bias_table(rpb_b[i].astype(F32) * LOG2E))
            xs = _outproj_mlp(xs, (oa, ob), None, w_out_e, i, g_mlp, w_mlp_in, w_mlp_out, layer,
                              tm=1024, name="outproj_mlp_even")
        else:
            wkt = w_in_o[i, :, D_MODEL:2 * D_MODEL].T.astype(BF16)
            q, kt, v, gate = _inproj_odd(xs, ret_norm_o[i][None, :], w_in_o, i, wkt)
            lg = jnp.stack([jax.nn.log_sigmoid(decay_fwd_o[i].astype(F32)),
                            jax.nn.log_sigmoid(decay_bwd_o[i].astype(F32))])
            y = _retention(lg, q, kt, v)
            xs = _outproj_mlp(xs, (y,), (gate, ret_gn_o[i].astype(F32)[None, :]), w_out_o, i,
                              g_mlp, w_mlp_in, w_mlp_out, layer, name="outproj_mlp_odd")
    return xs.reshape(b, s, d)
```

```python
import functools

import numpy as np
import jax
import jax.numpy as jnp
from jax import lax
from jax.experimental import pallas as pl
from jax.experimental.pallas import tpu as pltpu

D_MODEL = 1024
SEQ = 16384
HEAD_DIM = 64
A_Q_HEADS = 8
A_KV_HEADS = 2
A_GROUP = A_Q_HEADS // A_KV_HEADS
A_WINDOW = 128
A_BLOCK = 128
B_HEADS = 8
GRID_W = 64
GRID_ROWS = SEQ // GRID_W
NA_KH = 8
NA_KW = 16
RET_HEADS = 8
RET_DK = D_MODEL // RET_HEADS
RET_DV = 2 * D_MODEL // RET_HEADS
RET_CHUNK = 256
D_FF = 4 * D_MODEL
ODD_MIX = RET_HEADS * RET_DV
RMS_EPS = 1e-6
GN_EPS = 1e-5

LANES = 128
MXU_N = 256
PROJ_CHUNK = 1024
NEG = -1e30
LOG2E = 1.4426950408889634
VMEM_LIMIT = 60 * 1024 * 1024

F32 = jnp.float32
BF16 = jnp.bfloat16

E_QA, E_KA, E_VA, E_QB, E_KB, E_VB = 0, 512, 640, 768, 1280, 1792
E_COLS = 2304


def _const_spec(shape):
    return pl.BlockSpec(shape, lambda *_: (0,) * len(shape), pipeline_mode=pl.Buffered(1))


def _params(*sem):
    return pltpu.CompilerParams(dimension_semantics=sem, vmem_limit_bytes=VMEM_LIMIT)


def _rmsnorm_rows(x, g):
    return x * lax.rsqrt(jnp.mean(x * x, axis=-1, keepdims=True) + RMS_EPS) * g


def _silu(g):
    return g * (0.5 * jnp.tanh(0.5 * g) + 0.5)


def _dot(a, b):
    return jnp.dot(a, b, preferred_element_type=F32)


CAST_STEPS = 8


def _slab_spec(w, layer):
    return pl.BlockSpec((None, w.shape[1] // CAST_STEPS, w.shape[2]),
                        lambda t: (layer, jnp.minimum(t, CAST_STEPS - 1), 0))


def _cast_slab(t, src_ref, dst_ref):
    rows = src_ref.shape[0]
    dst_ref[pl.ds(pl.multiple_of(t * rows, rows), rows), :] = src_ref[...].astype(BF16)


def _after_cast(index):
    return lambda t: index(jnp.maximum(t - CAST_STEPS, 0))


def _dot_nt(a, b):
    return lax.dot_general(a, b, (((1,), (1,)), ((), ())), preferred_element_type=F32)


def _inproj_even_kernel(x_ref, g_ref, w_f32, gain_ref, qa_ref, ka2_ref, va2_ref, qb_ref, kb_ref, vb_ref, w_ref):
    t = pl.program_id(0)

    @pl.when(t < CAST_STEPS)
    def _():
        _cast_slab(t, w_f32, w_ref)

    @pl.when(t >= CAST_STEPS)
    def _():
        xb = _rmsnorm_rows(x_ref[...], g_ref[...]).astype(BF16)
        r = lax.broadcasted_iota(jnp.int32, (MXU_N, MXU_N), 0) // HEAD_DIM
        c = lax.broadcasted_iota(jnp.int32, (MXU_N, MXU_N), 1) // HEAD_DIM
        head_mean = jnp.where(r == c, 1.0 / HEAD_DIM, 0.0).astype(BF16)
        is_ka = lax.broadcasted_iota(jnp.int32, (1, MXU_N), 1) < E_VA - E_KA
        low_half = lax.broadcasted_iota(jnp.int32, (1, LANES), 1) < HEAD_DIM
        outs = ((qa_ref, E_QA, 512, "all"), (None, E_KA, 256, "ka"), (qb_ref, E_QB, 512, "all"),
                (kb_ref, E_KB, 512, "all"), (vb_ref, E_VB, 512, "none"))
        chunks = [(ref, start + off, off, normed) for ref, start, width, normed in outs
                  for off in range(0, width, MXU_N)]
        ys = [_dot(xb, w_ref[:, col:col + MXU_N]) for _, col, _, _ in chunks]
        ms = [_dot((y * y).astype(BF16), head_mean) if normed != "none" else None
              for y, (_, _, _, normed) in zip(ys, chunks)]
        for y, m, (ref, col, off, normed) in zip(ys, ms, chunks):
            if normed != "none":
                scale = lax.rsqrt(m + RMS_EPS) * gain_ref[:, col:col + MXU_N]
                y = y * (scale if normed == "all" else jnp.where(is_ka, scale, 1.0))
            if ref is not None:
                ref[:, off:off + MXU_N] = y.astype(BF16)
                continue
            for pair, dup_ref in ((y[:, :LANES], ka2_ref), (y[:, LANES:], va2_ref)):
                swapped = pltpu.roll(pair, HEAD_DIM, axis=1)
                dup_ref[:, :LANES] = jnp.where(low_half, pair, swapped).astype(BF16)
                dup_ref[:, LANES:] = jnp.where(low_half, swapped, pair).astype(BF16)


def _inproj_even(x, g, w_all, layer, gain, tm=1024):
    s = x.shape[0]
    row = _after_cast(lambda i: (i, 0))
    widths = (512, 256, 256, 512, 512, 512)
    return pl.pallas_call(
        _inproj_even_kernel,
        grid=(CAST_STEPS + s // tm,),
        in_specs=[pl.BlockSpec((tm, D_MODEL), row), _const_spec((1, D_MODEL)),
                  _slab_spec(w_all, layer), _const_spec((1, E_COLS))],
        out_specs=[pl.BlockSpec((tm, wd), row) for wd in widths],
        out_shape=[jax.ShapeDtypeStruct((s, wd), BF16) for wd in widths],
        scratch_shapes=[pltpu.VMEM(w_all.shape[1:], BF16)],
        compiler_params=_params("arbitrary"),
        name="inproj_even",
    )(x, g, w_all, gain)


def _window_kernel(sink_ref, q_ref, kp_ref, kc_ref, kn_ref, vp_ref, vc_ref, vn_ref, o_ref, *, tq):
    i = pl.program_id(0)
    lane = lax.broadcasted_iota(jnp.int32, (1, LANES), 1)
    low_half = lane < HEAD_DIM
    qi = lax.broadcasted_iota(jnp.int32, (A_BLOCK, 3 * A_BLOCK), 0)
    kj = lax.broadcasted_iota(jnp.int32, (A_BLOCK, 3 * A_BLOCK), 1)
    absrel = jnp.abs(kj - A_BLOCK - qi)
    in_window = absrel <= A_WINDOW
    absrel_f = absrel.astype(F32)
    kcol = lax.broadcasted_iota(jnp.int32, (1, 3 * A_BLOCK), 1)
    edge_first = jnp.where((i == 0) & (kcol < A_BLOCK), NEG, 0.0)
    edge_last = jnp.where((i == pl.num_programs(0) - 1) & (kcol >= 2 * A_BLOCK), NEG, 0.0)
    nblk = tq // A_BLOCK
    penalty = []
    for h in range(A_Q_HEADS):
        slope = LOG2E * 2.0 ** (-(8.0 / A_Q_HEADS) * (h + 1.0))
        base = jnp.where(in_window, -slope * absrel_f, NEG)
        penalty.append({0: base + edge_first, nblk - 1: base + edge_last, None: base})
    units = [(j, b) for j in range(A_KV_HEADS) for b in range(nblk)]
    kcat, vcat = [], []
    for j in range(A_KV_HEADS):
        ks = slice(LANES * j, LANES * (j + 1))
        kcat.append(jnp.concatenate([kp_ref[tq - A_BLOCK:, ks], kc_ref[:, ks], kn_ref[:A_BLOCK, ks]], axis=0))
        vcat.append(jnp.concatenate([vp_ref[tq - A_BLOCK:, ks], vc_ref[:, ks], vn_ref[:A_BLOCK, ks]], axis=0))
    def scores_of(j, b):
        rows = slice(A_BLOCK * b, A_BLOCK * (b + 1))
        lhs = []
        for e in range(A_GROUP):
            pair = q_ref[rows, LANES * (2 * j + e // 2):LANES * (2 * j + e // 2 + 1)]
            lhs.append(jnp.where(low_half if e % 2 == 0 else ~low_half, pair, jnp.zeros_like(pair)))
        return _dot_nt(jnp.concatenate(lhs, axis=0), kcat[j][A_BLOCK * b:A_BLOCK * (b + 3)])

    def softmax_of(j, b, s_all):
        p_unit, inv_unit = [], []
        for e in range(A_GROUP):
            h = A_GROUP * j + e
            sk = sink_ref[h]
            pen = penalty[h]
            s = s_all[A_BLOCK * e:A_BLOCK * (e + 1)] + pen.get(b, pen[None])
            m = jnp.maximum(jnp.max(s, axis=-1, keepdims=True), sk)
            p = jnp.exp2(s - m)
            inv_unit.append(1.0 / (jnp.sum(p, axis=-1, keepdims=True) + jnp.exp2(sk - m)))
            p_unit.append(p.astype(BF16))
        return jnp.concatenate(p_unit, axis=0), jnp.concatenate(inv_unit, axis=0)

    def emit(j, b, p_all, inv):
        rows = slice(A_BLOCK * b, A_BLOCK * (b + 1))
        o = _dot(p_all, vcat[j][A_BLOCK * b:A_BLOCK * (b + 3)]) * inv
        for pr in range(A_GROUP // 2):
            lo = o[A_BLOCK * 2 * pr:A_BLOCK * (2 * pr + 1)]
            hi = o[A_BLOCK * (2 * pr + 1):A_BLOCK * (2 * pr + 2)]
            col = LANES * (2 * j + pr)
            o_ref[rows, col:col + LANES] = jnp.where(low_half, lo, hi).astype(BF16)

    scores = [scores_of(j, b) for j, b in units]
    probs = [softmax_of(j, b, s_all) for (j, b), s_all in zip(units, scores)]
    for (j, b), (p_all, inv) in zip(units, probs):
        emit(j, b, p_all, inv)


def _window_attn(sink, qa, ka2, va2, tq=512):
    s = qa.shape[0]
    nb = s // tq
    prev = lambda i: (jnp.maximum(i - 1, 0), 0)
    cur = lambda i: (i, 0)
    nxt = lambda i: (jnp.minimum(i + 1, nb - 1), 0)
    kv = lambda f: pl.BlockSpec((tq, 2 * LANES), f)
    return pl.pallas_call(
        functools.partial(_window_kernel, tq=tq),
        grid=(nb,),
        in_specs=[pl.BlockSpec(memory_space=pltpu.SMEM),
                  pl.BlockSpec((tq, 512), cur), kv(prev), kv(cur), kv(nxt), kv(prev), kv(cur), kv(nxt)],
        out_specs=pl.BlockSpec((tq, 512), cur),
        out_shape=jax.ShapeDtypeStruct((s, 512), BF16),
        compiler_params=_params("parallel"),
        name="window_gqa",
    )(sink, qa, ka2, ka2, ka2, va2, va2, va2)


NA_ROWS_PER_STEP = 8
NA_TOK = NA_ROWS_PER_STEP * GRID_W
NA_KEYS = NA_KH * GRID_W


def _na_kernel(q_ref, kp_ref, kc_ref, kn_ref, vp_ref, vc_ref, vn_ref, bias_ref, o_ref, kcat, vcat):
    i = pl.program_id(0)
    kcat[0:NA_TOK] = kp_ref[...]
    kcat[NA_TOK:2 * NA_TOK] = kc_ref[...]
    kcat[2 * NA_TOK:3 * NA_TOK] = kn_ref[...]
    vcat[0:NA_TOK] = vp_ref[...]
    vcat[NA_TOK:2 * NA_TOK] = vc_ref[...]
    vcat[2 * NA_TOK:3 * NA_TOK] = vn_ref[...]
    lane = lax.broadcasted_iota(jnp.int32, (1, LANES), 1)
    low_half = lane < HEAD_DIM

    def row_body(rr, carry):
        r = i * NA_ROWS_PER_STEP + rr
        rstart = jnp.clip(r - NA_KH // 2, 0, GRID_ROWS - NA_KH)
        start = pl.multiple_of((rstart - (i - 1) * NA_ROWS_PER_STEP) * GRID_W, GRID_W)
        ri0 = rstart - r + NA_KH - 1
        qrows = pl.ds(pl.multiple_of(rr * GRID_W, GRID_W), GRID_W)
        npair = B_HEADS // 2
        scores = []
        for p in range(npair):
            cols = slice(LANES * p, LANES * (p + 1))
            qp = q_ref[qrows, cols]
            lhs = jnp.concatenate([jnp.where(low_half, qp, jnp.zeros_like(qp)),
                                   jnp.where(low_half, jnp.zeros_like(qp), qp)], axis=0)
            scores.append(_dot_nt(lhs, kcat[pl.ds(start, NA_KEYS), cols]))
        probs, invs = [], []
        for p in range(npair):
            bias = jnp.concatenate(
                [jnp.concatenate([bias_ref[2 * p + e, ri0 + 2 * m] for m in range(NA_KH // 2)], axis=1)
                 for e in range(2)], axis=0)
            s = scores[p] + bias
            pexp = jnp.exp2(s - jnp.max(s, axis=-1, keepdims=True))
            invs.append(1.0 / jnp.sum(pexp, axis=-1, keepdims=True))
            probs.append(pexp.astype(BF16))
        for p in range(npair):
            cols = slice(LANES * p, LANES * (p + 1))
            o = _dot(probs[p], vcat[pl.ds(start, NA_KEYS), cols]) * invs[p]
            o_ref[qrows, cols] = jnp.where(low_half, o[:GRID_W], o[GRID_W:]).astype(BF16)
        return carry

    lax.fori_loop(0, NA_ROWS_PER_STEP, row_body, 0, unroll=True)


def _na_bias_table(rpb):
    qc = np.arange(GRID_W)[:, None]
    kc = np.arange(GRID_W)[None, :]
    cstart = np.clip(qc - NA_KW // 2, 0, GRID_W - NA_KW)
    col_valid = (kc >= cstart) & (kc < cstart + NA_KW)
    col_idx = np.clip(kc - qc + NA_KW - 1, 0, 2 * NA_KW - 2)
    onehot = (col_idx.reshape(1, -1) == np.arange(2 * NA_KW - 1)[:, None]).astype(np.float32)
    n_ri = 2 * NA_KH - 1
    t = jnp.dot(rpb.astype(F32).reshape(B_HEADS * n_ri, 2 * NA_KW - 1), jnp.asarray(onehot),
                precision=lax.Precision.HIGHEST).reshape(B_HEADS, n_ri, GRID_W, GRID_W)
    t = jnp.where(col_valid[None, None], t, NEG)
    return jnp.concatenate([t[:, :-1], t[:, 1:]], axis=-1)


def _na_attn(qb, kb, vb, bias):
    s = qb.shape[0]
    nb = s // NA_TOK
    prev = lambda i: (jnp.maximum(i - 1, 0), 0)
    cur = lambda i: (i, 0)
    nxt = lambda i: (jnp.minimum(i + 1, nb - 1), 0)
    blk = lambda f: pl.BlockSpec((NA_TOK, 512), f)
    return pl.pallas_call(
        _na_kernel,
        grid=(nb,),
        in_specs=[blk(cur), blk(prev), blk(cur), blk(nxt), blk(prev), blk(cur), blk(nxt),
                  _const_spec(bias.shape)],
        out_specs=blk(cur),
        out_shape=jax.ShapeDtypeStruct((s, 512), BF16),
        scratch_shapes=[pltpu.VMEM((3 * NA_TOK, 512), BF16), pltpu.VMEM((3 * NA_TOK, 512), BF16)],
        compiler_params=_params("parallel"),
        name="neighbourhood_attn",
    )(qb, kb, kb, kb, vb, vb, vb, bias)


FF_CHUNK = 1024


def _outproj_mlp_kernel(*refs, n_mix, gated):
    t = pl.program_id(0)
    wout_f32, g_ref, w1_f32, w2_f32, o_ref, wout_ref, w1_ref, w2_ref = refs[-8:]

    @pl.when(t < CAST_STEPS)
    def _():
        for src, dst in ((wout_f32, wout_ref), (w1_f32, w1_ref), (w2_f32, w2_ref)):
            _cast_slab(t, src, dst)

    @pl.when(t >= CAST_STEPS)
    def _():
        _outproj_mlp_tile(refs[:-8], wout_ref, g_ref, w1_ref, w2_ref, o_ref, n_mix=n_mix, gated=gated)


def _outproj_mlp_tile(act_refs, wout_ref, g_ref, w1_ref, w2_ref, o_ref, *, n_mix, gated):
    x_ref = act_refs[0]
    mix_refs = act_refs[1:1 + n_mix]
    if gated:
        (y_ref,), gate_ref, gn_ref = mix_refs, act_refs[1 + n_mix], act_refs[2 + n_mix]
        proj = None
        for h in range(RET_HEADS):
            cols = slice(h * RET_DV, (h + 1) * RET_DV)
            y = y_ref[:, cols].astype(F32)
            d = y - jnp.mean(y, axis=-1, keepdims=True)
            var = jnp.mean(d * d, axis=-1, keepdims=True)
            yn = d * lax.rsqrt(var + GN_EPS) * gn_ref[:, cols]
            mix_h = (yn * gate_ref[:, cols].astype(F32)).astype(BF16)
            part = _dot(mix_h, wout_ref[cols, :])
            proj = part if proj is None else proj + part
    else:
        proj = _dot(jnp.concatenate([m_ref[...] for m_ref in mix_refs], axis=1), wout_ref[...])
    x1 = x_ref[...] + proj
    hb = _rmsnorm_rows(x1, g_ref[...]).astype(BF16)
    acc = x1
    for c in range(0, D_FF, FF_CHUNK):
        a = jnp.maximum(_dot(hb, w1_ref[:, c:c + FF_CHUNK]), 0.0)
        acc = acc + _dot((a * a).astype(BF16), w2_ref[c:c + FF_CHUNK, :])
    o_ref[...] = acc


def _outproj_mlp(x, mixes, gate_gn, wout_all, mix_layer, g, w1_all, w2_all, layer, tm=512, name="outproj_mlp"):
    s = x.shape[0]
    row = _after_cast(lambda i: (i, 0))
    gated = gate_gn is not None
    acts = tuple(mixes) + ((gate_gn[0],) if gated else ())
    in_specs = [pl.BlockSpec((tm, D_MODEL), row)]
    in_specs += [pl.BlockSpec((tm, m.shape[1]), row) for m in acts]
    in_specs += [_const_spec(gate_gn[1].shape)] if gated else []
    in_specs += [_slab_spec(wout_all, mix_layer), _const_spec(g.shape),
                 _slab_spec(w1_all, layer), _slab_spec(w2_all, layer)]
    consts = ((gate_gn[1],) if gated else ()) + (wout_all, g, w1_all, w2_all)
    return pl.pallas_call(
        functools.partial(_outproj_mlp_kernel, n_mix=len(mixes), gated=gated),
        grid=(CAST_STEPS + s // tm,),
        in_specs=in_specs,
        out_specs=pl.BlockSpec((tm, D_MODEL), row),
        out_shape=jax.ShapeDtypeStruct((s, D_MODEL), F32),
        scratch_shapes=[pltpu.VMEM(w.shape[1:], BF16) for w in (wout_all, w1_all, w2_all)],
        compiler_params=_params("arbitrary"),
        name=name,
    )(x, *acts, *consts)


O_Q, O_V, O_G = 0, 2 * D_MODEL, 2 * D_MODEL + ODD_MIX


def _inproj_odd_kernel(x_ref, g_ref, w_f32, wkt_ref, q_ref, kt_ref, v_ref, gate_ref, w_ref):
    t = pl.program_id(0)

    @pl.when(t < CAST_STEPS)
    def _():
        _cast_slab(t, w_f32, w_ref)

    @pl.when(t >= CAST_STEPS)
    def _():
        xb = _rmsnorm_rows(x_ref[...], g_ref[...]).astype(BF16)
        for off in range(0, ODD_MIX, PROJ_CHUNK):
            gate_ref[:, off:off + PROJ_CHUNK] = _silu(
                _dot(xb, w_ref[:, O_G + off:O_G + off + PROJ_CHUNK])).astype(BF16)
        for off in range(0, ODD_MIX, PROJ_CHUNK):
            v_ref[:, off:off + PROJ_CHUNK] = _dot(xb, w_ref[:, O_V + off:O_V + off + PROJ_CHUNK]).astype(BF16)
        q_ref[...] = _dot(xb, w_ref[:, O_Q:O_Q + D_MODEL]).astype(BF16)
        kt_ref[...] = (_dot_nt(wkt_ref[...], xb) * RET_DK ** -0.5).astype(BF16)


def _inproj_odd(x, g, w_all, layer, wkt, tm=1024):
    s = x.shape[0]
    row = _after_cast(lambda i: (i, 0))
    return pl.pallas_call(
        _inproj_odd_kernel,
        grid=(CAST_STEPS + s // tm,),
        in_specs=[pl.BlockSpec((tm, D_MODEL), row), _const_spec((1, D_MODEL)),
                  _slab_spec(w_all, layer), _const_spec(wkt.shape)],
        out_specs=[pl.BlockSpec((tm, D_MODEL), row), pl.BlockSpec((D_MODEL, tm), _after_cast(lambda i: (0, i))),
                   pl.BlockSpec((tm, ODD_MIX), row), pl.BlockSpec((tm, ODD_MIX), row)],
        out_shape=[jax.ShapeDtypeStruct((s, D_MODEL), BF16), jax.ShapeDtypeStruct((D_MODEL, s), BF16),
                   jax.ShapeDtypeStruct((s, ODD_MIX), BF16), jax.ShapeDtypeStruct((s, ODD_MIX), BF16)],
        scratch_shapes=[pltpu.VMEM(w_all.shape[1:], BF16)],
        compiler_params=_params("arbitrary"),
        name="inproj_odd",
    )(x, g, w_all, wkt)


RET_CHUNKS_PER_STEP = 32
RET_TOK = RET_CHUNKS_PER_STEP * RET_CHUNK
RET_STEPS = SEQ // RET_TOK
RET_NCHUNK = SEQ // RET_CHUNK


def _retention_kernel(lg_ref, q_ref, kt_ref, v_ref, o_ref, rf_all, state):
    h = pl.program_id(0)
    t = pl.program_id(1)
    second = t >= RET_STEPS
    blk = jnp.where(second, 2 * RET_STEPS - 1 - t, t)
    lgf = lg_ref[0, h]
    lgb = lg_ref[1, h]
    tok_lane = lax.broadcasted_iota(jnp.int32, (1, RET_CHUNK), 1).astype(F32)
    tok_row = lax.broadcasted_iota(jnp.int32, (RET_CHUNK, 1), 0).astype(F32)
    chunk_len = jnp.full((1, 1), RET_CHUNK, F32)

    @pl.when((t == 0) | (t == RET_STEPS))
    def _():
        state[...] = jnp.zeros_like(state)

    @pl.when(jnp.logical_not(second))
    def _():
        zeta = jnp.exp(lgf * (RET_CHUNK - 1.0 - tok_lane))
        decay = jnp.exp(lgf * chunk_len)
        r = state[...]
        for c in range(RET_CHUNKS_PER_STEP):
            tok = slice(c * RET_CHUNK, (c + 1) * RET_CHUNK)
            kz = (kt_ref[:, tok].astype(F32) * zeta).astype(BF16)
            rf_all[blk * RET_CHUNKS_PER_STEP + c] = r.astype(BF16)
            r = r * decay + _dot(kz, v_ref[tok, :])
        state[...] = r

    @pl.when(second)
    def _():
        diff = (lax.broadcasted_iota(jnp.int32, (RET_CHUNK, RET_CHUNK), 0)
                - lax.broadcasted_iota(jnp.int32, (RET_CHUNK, RET_CHUNK), 1)).astype(F32)
        dsum = jnp.where(diff >= 0, jnp.exp(lgf * jnp.maximum(diff, 0.0)), jnp.exp(lgb * jnp.maximum(-diff, 0.0)))
        cross_f = jnp.exp(lgf * (tok_row + 1.0))
        cross_b = jnp.exp(lgb * (RET_CHUNK - tok_row))
        zeta = jnp.exp(lgb * tok_lane)
        decay = jnp.exp(lgb * chunk_len)
        r = state[...]
        for c in reversed(range(RET_CHUNKS_PER_STEP)):
            tok = slice(c * RET_CHUNK, (c + 1) * RET_CHUNK)
            q = q_ref[tok, :]
            kt = kt_ref[:, tok]
            v = v_ref[tok, :]
            a = (_dot(q, kt) * dsum).astype(BF16)
            qf = q.astype(F32)
            lhs = jnp.concatenate([a, (qf * cross_f).astype(BF16), (qf * cross_b).astype(BF16)], axis=1)
            rhs = jnp.concatenate([v, rf_all[blk * RET_CHUNKS_PER_STEP + c], r.astype(BF16)], axis=0)
            o_ref[tok, :] = _dot(lhs, rhs).astype(BF16)
            r = r * decay + _dot((kt.astype(F32) * zeta).astype(BF16), v)
        state[...] = r


def _retention(lg, q, kt, v):
    s = q.shape[0]

    def blk_of(t):
        return jnp.where(t >= RET_STEPS, 2 * RET_STEPS - 1 - t, t)

    def second_blk(t):
        return jnp.where(t >= RET_STEPS, 2 * RET_STEPS - 1 - t, RET_STEPS - 1)

    return pl.pallas_call(
        _retention_kernel,
        grid=(RET_HEADS, 2 * RET_STEPS),
        in_specs=[pl.BlockSpec(memory_space=pltpu.SMEM),
                  pl.BlockSpec((RET_TOK, RET_DK), lambda h, t: (second_blk(t), h)),
                  pl.BlockSpec((RET_DK, RET_TOK), lambda h, t: (h, blk_of(t))),
                  pl.BlockSpec((RET_TOK, RET_DV), lambda h, t: (blk_of(t), h))],
        out_specs=pl.BlockSpec((RET_TOK, RET_DV), lambda h, t: (second_blk(t), h)),
        out_shape=jax.ShapeDtypeStruct((s, ODD_MIX), BF16),
        scratch_shapes=[pltpu.VMEM((RET_NCHUNK, RET_DK, RET_DV), BF16), pltpu.VMEM((RET_DK, RET_DV), F32)],
        compiler_params=_params("arbitrary", "arbitrary"),
        name="retention",
    )(lg, q, kt, v)


def _even_qk_gain(qn_a, kn_a, qn_b, kn_b):
    scale = HEAD_DIM ** -0.5 * LOG2E
    gain = jnp.concatenate([jnp.tile(qn_a, A_Q_HEADS) * scale, jnp.tile(kn_a, A_KV_HEADS),
                            jnp.ones((A_KV_HEADS * HEAD_DIM,), F32),
                            jnp.tile(qn_b, B_HEADS) * scale, jnp.tile(kn_b, B_HEADS),
                            jnp.ones((B_HEADS * HEAD_DIM,), F32)]).astype(F32)
    return gain[None, :]


def kernel(x, attn_norm_e, w_in_e, q_norm_a, k_norm_a, sink_a, q_norm_b, k_norm_b, rpb_b, w_out_e,
           ret_norm_o, w_in_o, decay_fwd_o, decay_bwd_o, ret_gn_o, w_out_o,
           mlp_norm, w_mlp_in, w_mlp_out):
    b, s, d = x.shape
    xs = x.reshape(b * s, d)
    depth = mlp_norm.shape[0]
    for layer in range(depth):
        i = layer // 2
        g_mlp = mlp_norm[layer][None, :]
        if layer % 2 == 0:
            gain = _even_qk_gain(q_norm_a[i], k_norm_a[i], q_norm_b[i], k_norm_b[i])
            qa, ka2, va2, qb, kb, vb = _inproj_even(xs, attn_norm_e[i][None, :], w_in_e, i, gain)
            oa = _window_attn(sink_a[i].astype(F32) * LOG2E, qa, ka2, va2)
            ob = _na_attn(qb, kb, vb, _na_bias_table(rpb_b[i].astype(F32) * LOG2E))
            xs = _outproj_mlp(xs, (oa, ob), None, w_out_e, i, g_mlp, w_mlp_in, w_mlp_out, layer,
                              tm=1024, name="outproj_mlp_even")
        else:
            wkt = w_in_o[i, :, D_MODEL:2 * D_MODEL].T.astype(BF16)
            q, kt, v, gate = _inproj_odd(xs, ret_norm_o[i][None, :], w_in_o, i, wkt)
            lg = jnp.stack([jax.nn.log_sigmoid(decay_fwd_o[i].astype(F32)),
                            jax.nn.log_sigmoid(decay_bwd_o[i].astype(F32))])
            y = _retention(lg, q, kt, v)
            xs = _outproj_mlp(xs, (y,), (gate, ret_gn_o[i].astype(F32)[None, :]), w_out_o, i,
                              g_mlp, w_mlp_in, w_mlp_out, layer, name="outproj_mlp_odd")
    return xs.reshape(b, s, d)
```

```python
import functools

import numpy as np
import jax
import jax.numpy as jnp
from jax import lax
from jax.experimental import pallas as pl
from jax.experimental.pallas import tpu as pltpu

D_MODEL = 1024
SEQ = 16384
HEAD_DIM = 64
A_Q_HEADS = 8
A_KV_HEADS = 2
A_GROUP = A_Q_HEADS // A_KV_HEADS
A_WINDOW = 128
A_BLOCK = 128
B_HEADS = 8
GRID_W = 64
GRID_ROWS = SEQ // GRID_W
NA_KH = 8
NA_KW = 16
RET_HEADS = 8
RET_DK = D_MODEL // RET_HEADS
RET_DV = 2 * D_MODEL // RET_HEADS
RET_CHUNK = 256
D_FF = 4 * D_MODEL
ODD_MIX = RET_HEADS * RET_DV
RMS_EPS = 1e-6
GN_EPS = 1e-5

LANES = 128
MXU_N = 256
PROJ_CHUNK = 1024
NEG = -1e30
LOG2E = 1.4426950408889634
VMEM_LIMIT = 60 * 1024 * 1024

F32 = jnp.float32
BF16 = jnp.bfloat16

E_QA, E_KA, E_VA, E_QB, E_KB, E_VB = 0, 512, 640, 768, 1280, 1792
E_COLS = 2304


def _const_spec(shape):
    return pl.BlockSpec(shape, lambda *_: (0,) * len(shape), pipeline_mode=pl.Buffered(1))


def _params(*sem):
    return pltpu.CompilerParams(dimension_semantics=sem, vmem_limit_bytes=VMEM_LIMIT)


def _rmsnorm_rows(x, g):
    return x * lax.rsqrt(jnp.mean(x * x, axis=-1, keepdims=True) + RMS_EPS) * g


def _silu(g):
    return g * (0.5 * jnp.tanh(0.5 * g) + 0.5)


def _dot(a, b):
    return jnp.dot(a, b, preferred_element_type=F32)


CAST_STEPS = 8


def _slab_spec(w, layer):
    return pl.BlockSpec((None, w.shape[1] // CAST_STEPS, w.shape[2]),
                        lambda t: (layer, jnp.minimum(t, CAST_STEPS - 1), 0))


def _cast_slab(t, src_ref, dst_ref):
    rows = src_ref.shape[0]
    dst_ref[pl.ds(pl.multiple_of(t * rows, rows), rows), :] = src_ref[...].astype(BF16)


def _after_cast(index):
    return lambda t: index(jnp.maximum(t - CAST_STEPS, 0))


def _dot_nt(a, b):
    return lax.dot_general(a, b, (((1,), (1,)), ((), ())), preferred_element_type=F32)


def _inproj_even_kernel(x_ref, g_ref, w_f32, gain_ref, qa_ref, ka2_ref, va2_ref, qb_ref, kb_ref, vb_ref, w_ref):
    t = pl.program_id(0)

    @pl.when(t < CAST_STEPS)
    def _():
        _cast_slab(t, w_f32, w_ref)

    @pl.when(t >= CAST_STEPS)
    def _():
        xb = _rmsnorm_rows(x_ref[...], g_ref[...]).astype(BF16)
        r = lax.broadcasted_iota(jnp.int32, (MXU_N, MXU_N), 0) // HEAD_DIM
        c = lax.broadcasted_iota(jnp.int32, (MXU_N, MXU_N), 1) // HEAD_DIM
        head_mean = jnp.where(r == c, 1.0 / HEAD_DIM, 0.0).astype(BF16)
        is_ka = lax.broadcasted_iota(jnp.int32, (1, MXU_N), 1) < E_VA - E_KA
        low_half = lax.broadcasted_iota(jnp.int32, (1, LANES), 1) < HEAD_DIM
        outs = ((qa_ref, E_QA, 512, "all"), (None, E_KA, 256, "ka"), (qb_ref, E_QB, 512, "all"),
                (kb_ref, E_KB, 512, "all"), (vb_ref, E_VB, 512, "none"))
        chunks = [(ref, start + off, off, normed) for ref, start, width, normed in outs
                  for off in range(0, width, MXU_N)]
        ys = [_dot(xb, w_ref[:, col:col + MXU_N]) for _, col, _, _ in chunks]
        ms = [_dot((y * y).astype(BF16), head_mean) if normed != "none" else None
              for y, (_, _, _, normed) in zip(ys, chunks)]
        for y, m, (ref, col, off, normed) in zip(ys, ms, chunks):
            if normed != "none":
                scale = lax.rsqrt(m + RMS_EPS) * gain_ref[:, col:col + MXU_N]
                y = y * (scale if normed == "all" else jnp.where(is_ka, scale, 1.0))
            if ref is not None:
                ref[:, off:off + MXU_N] = y.astype(BF16)
                continue
            for pair, dup_ref in ((y[:, :LANES], ka2_ref), (y[:, LANES:], va2_ref)):
                swapped = pltpu.roll(pair, HEAD_DIM, axis=1)
                dup_ref[:, :LANES] = jnp.where(low_half, pair, swapped).astype(BF16)
                dup_ref[:, LANES:] = jnp.where(low_half, swapped, pair).astype(BF16)


def _inproj_even(x, g, w_all, layer, gain, tm=1024):
    s = x.shape[0]
    row = _after_cast(lambda i: (i, 0))
    widths = (512, 256, 256, 512, 512, 512)
    return pl.pallas_call(
        _inproj_even_kernel,
        grid=(CAST_STEPS + s // tm,),
        in_specs=[pl.BlockSpec((tm, D_MODEL), row), _const_spec((1, D_MODEL)),
                  _slab_spec(w_all, layer), _const_spec((1, E_COLS))],
        out_specs=[pl.BlockSpec((tm, wd), row) for wd in widths],
        out_shape=[jax.ShapeDtypeStruct((s, wd), BF16) for wd in widths],
        scratch_shapes=[pltpu.VMEM(w_all.shape[1:], BF16)],
        compiler_params=_params("arbitrary"),
        name="inproj_even",
    )(x, g, w_all, gain)


def _window_kernel(sink_ref, q_ref, kp_ref, kc_ref, kn_ref, vp_ref, vc_ref, vn_ref, o_ref, *, tq):
    i = pl.program_id(0)
    lane = lax.broadcasted_iota(jnp.int32, (1, LANES), 1)
    low_half = lane < HEAD_DIM
    qi = lax.broadcasted_iota(jnp.int32, (A_BLOCK, 3 * A_BLOCK), 0)
    kj = lax.broadcasted_iota(jnp.int32, (A_BLOCK, 3 * A_BLOCK), 1)
    absrel = jnp.abs(kj - A_BLOCK - qi)
    in_window = absrel <= A_WINDOW
    absrel_f = absrel.astype(F32)
    kcol = lax.broadcasted_iota(jnp.int32, (1, 3 * A_BLOCK), 1)
    edge_first = jnp.where((i == 0) & (kcol < A_BLOCK), NEG, 0.0)
    edge_last = jnp.where((i == pl.num_programs(0) - 1) & (kcol >= 2 * A_BLOCK), NEG, 0.0)
    nblk = tq // A_BLOCK
    penalty = []
    for h in range(A_Q_HEADS):
        slope = LOG2E * 2.0 ** (-(8.0 / A_Q_HEADS) * (h + 1.0))
        base = jnp.where(in_window, -slope * absrel_f, NEG)
        penalty.append({0: base + edge_first, nblk - 1: base + edge_last, None: base})
    units = [(j, b) for j in range(A_KV_HEADS) for b in range(nblk)]
    kcat, vcat = [], []
    for j in range(A_KV_HEADS):
        ks = slice(LANES * j, LANES * (j + 1))
        kcat.append(jnp.concatenate([kp_ref[tq - A_BLOCK:, ks], kc_ref[:, ks], kn_ref[:A_BLOCK, ks]], axis=0))
        vcat.append(jnp.concatenate([vp_ref[tq - A_BLOCK:, ks], vc_ref[:, ks], vn_ref[:A_BLOCK, ks]], axis=0))
    def scores_of(j, b):
        rows = slice(A_BLOCK * b, A_BLOCK * (b + 1))
        lhs = []
        for e in range(A_GROUP):
            pair = q_ref[rows, LANES * (2 * j + e // 2):LANES * (2 * j + e // 2 + 1)]
            lhs.append(jnp.where(low_half if e % 2 == 0 else ~low_half, pair, jnp.zeros_like(pair)))
        return _dot_nt(jnp.concatenate(lhs, axis=0), kcat[j][A_BLOCK * b:A_BLOCK * (b + 3)])

    def softmax_of(j, b, s_all):
        p_unit, inv_unit = [], []
        for e in range(A_GROUP):
            h = A_GROUP * j + e
            sk = sink_ref[h]
            pen = penalty[h]
            s = s_all[A_BLOCK * e:A_BLOCK * (e + 1)] + pen.get(b, pen[None])
            m = jnp.maximum(jnp.max(s, axis=-1, keepdims=True), sk)
            p = jnp.exp2(s - m)
            inv_unit.append(1.0 / (jnp.sum(p, axis=-1, keepdims=True) + jnp.exp2(sk - m)))
            p_unit.append(p.astype(BF16))
        return jnp.concatenate(p_unit, axis=0), jnp.concatenate(inv_unit, axis=0)

    def emit(j, b, p_all, inv):
        rows = slice(A_BLOCK * b, A_BLOCK * (b + 1))
        o = _dot(p_all, vcat[j][A_BLOCK * b:A_BLOCK * (b + 3)]) * inv
        for pr in range(A_GROUP // 2):
            lo = o[A_BLOCK * 2 * pr:A_BLOCK * (2 * pr + 1)]
            hi = o[A_BLOCK * (2 * pr + 1):A_BLOCK * (2 * pr + 2)]
            col = LANES * (2 * j + pr)
            o_ref[rows, col:col + LANES] = jnp.where(low_half, lo, hi).astype(BF16)

    scores = [scores_of(j, b) for j, b in units]
    probs = [softmax_of(j, b, s_all) for (j, b), s_all in zip(units, scores)]
    for (j, b), (p_all, inv) in zip(units, probs):
        emit(j, b, p_all, inv)


def _window_attn(sink, qa, ka2, va2, tq=512):
    s = qa.shape[0]
    nb = s // tq
    prev = lambda i: (jnp.maximum(i - 1, 0), 0)
    cur = lambda i: (i, 0)
    nxt = lambda i: (jnp.minimum(i + 1, nb - 1), 0)
    kv = lambda f: pl.BlockSpec((tq, 2 * LANES), f)
    return pl.pallas_call(
        functools.partial(_window_kernel, tq=tq),
        grid=(nb,),
        in_specs=[pl.BlockSpec(memory_space=pltpu.SMEM),
                  pl.BlockSpec((tq, 512), cur), kv(prev), kv(cur), kv(nxt), kv(prev), kv(cur), kv(nxt)],
        out_specs=pl.BlockSpec((tq, 512), cur),
        out_shape=jax.ShapeDtypeStruct((s, 512), BF16),
        compiler_params=_params("parallel"),
        name="window_gqa",
    )(sink, qa, ka2, ka2, ka2, va2, va2, va2)


NA_ROWS_PER_STEP = 8
NA_TOK = NA_ROWS_PER_STEP * GRID_W
NA_KEYS = NA_KH * GRID_W
NA_ROWS_PER_GROUP = 2
NA_HALO_LO = (NA_KH // 2) * GRID_W
NA_HALO_HI = (NA_KH - 1 - NA_KH // 2) * GRID_W
NA_CAT = NA_HALO_LO + NA_TOK + NA_HALO_HI


def _na_kernel(q_ref, kp_ref, kc_ref, kn_ref, vp_ref, vc_ref, vn_ref, bias_ref, o_ref, kcat, vcat):
    i = pl.program_id(0)
    for cat, p_ref, c_ref, n_ref in ((kcat, kp_ref, kc_ref, kn_ref), (vcat, vp_ref, vc_ref, vn_ref)):
        cat[0:NA_HALO_LO] = p_ref[NA_TOK - NA_HALO_LO:, :]
        cat[NA_HALO_LO:NA_HALO_LO + NA_TOK] = c_ref[...]
        cat[NA_HALO_LO + NA_TOK:] = n_ref[:NA_HALO_HI, :]
    lane = lax.broadcasted_iota(jnp.int32, (1, LANES), 1)
    low_half = lane < HEAD_DIM

    npair = B_HEADS // 2

    def row_info(rr):
        r = i * NA_ROWS_PER_STEP + rr
        rstart = jnp.clip(r - NA_KH // 2, 0, GRID_ROWS - NA_KH)
        start = pl.multiple_of((rstart - i * NA_ROWS_PER_STEP + NA_KH // 2) * GRID_W, GRID_W)
        return start, rstart - r + NA_KH - 1, slice(rr * GRID_W, (rr + 1) * GRID_W)

    def scores_of(rr, p):
        start, _, qrows = row_info(rr)
        cols = slice(LANES * p, LANES * (p + 1))
        qp = q_ref[qrows, cols]
        lhs = jnp.concatenate([jnp.where(low_half, qp, jnp.zeros_like(qp)),
                               jnp.where(low_half, jnp.zeros_like(qp), qp)], axis=0)
        return _dot_nt(lhs, kcat[pl.ds(start, NA_KEYS), cols])

    def softmax_of(rr, p, scores):
        _, ri0, _ = row_info(rr)
        bias = jnp.concatenate(
            [jnp.concatenate([bias_ref[2 * p + e, ri0 + 2 * m] for m in range(NA_KH // 2)], axis=1)
             for e in range(2)], axis=0)
        s = scores + bias
        pexp = jnp.exp2(s - jnp.max(s, axis=-1, keepdims=True))
        return pexp.astype(BF16), 1.0 / jnp.sum(pexp, axis=-1, keepdims=True)

    def emit(rr, p, probs, inv):
        start, _, qrows = row_info(rr)
        cols = slice(LANES * p, LANES * (p + 1))
        o = _dot(probs, vcat[pl.ds(start, NA_KEYS), cols]) * inv
        o_ref[qrows, cols] = jnp.where(low_half, o[:GRID_W], o[GRID_W:]).astype(BF16)

    for g in range(0, NA_ROWS_PER_STEP, NA_ROWS_PER_GROUP):
        units = [(rr, p) for rr in range(g, g + NA_ROWS_PER_GROUP) for p in range(npair)]
        scores = [scores_of(rr, p) for rr, p in units]
        probs = [softmax_of(rr, p, s) for (rr, p), s in zip(units, scores)]
        for (rr, p), (pr, inv) in zip(units, probs):
            emit(rr, p, pr, inv)


def _na_bias_table(rpb):
    qc = np.arange(GRID_W)[:, None]
    kc = np.arange(GRID_W)[None, :]
    cstart = np.clip(qc - NA_KW // 2, 0, GRID_W - NA_KW)
    col_valid = (kc >= cstart) & (kc < cstart + NA_KW)
    col_idx = np.clip(kc - qc + NA_KW - 1, 0, 2 * NA_KW - 2)
    onehot = (col_idx.reshape(1, -1) == np.arange(2 * NA_KW - 1)[:, None]).astype(np.float32)
    n_ri = 2 * NA_KH - 1
    t = jnp.dot(rpb.astype(F32).reshape(B_HEADS * n_ri, 2 * NA_KW - 1), jnp.asarray(onehot),
                precision=lax.Precision.HIGHEST).reshape(B_HEADS, n_ri, GRID_W, GRID_W)
    t = jnp.where(col_valid[None, None], t, NEG)
    return jnp.concatenate([t[:, :-1], t[:, 1:]], axis=-1)


def _na_attn(qb, kb, vb, bias):
    s = qb.shape[0]
    nb = s // NA_TOK
    prev = lambda i: (jnp.maximum(i - 1, 0), 0)
    cur = lambda i: (i, 0)
    nxt = lambda i: (jnp.minimum(i + 1, nb - 1), 0)
    blk = lambda f: pl.BlockSpec((NA_TOK, 512), f)
    return pl.pallas_call(
        _na_kernel,
        grid=(nb,),
        in_specs=[blk(cur), blk(prev), blk(cur), blk(nxt), blk(prev), blk(cur), blk(nxt),
                  _const_spec(bias.shape)],
        out_specs=blk(cur),
        out_shape=jax.ShapeDtypeStruct((s, 512), BF16),
        scratch_shapes=[pltpu.VMEM((NA_CAT, 512), BF16), pltpu.VMEM((NA_CAT, 512), BF16)],
        compiler_params=_params("parallel"),
        name="neighbourhood_attn",
    )(qb, kb, kb, kb, vb, vb, vb, bias)


FF_CHUNK = 1024


def _outproj_mlp_kernel(*refs, n_mix, gated):
    t = pl.program_id(0)
    wout_f32, g_ref, w1_f32, w2_f32, o_ref, wout_ref, w1_ref, w2_ref = refs[-8:]

    @pl.when(t < CAST_STEPS)
    def _():
        for src, dst in ((wout_f32, wout_ref), (w1_f32, w1_ref), (w2_f32, w2_ref)):
            _cast_slab(t, src, dst)

    @pl.when(t >= CAST_STEPS)
    def _():
        _outproj_mlp_tile(refs[:-8], wout_ref, g_ref, w1_ref, w2_ref, o_ref, n_mix=n_mix, gated=gated)


def _outproj_mlp_tile(act_refs, wout_ref, g_ref, w1_ref, w2_ref, o_ref, *, n_mix, gated):
    x_ref = act_refs[0]
    mix_refs = act_refs[1:1 + n_mix]
    if gated:
        (y_ref,), gate_ref, gn_ref = mix_refs, act_refs[1 + n_mix], act_refs[2 + n_mix]
        proj = None
        for h in range(RET_HEADS):
            cols = slice(h * RET_DV, (h + 1) * RET_DV)
            y = y_ref[:, cols].astype(F32)
            d = y - jnp.mean(y, axis=-1, keepdims=True)
            var = jnp.mean(d * d, axis=-1, keepdims=True)
            yn = d * lax.rsqrt(var + GN_EPS) * gn_ref[:, cols]
            mix_h = (yn * gate_ref[:, cols].astype(F32)).astype(BF16)
            part = _dot(mix_h, wout_ref[cols, :])
            proj = part if proj is None else proj + part
    else:
        proj = _dot(jnp.concatenate([m_ref[...] for m_ref in mix_refs], axis=1), wout_ref[...])
    x1 = x_ref[...] + proj
    hb = _rmsnorm_rows(x1, g_ref[...]).astype(BF16)
    acc = x1
    for c in range(0, D_FF, FF_CHUNK):
        a = jnp.maximum(_dot(hb, w1_ref[:, c:c + FF_CHUNK]), 0.0)
        acc = acc + _dot((a * a).astype(BF16), w2_ref[c:c + FF_CHUNK, :])
    o_ref[...] = acc


def _outproj_mlp(x, mixes, gate_gn, wout_all, mix_layer, g, w1_all, w2_all, layer, tm=512, name="outproj_mlp"):
    s = x.shape[0]
    row = _after_cast(lambda i: (i, 0))
    gated = gate_gn is not None
    acts = tuple(mixes) + ((gate_gn[0],) if gated else ())
    in_specs = [pl.BlockSpec((tm, D_MODEL), row)]
    in_specs += [pl.BlockSpec((tm, m.shape[1]), row) for m in acts]
    in_specs += [_const_spec(gate_gn[1].shape)] if gated else []
    in_specs += [_slab_spec(wout_all, mix_layer), _const_spec(g.shape),
                 _slab_spec(w1_all, layer), _slab_spec(w2_all, layer)]
    consts = ((gate_gn[1],) if gated else ()) + (wout_all, g, w1_all, w2_all)
    return pl.pallas_call(
        functools.partial(_outproj_mlp_kernel, n_mix=len(mixes), gated=gated),
        grid=(CAST_STEPS + s // tm,),
        in_specs=in_specs,
        out_specs=pl.BlockSpec((tm, D_MODEL), row),
        out_shape=jax.ShapeDtypeStruct((s, D_MODEL), F32),
        scratch_shapes=[pltpu.VMEM(w.shape[1:], BF16) for w in (wout_all, w1_all, w2_all)],
        compiler_params=_params("arbitrary"),
        name=name,
    )(x, *acts, *consts)


O_Q, O_V, O_G = 0, 2 * D_MODEL, 2 * D_MODEL + ODD_MIX


def _inproj_odd_kernel(x_ref, g_ref, w_f32, wkt_ref, q_ref, kt_ref, v_ref, gate_ref, w_ref):
    t = pl.program_id(0)

    @pl.when(t < CAST_STEPS)
    def _():
        _cast_slab(t, w_f32, w_ref)

    @pl.when(t >= CAST_STEPS)
    def _():
        xb = _rmsnorm_rows(x_ref[...], g_ref[...]).astype(BF16)
        for off in range(0, ODD_MIX, PROJ_CHUNK):
            gate_ref[:, off:off + PROJ_CHUNK] = _silu(
                _dot(xb, w_ref[:, O_G + off:O_G + off + PROJ_CHUNK])).astype(BF16)
        for off in range(0, ODD_MIX, PROJ_CHUNK):
            v_ref[:, off:off + PROJ_CHUNK] = _dot(xb, w_ref[:, O_V + off:O_V + off + PROJ_CHUNK]).astype(BF16)
        q_ref[...] = _dot(xb, w_ref[:, O_Q:O_Q + D_MODEL]).astype(BF16)
        kt_ref[...] = (_dot_nt(wkt_ref[...], xb) * RET_DK ** -0.5).astype(BF16)


def _inproj_odd(x, g, w_all, layer, wkt, tm=1024):
    s = x.shape[0]
    row = _after_cast(lambda i: (i, 0))
    return pl.pallas_call(
        _inproj_odd_kernel,
        grid=(CAST_STEPS + s // tm,),
        in_specs=[pl.BlockSpec((tm, D_MODEL), row), _const_spec((1, D_MODEL)),
                  _slab_spec(w_all, layer), _const_spec(wkt.shape)],
        out_specs=[pl.BlockSpec((tm, D_MODEL), row), pl.BlockSpec((D_MODEL, tm), _after_cast(lambda i: (0, i))),
                   pl.BlockSpec((tm, ODD_MIX), row), pl.BlockSpec((tm, ODD_MIX), row)],
        out_shape=[jax.ShapeDtypeStruct((s, D_MODEL), BF16), jax.ShapeDtypeStruct((D_MODEL, s), BF16),
                   jax.ShapeDtypeStruct((s, ODD_MIX), BF16), jax.ShapeDtypeStruct((s, ODD_MIX), BF16)],
        scratch_shapes=[pltpu.VMEM(w_all.shape[1:], BF16)],
        compiler_params=_params("arbitrary"),
        name="inproj_odd",
    )(x, g, w_all, wkt)


RET_CHUNKS_PER_STEP = 32
RET_TOK = RET_CHUNKS_PER_STEP * RET_CHUNK
RET_STEPS = SEQ // RET_TOK
RET_NCHUNK = SEQ // RET_CHUNK


def _retention_kernel(lg_ref, q_ref, kt_ref, v_ref, o_ref, rf_all, state):
    h = pl.program_id(0)
    t = pl.program_id(1)
    second = t >= RET_STEPS
    blk = jnp.where(second, 2 * RET_STEPS - 1 - t, t)
    lgf = lg_ref[0, h]
    lgb = lg_ref[1, h]
    tok_lane = lax.broadcasted_iota(jnp.int32, (1, RET_CHUNK), 1).astype(F32)
    tok_row = lax.broadcasted_iota(jnp.int32, (RET_CHUNK, 1), 0).astype(F32)
    chunk_len = jnp.full((1, 1), RET_CHUNK, F32)

    @pl.when((t == 0) | (t == RET_STEPS))
    def _():
        state[...] = jnp.zeros_like(state)

    @pl.when(jnp.logical_not(second))
    def _():
        zeta = jnp.exp(lgf * (RET_CHUNK - 1.0 - tok_lane))
        decay = jnp.exp(lgf * chunk_len)
        r = state[...]
        for c in range(RET_CHUNKS_PER_STEP):
            tok = slice(c * RET_CHUNK, (c + 1) * RET_CHUNK)
            kz = (kt_ref[:, tok].astype(F32) * zeta).astype(BF16)
            rf_all[blk * RET_CHUNKS_PER_STEP + c] = r.astype(BF16)
            r = r * decay + _dot(kz, v_ref[tok, :])
        state[...] = r

    @pl.when(second)
    def _():
        diff = (lax.broadcasted_iota(jnp.int32, (RET_CHUNK, RET_CHUNK), 0)
                - lax.broadcasted_iota(jnp.int32, (RET_CHUNK, RET_CHUNK), 1)).astype(F32)
        dsum = jnp.where(diff >= 0, jnp.exp(lgf * jnp.maximum(diff, 0.0)), jnp.exp(lgb * jnp.maximum(-diff, 0.0)))
        cross_f = jnp.exp(lgf * (tok_row + 1.0))
        cross_b = jnp.exp(lgb * (RET_CHUNK - tok_row))
        zeta = jnp.exp(lgb * tok_lane)
        decay = jnp.exp(lgb * chunk_len)
        r = state[...]
        for c in reversed(range(RET_CHUNKS_PER_STEP)):
            tok = slice(c * RET_CHUNK, (c + 1) * RET_CHUNK)
            q = q_ref[tok, :]
            kt = kt_ref[:, tok]
            v = v_ref[tok, :]
            a = (_dot(q, kt) * dsum).astype(BF16)
            qf = q.astype(F32)
            lhs = jnp.concatenate([a, (qf * cross_f).astype(BF16), (qf * cross_b).astype(BF16)], axis=1)
            rhs = jnp.concatenate([v, rf_all[blk * RET_CHUNKS_PER_STEP + c], r.astype(BF16)], axis=0)
            o_ref[tok, :] = _dot(lhs, rhs).astype(BF16)
            r = r * decay + _dot((kt.astype(F32) * zeta).astype(BF16), v)
        state[...] = r


def _retention(lg, q, kt, v):
    s = q.shape[0]

    def blk_of(t):
        return jnp.where(t >= RET_STEPS, 2 * RET_STEPS - 1 - t, t)

    def second_blk(t):
        return jnp.where(t >= RET_STEPS, 2 * RET_STEPS - 1 - t, RET_STEPS - 1)

    return pl.pallas_call(
        _retention_kernel,
        grid=(RET_HEADS, 2 * RET_STEPS),
        in_specs=[pl.BlockSpec(memory_space=pltpu.SMEM),
                  pl.BlockSpec((RET_TOK, RET_DK), lambda h, t: (second_blk(t), h)),
                  pl.BlockSpec((RET_DK, RET_TOK), lambda h, t: (h, blk_of(t))),
                  pl.BlockSpec((RET_TOK, RET_DV), lambda h, t: (blk_of(t), h))],
        out_specs=pl.BlockSpec((RET_TOK, RET_DV), lambda h, t: (second_blk(t), h)),
        out_shape=jax.ShapeDtypeStruct((s, ODD_MIX), BF16),
        scratch_shapes=[pltpu.VMEM((RET_NCHUNK, RET_DK, RET_DV), BF16), pltpu.VMEM((RET_DK, RET_DV), F32)],
        compiler_params=_params("arbitrary", "arbitrary"),
        name="retention",
    )(lg, q, kt, v)


def _even_qk_gain(qn_a, kn_a, qn_b, kn_b):
    scale = HEAD_DIM ** -0.5 * LOG2E
    gain = jnp.concatenate([jnp.tile(qn_a, A_Q_HEADS) * scale, jnp.tile(kn_a, A_KV_HEADS),
                            jnp.ones((A_KV_HEADS * HEAD_DIM,), F32),
                            jnp.tile(qn_b, B_HEADS) * scale, jnp.tile(kn_b, B_HEADS),
                            jnp.ones((B_HEADS * HEAD_DIM,), F32)]).astype(F32)
    return gain[None, :]


def kernel(x, attn_norm_e, w_in_e, q_norm_a, k_norm_a, sink_a, q_norm_b, k_norm_b, rpb_b, w_out_e,
           ret_norm_o, w_in_o, decay_fwd_o, decay_bwd_o, ret_gn_o, w_out_o,
           mlp_norm, w_mlp_in, w_mlp_out):
    b, s, d = x.shape
    xs = x.reshape(b * s, d)
    depth = mlp_norm.shape[0]
    for layer in range(depth):
        i = layer // 2
        g_mlp = mlp_norm[layer][None, :]
        if layer % 2 == 0:
            gain = _even_qk_gain(q_norm_a[i], k_norm_a[i], q_norm_b[i], k_norm_b[i])
            qa, ka2, va2, qb, kb, vb = _inproj_even(xs, attn_norm_e[i][None, :], w_in_e, i, gain)
            oa = _window_attn(sink_a[i].astype(F32) * LOG2E, qa, ka2, va2)
            ob = _na_attn(qb, kb, vb, _na_bias_table(rpb_b[i].astype(F32) * LOG2E))
            xs = _outproj_mlp(xs, (oa, ob), None, w_out_e, i, g_mlp, w_mlp_in, w_mlp_out, layer,
                              tm=1024, name="outproj_mlp_even")
        else:
            wkt = w_in_o[i, :, D_MODEL:2 * D_MODEL].T.astype(BF16)
            q, kt, v, gate = _inproj_odd(xs, ret_norm_o[i][None, :], w_in_o, i, wkt)
            lg = jnp.stack([jax.nn.log_sigmoid(decay_fwd_o[i].astype(F32)),
                            jax.nn.log_sigmoid(decay_bwd_o[i].astype(F32))])
            y = _retention(lg, q, kt, v)
            xs = _outproj_mlp(xs, (y,), (gate, ret_gn_o[i].astype(F32)[None, :]), w_out_o, i,
                              g_mlp, w_mlp_in, w_mlp_out, layer, name="outproj_mlp_odd")
    return xs.reshape(b, s, d)
```

```python
import functools

import numpy as np
import jax
import jax.numpy as jnp
from jax import lax
from jax.experimental import pallas as pl
from jax.experimental.pallas import tpu as pltpu

D_MODEL = 1024
SEQ = 16384
HEAD_DIM = 64
A_Q_HEADS = 8
A_KV_HEADS = 2
A_GROUP = A_Q_HEADS // A_KV_HEADS
A_WINDOW = 128
A_BLOCK = 128
B_HEADS = 8
GRID_W = 64
GRID_ROWS = SEQ // GRID_W
NA_KH = 8
NA_KW = 16
RET_HEADS = 8
RET_DK = D_MODEL // RET_HEADS
RET_DV = 2 * D_MODEL // RET_HEADS
RET_CHUNK = 256
D_FF = 4 * D_MODEL
ODD_MIX = RET_HEADS * RET_DV
RMS_EPS = 1e-6
GN_EPS = 1e-5

LANES = 128
MXU_N = 256
PROJ_CHUNK = 1024
NEG = -1e30
LOG2E = 1.4426950408889634
VMEM_LIMIT = 60 * 1024 * 1024

F32 = jnp.float32
BF16 = jnp.bfloat16

E_QA, E_KA, E_VA, E_QB, E_KB, E_VB = 0, 512, 640, 768, 1280, 1792
E_COLS = 2304


def _const_spec(shape):
    return pl.BlockSpec(shape, lambda *_: (0,) * len(shape), pipeline_mode=pl.Buffered(1))


def _params(*sem):
    return pltpu.CompilerParams(dimension_semantics=sem, vmem_limit_bytes=VMEM_LIMIT)


def _rmsnorm_rows(x, g):
    return x * lax.rsqrt(jnp.mean(x * x, axis=-1, keepdims=True) + RMS_EPS) * g


def _silu(g):
    return g * (0.5 * jnp.tanh(0.5 * g) + 0.5)


def _dot(a, b):
    return jnp.dot(a, b, preferred_element_type=F32)


CAST_STEPS = 8


def _slab_spec(w, layer):
    return pl.BlockSpec((None, w.shape[1] // CAST_STEPS, w.shape[2]),
                        lambda t: (layer, jnp.minimum(t, CAST_STEPS - 1), 0))


def _cast_slab(t, src_ref, dst_ref):
    rows = src_ref.shape[0]
    dst_ref[pl.ds(pl.multiple_of(t * rows, rows), rows), :] = src_ref[...].astype(BF16)


def _after_cast(index):
    return lambda t: index(jnp.maximum(t - CAST_STEPS, 0))


def _dot_nt(a, b):
    return lax.dot_general(a, b, (((1,), (1,)), ((), ())), preferred_element_type=F32)


def _inproj_even_kernel(x_ref, g_ref, w_f32, gain_ref, qa_ref, ka2_ref, va2_ref, qb_ref, kb_ref, vb_ref, w_ref):
    t = pl.program_id(0)

    @pl.when(t < CAST_STEPS)
    def _():
        _cast_slab(t, w_f32, w_ref)

    @pl.when(t >= CAST_STEPS)
    def _():
        xb = _rmsnorm_rows(x_ref[...], g_ref[...]).astype(BF16)
        r = lax.broadcasted_iota(jnp.int32, (MXU_N, MXU_N), 0) // HEAD_DIM
        c = lax.broadcasted_iota(jnp.int32, (MXU_N, MXU_N), 1) // HEAD_DIM
        head_mean = jnp.where(r == c, 1.0 / HEAD_DIM, 0.0).astype(BF16)
        is_ka = lax.broadcasted_iota(jnp.int32, (1, MXU_N), 1) < E_VA - E_KA
        low_half = lax.broadcasted_iota(jnp.int32, (1, LANES), 1) < HEAD_DIM
        outs = ((qa_ref, E_QA, 512, "all"), (None, E_KA, 256, "ka"), (qb_ref, E_QB, 512, "all"),
                (kb_ref, E_KB, 512, "all"), (vb_ref, E_VB, 512, "none"))
        chunks = [(ref, start + off, off, normed) for ref, start, width, normed in outs
                  for off in range(0, width, MXU_N)]
        ys = [_dot(xb, w_ref[:, col:col + MXU_N]) for _, col, _, _ in chunks]
        ms = [_dot((y * y).astype(BF16), head_mean) if normed != "none" else None
              for y, (_, _, _, normed) in zip(ys, chunks)]
        for y, m, (ref, col, off, normed) in zip(ys, ms, chunks):
            if normed != "none":
                scale = lax.rsqrt(m + RMS_EPS) * gain_ref[:, col:col + MXU_N]
                y = y * (scale if normed == "all" else jnp.where(is_ka, scale, 1.0))
            if ref is not None:
                ref[:, off:off + MXU_N] = y.astype(BF16)
                continue
            for pair, dup_ref in ((y[:, :LANES], ka2_ref), (y[:, LANES:], va2_ref)):
                swapped = pltpu.roll(pair, HEAD_DIM, axis=1)
                dup_ref[:, :LANES] = jnp.where(low_half, pair, swapped).astype(BF16)
                dup_ref[:, LANES:] = jnp.where(low_half, swapped, pair).astype(BF16)


def _inproj_even(x, g, w_all, layer, gain, tm=1024):
    s = x.shape[0]
    row = _after_cast(lambda i: (i, 0))
    widths = (512, 256, 256, 512, 512, 512)
    return pl.pallas_call(
        _inproj_even_kernel,
        grid=(CAST_STEPS + s // tm,),
        in_specs=[pl.BlockSpec((tm, D_MODEL), row), _const_spec((1, D_MODEL)),
                  _slab_spec(w_all, layer), _const_spec((1, E_COLS))],
        out_specs=[pl.BlockSpec((tm, wd), row) for wd in widths],
        out_shape=[jax.ShapeDtypeStruct((s, wd), BF16) for wd in widths],
        scratch_shapes=[pltpu.VMEM(w_all.shape[1:], BF16)],
        compiler_params=_params("arbitrary"),
        name="inproj_even",
    )(x, g, w_all, gain)


def _window_kernel(sink_ref, q_ref, kp_ref, kc_ref, kn_ref, vp_ref, vc_ref, vn_ref, o_ref, *, tq):
    i = pl.program_id(0)
    lane = lax.broadcasted_iota(jnp.int32, (1, LANES), 1)
    low_half = lane < HEAD_DIM
    qi = lax.broadcasted_iota(jnp.int32, (A_BLOCK, 3 * A_BLOCK), 0)
    kj = lax.broadcasted_iota(jnp.int32, (A_BLOCK, 3 * A_BLOCK), 1)
    absrel = jnp.abs(kj - A_BLOCK - qi)
    in_window = absrel <= A_WINDOW
    absrel_f = absrel.astype(F32)
    kcol = lax.broadcasted_iota(jnp.int32, (1, 3 * A_BLOCK), 1)
    edge_first = jnp.where((i == 0) & (kcol < A_BLOCK), NEG, 0.0)
    edge_last = jnp.where((i == pl.num_programs(0) - 1) & (kcol >= 2 * A_BLOCK), NEG, 0.0)
    nblk = tq // A_BLOCK
    penalty = []
    for h in range(A_Q_HEADS):
        slope = LOG2E * 2.0 ** (-(8.0 / A_Q_HEADS) * (h + 1.0))
        base = jnp.where(in_window, -slope * absrel_f, NEG)
        penalty.append({0: base + edge_first, nblk - 1: base + edge_last, None: base})
    units = [(j, b) for j in range(A_KV_HEADS) for b in range(nblk)]
    kcat, vcat = [], []
    for j in range(A_KV_HEADS):
        ks = slice(LANES * j, LANES * (j + 1))
        kcat.append(jnp.concatenate([kp_ref[tq - A_BLOCK:, ks], kc_ref[:, ks], kn_ref[:A_BLOCK, ks]], axis=0))
        vcat.append(jnp.concatenate([vp_ref[tq - A_BLOCK:, ks], vc_ref[:, ks], vn_ref[:A_BLOCK, ks]], axis=0))
    def scores_of(j, b):
        rows = slice(A_BLOCK * b, A_BLOCK * (b + 1))
        lhs = []
        for e in range(A_GROUP):
            pair = q_ref[rows, LANES * (2 * j + e // 2):LANES * (2 * j + e // 2 + 1)]
            lhs.append(jnp.where(low_half if e % 2 == 0 else ~low_half, pair, jnp.zeros_like(pair)))
        return _dot_nt(jnp.concatenate(lhs, axis=0), kcat[j][A_BLOCK * b:A_BLOCK * (b + 3)])

    def softmax_of(j, b, s_all):
        p_unit, inv_unit = [], []
        for e in range(A_GROUP):
            h = A_GROUP * j + e
            sk = sink_ref[h]
            pen = penalty[h]
            s = s_all[A_BLOCK * e:A_BLOCK * (e + 1)] + pen.get(b, pen[None])
            m = jnp.maximum(jnp.max(s, axis=-1, keepdims=True), sk)
            p = jnp.exp2(s - m)
            inv_unit.append(1.0 / (jnp.sum(p, axis=-1, keepdims=True) + jnp.exp2(sk - m)))
            p_unit.append(p.astype(BF16))
        return jnp.concatenate(p_unit, axis=0), jnp.concatenate(inv_unit, axis=0)

    def emit(j, b, p_all, inv):
        rows = slice(A_BLOCK * b, A_BLOCK * (b + 1))
        o = _dot(p_all, vcat[j][A_BLOCK * b:A_BLOCK * (b + 3)]) * inv
        for pr in range(A_GROUP // 2):
            lo = o[A_BLOCK * 2 * pr:A_BLOCK * (2 * pr + 1)]
            hi = o[A_BLOCK * (2 * pr + 1):A_BLOCK * (2 * pr + 2)]
            col = LANES * (2 * j + pr)
            o_ref[rows, col:col + LANES] = jnp.where(low_half, lo, hi).astype(BF16)

    scores = [scores_of(j, b) for j, b in units]
    probs = [softmax_of(j, b, s_all) for (j, b), s_all in zip(units, scores)]
    for (j, b), (p_all, inv) in zip(units, probs):
        emit(j, b, p_all, inv)


def _window_attn(sink, qa, ka2, va2, tq=512):
    s = qa.shape[0]
    nb = s // tq
    prev = lambda i: (jnp.maximum(i - 1, 0), 0)
    cur = lambda i: (i, 0)
    nxt = lambda i: (jnp.minimum(i + 1, nb - 1), 0)
    kv = lambda f: pl.BlockSpec((tq, 2 * LANES), f)
    return pl.pallas_call(
        functools.partial(_window_kernel, tq=tq),
        grid=(nb,),
        in_specs=[pl.BlockSpec(memory_space=pltpu.SMEM),
                  pl.BlockSpec((tq, 512), cur), kv(prev), kv(cur), kv(nxt), kv(prev), kv(cur), kv(nxt)],
        out_specs=pl.BlockSpec((tq, 512), cur),
        out_shape=jax.ShapeDtypeStruct((s, 512), BF16),
        compiler_params=_params("parallel"),
        name="window_gqa",
    )(sink, qa, ka2, ka2, ka2, va2, va2, va2)


NA_ROWS_PER_STEP = 8
NA_TOK = NA_ROWS_PER_STEP * GRID_W
NA_KEYS = NA_KH * GRID_W
NA_ROWS_PER_GROUP = 2
NA_HALO_LO = (NA_KH // 2) * GRID_W
NA_HALO_HI = (NA_KH - 1 - NA_KH // 2) * GRID_W
NA_CAT = NA_HALO_LO + NA_TOK + NA_HALO_HI


def _na_cat_base(i):
    first_row = jnp.clip(i * NA_ROWS_PER_STEP - NA_KH // 2, 0, GRID_ROWS - NA_CAT // GRID_W)
    return first_row * GRID_W


def _na_kernel(q_ref, kcat, vcat, bias_ref, o_ref):
    i = pl.program_id(0)
    lane = lax.broadcasted_iota(jnp.int32, (1, LANES), 1)
    low_half = lane < HEAD_DIM

    npair = B_HEADS // 2

    def row_info(rr):
        r = i * NA_ROWS_PER_STEP + rr
        rstart = jnp.clip(r - NA_KH // 2, 0, GRID_ROWS - NA_KH)
        start = pl.multiple_of(rstart * GRID_W - _na_cat_base(i), GRID_W)
        return start, rstart - r + NA_KH - 1, slice(rr * GRID_W, (rr + 1) * GRID_W)

    def scores_of(rr, p):
        start, _, qrows = row_info(rr)
        cols = slice(LANES * p, LANES * (p + 1))
        qp = q_ref[qrows, cols]
        lhs = jnp.concatenate([jnp.where(low_half, qp, jnp.zeros_like(qp)),
                               jnp.where(low_half, jnp.zeros_like(qp), qp)], axis=0)
        return _dot_nt(lhs, kcat[pl.ds(start, NA_KEYS), cols])

    def softmax_of(rr, p, scores):
        _, ri0, _ = row_info(rr)
        bias = jnp.concatenate(
            [jnp.concatenate([bias_ref[2 * p + e, ri0 + 2 * m] for m in range(NA_KH // 2)], axis=1)
             for e in range(2)], axis=0)
        s = scores + bias
        pexp = jnp.exp2(s - jnp.max(s, axis=-1, keepdims=True))
        return pexp.astype(BF16), 1.0 / jnp.sum(pexp, axis=-1, keepdims=True)

    def emit(rr, p, probs, inv):
        start, _, qrows = row_info(rr)
        cols = slice(LANES * p, LANES * (p + 1))
        o = _dot(probs, vcat[pl.ds(start, NA_KEYS), cols]) * inv
        o_ref[qrows, cols] = jnp.where(low_half, o[:GRID_W], o[GRID_W:]).astype(BF16)

    for g in range(0, NA_ROWS_PER_STEP, NA_ROWS_PER_GROUP):
        units = [(rr, p) for rr in range(g, g + NA_ROWS_PER_GROUP) for p in range(npair)]
        scores = [scores_of(rr, p) for rr, p in units]
        probs = [softmax_of(rr, p, s) for (rr, p), s in zip(units, scores)]
        for (rr, p), (pr, inv) in zip(units, probs):
            emit(rr, p, pr, inv)


def _na_bias_table(rpb):
    qc = np.arange(GRID_W)[:, None]
    kc = np.arange(GRID_W)[None, :]
    cstart = np.clip(qc - NA_KW // 2, 0, GRID_W - NA_KW)
    col_valid = (kc >= cstart) & (kc < cstart + NA_KW)
    col_idx = np.clip(kc - qc + NA_KW - 1, 0, 2 * NA_KW - 2)
    onehot = (col_idx.reshape(1, -1) == np.arange(2 * NA_KW - 1)[:, None]).astype(np.float32)
    n_ri = 2 * NA_KH - 1
    t = jnp.dot(rpb.astype(F32).reshape(B_HEADS * n_ri, 2 * NA_KW - 1), jnp.asarray(onehot),
                precision=lax.Precision.HIGHEST).reshape(B_HEADS, n_ri, GRID_W, GRID_W)
    t = jnp.where(col_valid[None, None], t, NEG)
    return jnp.concatenate([t[:, :-1], t[:, 1:]], axis=-1)


def _na_attn(qb, kb, vb, bias):
    s = qb.shape[0]
    nb = s // NA_TOK
    blk = pl.BlockSpec((NA_TOK, 512), lambda i: (i, 0))
    halo = pl.BlockSpec((pl.Element(NA_CAT), pl.Element(512)), lambda i: (_na_cat_base(i), 0))
    return pl.pallas_call(
        _na_kernel,
        grid=(nb,),
        in_specs=[blk, halo, halo, _const_spec(bias.shape)],
        out_specs=blk,
        out_shape=jax.ShapeDtypeStruct((s, 512), BF16),
        compiler_params=_params("parallel"),
        name="neighbourhood_attn",
    )(qb, kb, vb, bias)


FF_CHUNK = 1024


def _outproj_mlp_kernel(*refs, n_mix, gated):
    t = pl.program_id(0)
    wout_f32, g_ref, w1_f32, w2_f32, o_ref, wout_ref, w1_ref, w2_ref = refs[-8:]

    @pl.when(t < CAST_STEPS)
    def _():
        for src, dst in ((wout_f32, wout_ref), (w1_f32, w1_ref), (w2_f32, w2_ref)):
            _cast_slab(t, src, dst)

    @pl.when(t >= CAST_STEPS)
    def _():
        _outproj_mlp_tile(refs[:-8], wout_ref, g_ref, w1_ref, w2_ref, o_ref, n_mix=n_mix, gated=gated)


def _outproj_mlp_tile(act_refs, wout_ref, g_ref, w1_ref, w2_ref, o_ref, *, n_mix, gated):
    x_ref = act_refs[0]
    mix_refs = act_refs[1:1 + n_mix]
    if gated:
        (y_ref,), gate_ref, gn_ref = mix_refs, act_refs[1 + n_mix], act_refs[2 + n_mix]
        proj = None
        for h in range(RET_HEADS):
            cols = slice(h * RET_DV, (h + 1) * RET_DV)
            y = y_ref[:, cols].astype(F32)
            d = y - jnp.mean(y, axis=-1, keepdims=True)
            var = jnp.mean(d * d, axis=-1, keepdims=True)
            yn = d * lax.rsqrt(var + GN_EPS) * gn_ref[:, cols]
            mix_h = (yn * gate_ref[:, cols].astype(F32)).astype(BF16)
            part = _dot(mix_h, wout_ref[cols, :])
            proj = part if proj is None else proj + part
    else:
        proj = _dot(jnp.concatenate([m_ref[...] for m_ref in mix_refs], axis=1), wout_ref[...])
    x1 = x_ref[...] + proj
    hb = _rmsnorm_rows(x1, g_ref[...]).astype(BF16)
    acc = x1
    for c in range(0, D_FF, FF_CHUNK):
        a = jnp.maximum(_dot(hb, w1_ref[:, c:c + FF_CHUNK]), 0.0)
        acc = acc + _dot((a * a).astype(BF16), w2_ref[c:c + FF_CHUNK, :])
    o_ref[...] = acc


def _outproj_mlp(x, mixes, gate_gn, wout_all, mix_layer, g, w1_all, w2_all, layer, tm=512, name="outproj_mlp"):
    s = x.shape[0]
    row = _after_cast(lambda i: (i, 0))
    gated = gate_gn is not None
    acts = tuple(mixes) + ((gate_gn[0],) if gated else ())
    in_specs = [pl.BlockSpec((tm, D_MODEL), row)]
    in_specs += [pl.BlockSpec((tm, m.shape[1]), row) for m in acts]
    in_specs += [_const_spec(gate_gn[1].shape)] if gated else []
    in_specs += [_slab_spec(wout_all, mix_layer), _const_spec(g.shape),
                 _slab_spec(w1_all, layer), _slab_spec(w2_all, layer)]
    consts = ((gate_gn[1],) if gated else ()) + (wout_all, g, w1_all, w2_all)
    return pl.pallas_call(
        functools.partial(_outproj_mlp_kernel, n_mix=len(mixes), gated=gated),
        grid=(CAST_STEPS + s // tm,),
        in_specs=in_specs,
        out_specs=pl.BlockSpec((tm, D_MODEL), row),
        out_shape=jax.ShapeDtypeStruct((s, D_MODEL), F32),
        scratch_shapes=[pltpu.VMEM(w.shape[1:], BF16) for w in (wout_all, w1_all, w2_all)],
        compiler_params=_params("arbitrary"),
        name=name,
    )(x, *acts, *consts)


O_Q, O_V, O_G = 0, 2 * D_MODEL, 2 * D_MODEL + ODD_MIX


def _inproj_odd_kernel(x_ref, g_ref, w_f32, wkt_ref, q_ref, kt_ref, v_ref, gate_ref, w_ref):
    t = pl.program_id(0)

    @pl.when(t < CAST_STEPS)
    def _():
        _cast_slab(t, w_f32, w_ref)

    @pl.when(t >= CAST_STEPS)
    def _():
        xb = _rmsnorm_rows(x_ref[...], g_ref[...]).astype(BF16)
        for off in range(0, ODD_MIX, PROJ_CHUNK):
            gate_ref[:, off:off + PROJ_CHUNK] = _silu(
                _dot(xb, w_ref[:, O_G + off:O_G + off + PROJ_CHUNK])).astype(BF16)
        for off in range(0, ODD_MIX, PROJ_CHUNK):
            v_ref[:, off:off + PROJ_CHUNK] = _dot(xb, w_ref[:, O_V + off:O_V + off + PROJ_CHUNK]).astype(BF16)
        q_ref[...] = _dot(xb, w_ref[:, O_Q:O_Q + D_MODEL]).astype(BF16)
        kt_ref[...] = (_dot_nt(wkt_ref[...], xb) * RET_DK ** -0.5).astype(BF16)


def _inproj_odd(x, g, w_all, layer, wkt, tm=1024):
    s = x.shape[0]
    row = _after_cast(lambda i: (i, 0))
    return pl.pallas_call(
        _inproj_odd_kernel,
        grid=(CAST_STEPS + s // tm,),
        in_specs=[pl.BlockSpec((tm, D_MODEL), row), _const_spec((1, D_MODEL)),
                  _slab_spec(w_all, layer), _const_spec(wkt.shape)],
        out_specs=[pl.BlockSpec((tm, D_MODEL), row), pl.BlockSpec((D_MODEL, tm), _after_cast(lambda i: (0, i))),
                   pl.BlockSpec((tm, ODD_MIX), row), pl.BlockSpec((tm, ODD_MIX), row)],
        out_shape=[jax.ShapeDtypeStruct((s, D_MODEL), BF16), jax.ShapeDtypeStruct((D_MODEL, s), BF16),
                   jax.ShapeDtypeStruct((s, ODD_MIX), BF16), jax.ShapeDtypeStruct((s, ODD_MIX), BF16)],
        scratch_shapes=[pltpu.VMEM(w_all.shape[1:], BF16)],
        compiler_params=_params("arbitrary"),
        name="inproj_odd",
    )(x, g, w_all, wkt)


RET_CHUNKS_PER_STEP = 32
RET_TOK = RET_CHUNKS_PER_STEP * RET_CHUNK
RET_STEPS = SEQ // RET_TOK
RET_NCHUNK = SEQ // RET_CHUNK


def _retention_kernel(lg_ref, q_ref, kt_ref, v_ref, o_ref, rf_all, state):
    h = pl.program_id(0)
    t = pl.program_id(1)
    second = t >= RET_STEPS
    blk = jnp.where(second, 2 * RET_STEPS - 1 - t, t)
    lgf = lg_ref[0, h]
    lgb = lg_ref[1, h]
    tok_lane = lax.broadcasted_iota(jnp.int32, (1, RET_CHUNK), 1).astype(F32)
    tok_row = lax.broadcasted_iota(jnp.int32, (RET_CHUNK, 1), 0).astype(F32)
    chunk_len = jnp.full((1, 1), RET_CHUNK, F32)

    @pl.when((t == 0) | (t == RET_STEPS))
    def _():
        state[...] = jnp.zeros_like(state)

    @pl.when(jnp.logical_not(second))
    def _():
        zeta = jnp.exp(lgf * (RET_CHUNK - 1.0 - tok_lane))
        decay = jnp.exp(lgf * chunk_len)
        r = state[...]
        for c in range(RET_CHUNKS_PER_STEP):
            tok = slice(c * RET_CHUNK, (c + 1) * RET_CHUNK)
            kz = (kt_ref[:, tok].astype(F32) * zeta).astype(BF16)
            rf_all[blk * RET_CHUNKS_PER_STEP + c] = r.astype(BF16)
            r = r * decay + _dot(kz, v_ref[tok, :])
        state[...] = r

    @pl.when(second)
    def _():
        diff = (lax.broadcasted_iota(jnp.int32, (RET_CHUNK, RET_CHUNK), 0)
                - lax.broadcasted_iota(jnp.int32, (RET_CHUNK, RET_CHUNK), 1)).astype(F32)
        dsum = jnp.where(diff >= 0, jnp.exp(lgf * jnp.maximum(diff, 0.0)), jnp.exp(lgb * jnp.maximum(-diff, 0.0)))
        cross_f = jnp.exp(lgf * (tok_row + 1.0))
        cross_b = jnp.exp(lgb * (RET_CHUNK - tok_row))
        zeta = jnp.exp(lgb * tok_lane)
        decay = jnp.exp(lgb * chunk_len)
        r = state[...]
        for c in reversed(range(RET_CHUNKS_PER_STEP)):
            tok = slice(c * RET_CHUNK, (c + 1) * RET_CHUNK)
            q = q_ref[tok, :]
            kt = kt_ref[:, tok]
            v = v_ref[tok, :]
            a = (_dot(q, kt) * dsum).astype(BF16)
            qf = q.astype(F32)
            lhs = jnp.concatenate([a, (qf * cross_f).astype(BF16), (qf * cross_b).astype(BF16)], axis=1)
            rhs = jnp.concatenate([v, rf_all[blk * RET_CHUNKS_PER_STEP + c], r.astype(BF16)], axis=0)
            o_ref[tok, :] = _dot(lhs, rhs).astype(BF16)
            r = r * decay + _dot((kt.astype(F32) * zeta).astype(BF16), v)
        state[...] = r


def _retention(lg, q, kt, v):
    s = q.shape[0]

    def blk_of(t):
        return jnp.where(t >= RET_STEPS, 2 * RET_STEPS - 1 - t, t)

    def second_blk(t):
        return jnp.where(t >= RET_STEPS, 2 * RET_STEPS - 1 - t, RET_STEPS - 1)

    return pl.pallas_call(
        _retention_kernel,
        grid=(RET_HEADS, 2 * RET_STEPS),
        in_specs=[pl.BlockSpec(memory_space=pltpu.SMEM),
                  pl.BlockSpec((RET_TOK, RET_DK), lambda h, t: (second_blk(t), h)),
                  pl.BlockSpec((RET_DK, RET_TOK), lambda h, t: (h, blk_of(t))),
                  pl.BlockSpec((RET_TOK, RET_DV), lambda h, t: (blk_of(t), h))],
        out_specs=pl.BlockSpec((RET_TOK, RET_DV), lambda h, t: (second_blk(t), h)),
        out_shape=jax.ShapeDtypeStruct((s, ODD_MIX), BF16),
        scratch_shapes=[pltpu.VMEM((RET_NCHUNK, RET_DK, RET_DV), BF16), pltpu.VMEM((RET_DK, RET_DV), F32)],
        compiler_params=_params("arbitrary", "arbitrary"),
        name="retention",
    )(lg, q, kt, v)


def _even_qk_gain(qn_a, kn_a, qn_b, kn_b):
    scale = HEAD_DIM ** -0.5 * LOG2E
    gain = jnp.concatenate([jnp.tile(qn_a, A_Q_HEADS) * scale, jnp.tile(kn_a, A_KV_HEADS),
                            jnp.ones((A_KV_HEADS * HEAD_DIM,), F32),
                            jnp.tile(qn_b, B_HEADS) * scale, jnp.tile(kn_b, B_HEADS),
                            jnp.ones((B_HEADS * HEAD_DIM,), F32)]).astype(F32)
    return gain[None, :]


def kernel(x, attn_norm_e, w_in_e, q_norm_a, k_norm_a, sink_a, q_norm_b, k_norm_b, rpb_b, w_out_e,
           ret_norm_o, w_in_o, decay_fwd_o, decay_bwd_o, ret_gn_o, w_out_o,
           mlp_norm, w_mlp_in, w_mlp_out):
    b, s, d = x.shape
    xs = x.reshape(b * s, d)
    depth = mlp_norm.shape[0]
    for layer in range(depth):
        i = layer // 2
        g_mlp = mlp_norm[layer][None, :]
        if layer % 2 == 0:
            gain = _even_qk_gain(q_norm_a[i], k_norm_a[i], q_norm_b[i], k_norm_b[i])
            qa, ka2, va2, qb, kb, vb = _inproj_even(xs, attn_norm_e[i][None, :], w_in_e, i, gain)
            oa = _window_attn(sink_a[i].astype(F32) * LOG2E, qa, ka2, va2)
            ob = _na_attn(qb, kb, vb, _na_bias_table(rpb_b[i].astype(F32) * LOG2E))
            xs = _outproj_mlp(xs, (oa, ob), None, w_out_e, i, g_mlp, w_mlp_in, w_mlp_out, layer,
                              tm=1024, name="outproj_mlp_even")
        else:
            wkt = w_in_o[i, :, D_MODEL:2 * D_MODEL].T.astype(BF16)
            q, kt, v, gate = _inproj_odd(xs, ret_norm_o[i][None, :], w_in_o, i, wkt)
            lg = jnp.stack([jax.nn.log_sigmoid(decay_fwd_o[i].astype(F32)),
                            jax.nn.log_sigmoid(decay_bwd_o[i].astype(F32))])
            y = _retention(lg, q, kt, v)
            xs = _outproj_mlp(xs, (y,), (gate, ret_gn_o[i].astype(F32)[None, :]), w_out_o, i,
                              g_mlp, w_mlp_in, w_mlp_out, layer, name="outproj_mlp_odd")
    return xs.reshape(b, s, d)
```

```python
import functools

import numpy as np
import jax
import jax.numpy as jnp
from jax import lax
from jax.experimental import pallas as pl
from jax.experimental.pallas import tpu as pltpu

D_MODEL = 1024
SEQ = 16384
HEAD_DIM = 64
A_Q_HEADS = 8
A_KV_HEADS = 2
A_GROUP = A_Q_HEADS // A_KV_HEADS
A_WINDOW = 128
A_BLOCK = 128
B_HEADS = 8
GRID_W = 64
GRID_ROWS = SEQ // GRID_W
NA_KH = 8
NA_KW = 16
RET_HEADS = 8
RET_DK = D_MODEL // RET_HEADS
RET_DV = 2 * D_MODEL // RET_HEADS
RET_CHUNK = 256
D_FF = 4 * D_MODEL
ODD_MIX = RET_HEADS * RET_DV
RMS_EPS = 1e-6
GN_EPS = 1e-5

LANES = 128
MXU_N = 256
PROJ_CHUNK = 1024
NEG = -1e30
LOG2E = 1.4426950408889634
VMEM_LIMIT = 60 * 1024 * 1024

F32 = jnp.float32
BF16 = jnp.bfloat16

E_QA, E_KA, E_VA, E_QB, E_KB, E_VB = 0, 512, 640, 768, 1280, 1792
E_COLS = 2304


def _const_spec(shape):
    return pl.BlockSpec(shape, lambda *_: (0,) * len(shape), pipeline_mode=pl.Buffered(1))


def _params(*sem):
    return pltpu.CompilerParams(dimension_semantics=sem, vmem_limit_bytes=VMEM_LIMIT)


def _rmsnorm_rows(x, g):
    return x * lax.rsqrt(jnp.mean(x * x, axis=-1, keepdims=True) + RMS_EPS) * g


def _silu(g):
    return g * (0.5 * jnp.tanh(0.5 * g) + 0.5)


def _dot(a, b):
    return jnp.dot(a, b, preferred_element_type=F32)


CAST_STEPS = 8


def _slab_spec(w, layer):
    return pl.BlockSpec((None, w.shape[1] // CAST_STEPS, w.shape[2]),
                        lambda t: (layer, jnp.minimum(t, CAST_STEPS - 1), 0))


def _cast_slab(t, src_ref, dst_ref):
    rows = src_ref.shape[0]
    dst_ref[pl.ds(pl.multiple_of(t * rows, rows), rows), :] = src_ref[...].astype(BF16)


def _after_cast(index):
    return lambda t: index(jnp.maximum(t - CAST_STEPS, 0))


def _dot_nt(a, b):
    return lax.dot_general(a, b, (((1,), (1,)), ((), ())), preferred_element_type=F32)


def _inproj_even_kernel(x_ref, g_ref, w_f32, gain_ref, qa_ref, ka2_ref, va2_ref, qb_ref, kb_ref, vb_ref, w_ref):
    t = pl.program_id(0)

    @pl.when(t < CAST_STEPS)
    def _():
        _cast_slab(t, w_f32, w_ref)

    @pl.when(t >= CAST_STEPS)
    def _():
        xb = _rmsnorm_rows(x_ref[...], g_ref[...]).astype(BF16)
        r = lax.broadcasted_iota(jnp.int32, (MXU_N, MXU_N), 0) // HEAD_DIM
        c = lax.broadcasted_iota(jnp.int32, (MXU_N, MXU_N), 1) // HEAD_DIM
        head_mean = jnp.where(r == c, 1.0 / HEAD_DIM, 0.0).astype(BF16)
        is_ka = lax.broadcasted_iota(jnp.int32, (1, MXU_N), 1) < E_VA - E_KA
        low_half = lax.broadcasted_iota(jnp.int32, (1, LANES), 1) < HEAD_DIM
        outs = ((qa_ref, E_QA, 512, "all"), (None, E_KA, 256, "ka"), (qb_ref, E_QB, 512, "all"),
                (kb_ref, E_KB, 512, "all"), (vb_ref, E_VB, 512, "none"))
        chunks = [(ref, start + off, off, normed) for ref, start, width, normed in outs
                  for off in range(0, width, MXU_N)]
        ys = [_dot(xb, w_ref[:, col:col + MXU_N]) for _, col, _, _ in chunks]
        ms = [_dot((y * y).astype(BF16), head_mean) if normed != "none" else None
              for y, (_, _, _, normed) in zip(ys, chunks)]
        for y, m, (ref, col, off, normed) in zip(ys, ms, chunks):
            if normed != "none":
                scale = lax.rsqrt(m + RMS_EPS) * gain_ref[:, col:col + MXU_N]
                y = y * (scale if normed == "all" else jnp.where(is_ka, scale, 1.0))
            if ref is not None:
                ref[:, off:off + MXU_N] = y.astype(BF16)
                continue
            for pair, dup_ref in ((y[:, :LANES], ka2_ref), (y[:, LANES:], va2_ref)):
                swapped = pltpu.roll(pair, HEAD_DIM, axis=1)
                dup_ref[:, :LANES] = jnp.where(low_half, pair, swapped).astype(BF16)
                dup_ref[:, LANES:] = jnp.where(low_half, swapped, pair).astype(BF16)


def _inproj_even(x, g, w_all, layer, gain, tm=1024):
    s = x.shape[0]
    row = _after_cast(lambda i: (i, 0))
    widths = (512, 256, 256, 512, 512, 512)
    return pl.pallas_call(
        _inproj_even_kernel,
        grid=(CAST_STEPS + s // tm,),
        in_specs=[pl.BlockSpec((tm, D_MODEL), row), _const_spec((1, D_MODEL)),
                  _slab_spec(w_all, layer), _const_spec((1, E_COLS))],
        out_specs=[pl.BlockSpec((tm, wd), row) for wd in widths],
        out_shape=[jax.ShapeDtypeStruct((s, wd), BF16) for wd in widths],
        scratch_shapes=[pltpu.VMEM(w_all.shape[1:], BF16)],
        compiler_params=_params("arbitrary"),
        name="inproj_even",
    )(x, g, w_all, gain)


def _window_penalty():
    absrel = np.abs(np.arange(3 * A_BLOCK)[None, :] - A_BLOCK - np.arange(A_BLOCK)[:, None])
    slopes = LOG2E * 2.0 ** (-(8.0 / A_Q_HEADS) * (np.arange(A_Q_HEADS) + 1.0))
    pen = np.where(absrel <= A_WINDOW, -slopes[:, None, None] * absrel[None], NEG)
    return jnp.asarray(pen, F32)


def _window_kernel(sink_ref, pen_ref, q_ref, kp_ref, kc_ref, kn_ref, vp_ref, vc_ref, vn_ref, o_ref, *, tq):
    i = pl.program_id(0)
    lane = lax.broadcasted_iota(jnp.int32, (1, LANES), 1)
    low_half = lane < HEAD_DIM
    kcol = lax.broadcasted_iota(jnp.int32, (1, 3 * A_BLOCK), 1)
    edge_first = jnp.where((i == 0) & (kcol < A_BLOCK), NEG, 0.0)
    edge_last = jnp.where((i == pl.num_programs(0) - 1) & (kcol >= 2 * A_BLOCK), NEG, 0.0)
    nblk = tq // A_BLOCK
    edges = {0: edge_first, nblk - 1: edge_last}
    units = [(j, b) for j in range(A_KV_HEADS) for b in range(nblk)]
    kcat, vcat = [], []
    for j in range(A_KV_HEADS):
        ks = slice(LANES * j, LANES * (j + 1))
        kcat.append(jnp.concatenate([kp_ref[tq - A_BLOCK:, ks], kc_ref[:, ks], kn_ref[:A_BLOCK, ks]], axis=0))
        vcat.append(jnp.concatenate([vp_ref[tq - A_BLOCK:, ks], vc_ref[:, ks], vn_ref[:A_BLOCK, ks]], axis=0))
    def scores_of(j, b):
        rows = slice(A_BLOCK * b, A_BLOCK * (b + 1))
        lhs = []
        for e in range(A_GROUP):
            pair = q_ref[rows, LANES * (2 * j + e // 2):LANES * (2 * j + e // 2 + 1)]
            lhs.append(jnp.where(low_half if e % 2 == 0 else ~low_half, pair, jnp.zeros_like(pair)))
        return _dot_nt(jnp.concatenate(lhs, axis=0), kcat[j][A_BLOCK * b:A_BLOCK * (b + 3)])

    def softmax_of(j, b, s_all):
        p_unit, inv_unit = [], []
        for e in range(A_GROUP):
            h = A_GROUP * j + e
            sk = sink_ref[h]
            pen = pen_ref[h] + edges[b] if b in edges else pen_ref[h]
            s = s_all[A_BLOCK * e:A_BLOCK * (e + 1)] + pen
            m = jnp.maximum(jnp.max(s, axis=-1, keepdims=True), sk)
            p = jnp.exp2(s - m)
            inv_unit.append(1.0 / (jnp.sum(p, axis=-1, keepdims=True) + jnp.exp2(sk - m)))
            p_unit.append(p.astype(BF16))
        return jnp.concatenate(p_unit, axis=0), jnp.concatenate(inv_unit, axis=0)

    def emit(j, b, p_all, inv):
        rows = slice(A_BLOCK * b, A_BLOCK * (b + 1))
        o = _dot(p_all, vcat[j][A_BLOCK * b:A_BLOCK * (b + 3)]) * inv
        for pr in range(A_GROUP // 2):
            lo = o[A_BLOCK * 2 * pr:A_BLOCK * (2 * pr + 1)]
            hi = o[A_BLOCK * (2 * pr + 1):A_BLOCK * (2 * pr + 2)]
            col = LANES * (2 * j + pr)
            o_ref[rows, col:col + LANES] = jnp.where(low_half, lo, hi).astype(BF16)

    scores = [scores_of(j, b) for j, b in units]
    probs = [softmax_of(j, b, s_all) for (j, b), s_all in zip(units, scores)]
    for (j, b), (p_all, inv) in zip(units, probs):
        emit(j, b, p_all, inv)


def _window_attn(sink, qa, ka2, va2, tq=512):
    s = qa.shape[0]
    nb = s // tq
    prev = lambda i: (jnp.maximum(i - 1, 0), 0)
    cur = lambda i: (i, 0)
    nxt = lambda i: (jnp.minimum(i + 1, nb - 1), 0)
    kv = lambda f: pl.BlockSpec((tq, 2 * LANES), f)
    return pl.pallas_call(
        functools.partial(_window_kernel, tq=tq),
        grid=(nb,),
        in_specs=[pl.BlockSpec(memory_space=pltpu.SMEM), _const_spec((A_Q_HEADS, A_BLOCK, 3 * A_BLOCK)),
                  pl.BlockSpec((tq, 512), cur), kv(prev), kv(cur), kv(nxt), kv(prev), kv(cur), kv(nxt)],
        out_specs=pl.BlockSpec((tq, 512), cur),
        out_shape=jax.ShapeDtypeStruct((s, 512), BF16),
        compiler_params=_params("parallel"),
        name="window_gqa",
    )(sink, _window_penalty(), qa, ka2, ka2, ka2, va2, va2, va2)


NA_ROWS_PER_STEP = 8
NA_TOK = NA_ROWS_PER_STEP * GRID_W
NA_KEYS = NA_KH * GRID_W
NA_ROWS_PER_GROUP = 2
NA_HALO_LO = (NA_KH // 2) * GRID_W
NA_HALO_HI = (NA_KH - 1 - NA_KH // 2) * GRID_W
NA_CAT = NA_HALO_LO + NA_TOK + NA_HALO_HI


def _na_cat_base(i):
    first_row = jnp.clip(i * NA_ROWS_PER_STEP - NA_KH // 2, 0, GRID_ROWS - NA_CAT // GRID_W)
    return first_row * GRID_W


def _na_kernel(q_ref, kcat, vcat, bias_ref, o_ref):
    i = pl.program_id(0)
    lane = lax.broadcasted_iota(jnp.int32, (1, LANES), 1)
    low_half = lane < HEAD_DIM

    npair = B_HEADS // 2

    def row_info(rr):
        r = i * NA_ROWS_PER_STEP + rr
        rstart = jnp.clip(r - NA_KH // 2, 0, GRID_ROWS - NA_KH)
        start = pl.multiple_of(rstart * GRID_W - _na_cat_base(i), GRID_W)
        return start, rstart - r + NA_KH - 1, slice(rr * GRID_W, (rr + 1) * GRID_W)

    def scores_of(rr, p):
        start, _, qrows = row_info(rr)
        cols = slice(LANES * p, LANES * (p + 1))
        qp = q_ref[qrows, cols]
        lhs = jnp.concatenate([jnp.where(low_half, qp, jnp.zeros_like(qp)),
                               jnp.where(low_half, jnp.zeros_like(qp), qp)], axis=0)
        return _dot_nt(lhs, kcat[pl.ds(start, NA_KEYS), cols])

    def softmax_of(rr, p, scores):
        _, ri0, _ = row_info(rr)
        bias = jnp.concatenate(
            [jnp.concatenate([bias_ref[2 * p + e, ri0 + 2 * m] for m in range(NA_KH // 2)], axis=1)
             for e in range(2)], axis=0)
        s = scores + bias
        pexp = jnp.exp2(s - jnp.max(s, axis=-1, keepdims=True))
        return pexp.astype(BF16), 1.0 / jnp.sum(pexp, axis=-1, keepdims=True)

    def emit(rr, p, probs, inv):
        start, _, qrows = row_info(rr)
        cols = slice(LANES * p, LANES * (p + 1))
        o = _dot(probs, vcat[pl.ds(start, NA_KEYS), cols]) * inv
        o_ref[qrows, cols] = jnp.where(low_half, o[:GRID_W], o[GRID_W:]).astype(BF16)

    for g in range(0, NA_ROWS_PER_STEP, NA_ROWS_PER_GROUP):
        units = [(rr, p) for rr in range(g, g + NA_ROWS_PER_GROUP) for p in range(npair)]
        scores = [scores_of(rr, p) for rr, p in units]
        probs = [softmax_of(rr, p, s) for (rr, p), s in zip(units, scores)]
        for (rr, p), (pr, inv) in zip(units, probs):
            emit(rr, p, pr, inv)


def _na_bias_table(rpb):
    qc = np.arange(GRID_W)[:, None]
    kc = np.arange(GRID_W)[None, :]
    cstart = np.clip(qc - NA_KW // 2, 0, GRID_W - NA_KW)
    col_valid = (kc >= cstart) & (kc < cstart + NA_KW)
    col_idx = np.clip(kc - qc + NA_KW - 1, 0, 2 * NA_KW - 2)
    onehot = (col_idx.reshape(1, -1) == np.arange(2 * NA_KW - 1)[:, None]).astype(np.float32)
    n_ri = 2 * NA_KH - 1
    t = jnp.dot(rpb.astype(F32).reshape(B_HEADS * n_ri, 2 * NA_KW - 1), jnp.asarray(onehot),
                precision=lax.Precision.HIGHEST).reshape(B_HEADS, n_ri, GRID_W, GRID_W)
    t = jnp.where(col_valid[None, None], t, NEG)
    return jnp.concatenate([t[:, :-1], t[:, 1:]], axis=-1)


def _na_attn(qb, kb, vb, bias):
    s = qb.shape[0]
    nb = s // NA_TOK
    blk = pl.BlockSpec((NA_TOK, 512), lambda i: (i, 0))
    halo = pl.BlockSpec((pl.Element(NA_CAT), pl.Element(512)), lambda i: (_na_cat_base(i), 0))
    return pl.pallas_call(
        _na_kernel,
        grid=(nb,),
        in_specs=[blk, halo, halo, _const_spec(bias.shape)],
        out_specs=blk,
        out_shape=jax.ShapeDtypeStruct((s, 512), BF16),
        compiler_params=_params("parallel"),
        name="neighbourhood_attn",
    )(qb, kb, vb, bias)


FF_CHUNK = 1024


def _outproj_mlp_kernel(*refs, n_mix, gated):
    t = pl.program_id(0)
    wout_f32, g_ref, w1_f32, w2_f32, o_ref, wout_ref, w1_ref, w2_ref = refs[-8:]

    @pl.when(t < CAST_STEPS)
    def _():
        for src, dst in ((wout_f32, wout_ref), (w1_f32, w1_ref), (w2_f32, w2_ref)):
            _cast_slab(t, src, dst)

    @pl.when(t >= CAST_STEPS)
    def _():
        _outproj_mlp_tile(refs[:-8], wout_ref, g_ref, w1_ref, w2_ref, o_ref, n_mix=n_mix, gated=gated)


def _outproj_mlp_tile(act_refs, wout_ref, g_ref, w1_ref, w2_ref, o_ref, *, n_mix, gated):
    x_ref = act_refs[0]
    mix_refs = act_refs[1:1 + n_mix]
    if gated:
        (y_ref,), gate_ref, gn_ref = mix_refs, act_refs[1 + n_mix], act_refs[2 + n_mix]
        proj = None
        for h in range(RET_HEADS):
            cols = slice(h * RET_DV, (h + 1) * RET_DV)
            y = y_ref[:, cols].astype(F32)
            d = y - jnp.mean(y, axis=-1, keepdims=True)
            var = jnp.mean(d * d, axis=-1, keepdims=True)
            yn = d * lax.rsqrt(var + GN_EPS) * gn_ref[:, cols]
            mix_h = (yn * gate_ref[:, cols].astype(F32)).astype(BF16)
            part = _dot(mix_h, wout_ref[cols, :])
            proj = part if proj is None else proj + part
    else:
        proj = _dot(jnp.concatenate([m_ref[...] for m_ref in mix_refs], axis=1), wout_ref[...])
    x1 = x_ref[...] + proj
    hb = _rmsnorm_rows(x1, g_ref[...]).astype(BF16)
    acc = x1
    for c in range(0, D_FF, FF_CHUNK):
        a = jnp.maximum(_dot(hb, w1_ref[:, c:c + FF_CHUNK]), 0.0)
        acc = acc + _dot((a * a).astype(BF16), w2_ref[c:c + FF_CHUNK, :])
    o_ref[...] = acc


def _outproj_mlp(x, mixes, gate_gn, wout_all, mix_layer, g, w1_all, w2_all, layer, tm=512, name="outproj_mlp"):
    s = x.shape[0]
    row = _after_cast(lambda i: (i, 0))
    gated = gate_gn is not None
    acts = tuple(mixes) + ((gate_gn[0],) if gated else ())
    in_specs = [pl.BlockSpec((tm, D_MODEL), row)]
    in_specs += [pl.BlockSpec((tm, m.shape[1]), row) for m in acts]
    in_specs += [_const_spec(gate_gn[1].shape)] if gated else []
    in_specs += [_slab_spec(wout_all, mix_layer), _const_spec(g.shape),
                 _slab_spec(w1_all, layer), _slab_spec(w2_all, layer)]
    consts = ((gate_gn[1],) if gated else ()) + (wout_all, g, w1_all, w2_all)
    return pl.pallas_call(
        functools.partial(_outproj_mlp_kernel, n_mix=len(mixes), gated=gated),
        grid=(CAST_STEPS + s // tm,),
        in_specs=in_specs,
        out_specs=pl.BlockSpec((tm, D_MODEL), row),
        out_shape=jax.ShapeDtypeStruct((s, D_MODEL), F32),
        scratch_shapes=[pltpu.VMEM(w.shape[1:], BF16) for w in (wout_all, w1_all, w2_all)],
        compiler_params=_params("arbitrary"),
        name=name,
    )(x, *acts, *consts)


O_Q, O_V, O_G = 0, 2 * D_MODEL, 2 * D_MODEL + ODD_MIX


def _inproj_odd_kernel(x_ref, g_ref, w_f32, wkt_ref, q_ref, kt_ref, v_ref, gate_ref, w_ref):
    t = pl.program_id(0)

    @pl.when(t < CAST_STEPS)
    def _():
        _cast_slab(t, w_f32, w_ref)

    @pl.when(t >= CAST_STEPS)
    def _():
        xb = _rmsnorm_rows(x_ref[...], g_ref[...]).astype(BF16)
        for off in range(0, ODD_MIX, PROJ_CHUNK):
            gate_ref[:, off:off + PROJ_CHUNK] = _silu(
                _dot(xb, w_ref[:, O_G + off:O_G + off + PROJ_CHUNK])).astype(BF16)
        for off in range(0, ODD_MIX, PROJ_CHUNK):
            v_ref[:, off:off + PROJ_CHUNK] = _dot(xb, w_ref[:, O_V + off:O_V + off + PROJ_CHUNK]).astype(BF16)
        q_ref[...] = _dot(xb, w_ref[:, O_Q:O_Q + D_MODEL]).astype(BF16)
        kt_ref[...] = (_dot_nt(wkt_ref[...], xb) * RET_DK ** -0.5).astype(BF16)


def _inproj_odd(x, g, w_all, layer, wkt, tm=1024):
    s = x.shape[0]
    row = _after_cast(lambda i: (i, 0))
    return pl.pallas_call(
        _inproj_odd_kernel,
        grid=(CAST_STEPS + s // tm,),
        in_specs=[pl.BlockSpec((tm, D_MODEL), row), _const_spec((1, D_MODEL)),
                  _slab_spec(w_all, layer), _const_spec(wkt.shape)],
        out_specs=[pl.BlockSpec((tm, D_MODEL), row), pl.BlockSpec((D_MODEL, tm), _after_cast(lambda i: (0, i))),
                   pl.BlockSpec((tm, ODD_MIX), row), pl.BlockSpec((tm, ODD_MIX), row)],
        out_shape=[jax.ShapeDtypeStruct((s, D_MODEL), BF16), jax.ShapeDtypeStruct((D_MODEL, s), BF16),
                   jax.ShapeDtypeStruct((s, ODD_MIX), BF16), jax.ShapeDtypeStruct((s, ODD_MIX), BF16)],
        scratch_shapes=[pltpu.VMEM(w_all.shape[1:], BF16)],
        compiler_params=_params("arbitrary"),
        name="inproj_odd",
    )(x, g, w_all, wkt)


RET_CHUNKS_PER_STEP = 32
RET_TOK = RET_CHUNKS_PER_STEP * RET_CHUNK
RET_STEPS = SEQ // RET_TOK
RET_NCHUNK = SEQ // RET_CHUNK


def _retention_kernel(lg_ref, q_ref, kt_ref, v_ref, o_ref, rf_all, state):
    h = pl.program_id(0)
    t = pl.program_id(1)
    second = t >= RET_STEPS
    blk = jnp.where(second, 2 * RET_STEPS - 1 - t, t)
    lgf = lg_ref[0, h]
    lgb = lg_ref[1, h]
    tok_lane = lax.broadcasted_iota(jnp.int32, (1, RET_CHUNK), 1).astype(F32)
    tok_row = lax.broadcasted_iota(jnp.int32, (RET_CHUNK, 1), 0).astype(F32)
    chunk_len = jnp.full((1, 1), RET_CHUNK, F32)

    @pl.when((t == 0) | (t == RET_STEPS))
    def _():
        state[...] = jnp.zeros_like(state)

    @pl.when(jnp.logical_not(second))
    def _():
        zeta = jnp.exp(lgf * (RET_CHUNK - 1.0 - tok_lane))
        decay = jnp.exp(lgf * chunk_len)
        r = state[...]
        for c in range(RET_CHUNKS_PER_STEP):
            tok = slice(c * RET_CHUNK, (c + 1) * RET_CHUNK)
            kz = (kt_ref[:, tok].astype(F32) * zeta).astype(BF16)
            rf_all[blk * RET_CHUNKS_PER_STEP + c] = r.astype(BF16)
            r = r * decay + _dot(kz, v_ref[tok, :])
        state[...] = r

    @pl.when(second)
    def _():
        diff = (lax.broadcasted_iota(jnp.int32, (RET_CHUNK, RET_CHUNK), 0)
                - lax.broadcasted_iota(jnp.int32, (RET_CHUNK, RET_CHUNK), 1)).astype(F32)
        dsum = jnp.where(diff >= 0, jnp.exp(lgf * jnp.maximum(diff, 0.0)), jnp.exp(lgb * jnp.maximum(-diff, 0.0)))
        cross_f = jnp.exp(lgf * (tok_row + 1.0))
        cross_b = jnp.exp(lgb * (RET_CHUNK - tok_row))
        zeta = jnp.exp(lgb * tok_lane)
        decay = jnp.exp(lgb * chunk_len)
        r = state[...]
        for c in reversed(range(RET_CHUNKS_PER_STEP)):
            tok = slice(c * RET_CHUNK, (c + 1) * RET_CHUNK)
            q = q_ref[tok, :]
            kt = kt_ref[:, tok]
            v = v_ref[tok, :]
            a = (_dot(q, kt) * dsum).astype(BF16)
            qf = q.astype(F32)
            lhs = jnp.concatenate([a, (qf * cross_f).astype(BF16), (qf * cross_b).astype(BF16)], axis=1)
            rhs = jnp.concatenate([v, rf_all[blk * RET_CHUNKS_PER_STEP + c], r.astype(BF16)], axis=0)
            o_ref[tok, :] = _dot(lhs, rhs).astype(BF16)
            r = r * decay + _dot((kt.astype(F32) * zeta).astype(BF16), v)
        state[...] = r


def _retention(lg, q, kt, v):
    s = q.shape[0]

    def blk_of(t):
        return jnp.where(t >= RET_STEPS, 2 * RET_STEPS - 1 - t, t)

    def second_blk(t):
        return jnp.where(t >= RET_STEPS, 2 * RET_STEPS - 1 - t, RET_STEPS - 1)

    return pl.pallas_call(
        _retention_kernel,
        grid=(RET_HEADS, 2 * RET_STEPS),
        in_specs=[pl.BlockSpec(memory_space=pltpu.SMEM),
                  pl.BlockSpec((RET_TOK, RET_DK), lambda h, t: (second_blk(t), h)),
                  pl.BlockSpec((RET_DK, RET_TOK), lambda h, t: (h, blk_of(t))),
                  pl.BlockSpec((RET_TOK, RET_DV), lambda h, t: (blk_of(t), h))],
        out_specs=pl.BlockSpec((RET_TOK, RET_DV), lambda h, t: (second_blk(t), h)),
        out_shape=jax.ShapeDtypeStruct((s, ODD_MIX), BF16),
        scratch_shapes=[pltpu.VMEM((RET_NCHUNK, RET_DK, RET_DV), BF16), pltpu.VMEM((RET_DK, RET_DV), F32)],
        compiler_params=_params("arbitrary", "arbitrary"),
        name="retention",
    )(lg, q, kt, v)


def _even_qk_gain(qn_a, kn_a, qn_b, kn_b):
    scale = HEAD_DIM ** -0.5 * LOG2E
    gain = jnp.concatenate([jnp.tile(qn_a, A_Q_HEADS) * scale, jnp.tile(kn_a, A_KV_HEADS),
                            jnp.ones((A_KV_HEADS * HEAD_DIM,), F32),
                            jnp.tile(qn_b, B_HEADS) * scale, jnp.tile(kn_b, B_HEADS),
                            jnp.ones((B_HEADS * HEAD_DIM,), F32)]).astype(F32)
    return gain[None, :]


def kernel(x, attn_norm_e, w_in_e, q_norm_a, k_norm_a, sink_a, q_norm_b, k_norm_b, rpb_b, w_out_e,
           ret_norm_o, w_in_o, decay_fwd_o, decay_bwd_o, ret_gn_o, w_out_o,
           mlp_norm, w_mlp_in, w_mlp_out):
    b, s, d = x.shape
    xs = x.reshape(b * s, d)
    depth = mlp_norm.shape[0]
    for layer in range(depth):
        i = layer // 2
        g_mlp = mlp_norm[layer][None, :]
        if layer % 2 == 0:
            gain = _even_qk_gain(q_norm_a[i], k_norm_a[i], q_norm_b[i], k_norm_b[i])
            qa, ka2, va2, qb, kb, vb = _inproj_even(xs, attn_norm_e[i][None, :], w_in_e, i, gain)
            oa = _window_attn(sink_a[i].astype(F32) * LOG2E, qa, ka2, va2)
            ob = _na_attn(qb, kb, vb, _na_bias_table(rpb_b[i].astype(F32) * LOG2E))
            xs = _outproj_mlp(xs, (oa, ob), None, w_out_e, i, g_mlp, w_mlp_in, w_mlp_out, layer,
                              tm=1024, name="outproj_mlp_even")
        else:
            wkt = w_in_o[i, :, D_MODEL:2 * D_MODEL].T.astype(BF16)
            q, kt, v, gate = _inproj_odd(xs, ret_norm_o[i][None, :], w_in_o, i, wkt)
            lg = jnp.stack([jax.nn.log_sigmoid(decay_fwd_o[i].astype(F32)),
                            jax.nn.log_sigmoid(decay_bwd_o[i].astype(F32))])
            y = _retention(lg, q, kt, v)
            xs = _outproj_mlp(xs, (y,), (gate, ret_gn_o[i].astype(F32)[None, :]), w_out_o, i,
                              g_mlp, w_mlp_in, w_mlp_out, layer, name="outproj_mlp_odd")
    return xs.reshape(b, s, d)
```

```python
import functools

import numpy as np
import jax
import jax.numpy as jnp
from jax import lax
from jax.experimental import pallas as pl
from jax.experimental.pallas import tpu as pltpu

D_MODEL = 1024
SEQ = 16384
HEAD_DIM = 64
A_Q_HEADS = 8
A_KV_HEADS = 2
A_GROUP = A_Q_HEADS // A_KV_HEADS
A_WINDOW = 128
A_BLOCK = 128
B_HEADS = 8
GRID_W = 64
GRID_ROWS = SEQ // GRID_W
NA_KH = 8
NA_KW = 16
RET_HEADS = 8
RET_DK = D_MODEL // RET_HEADS
RET_DV = 2 * D_MODEL // RET_HEADS
RET_CHUNK = 256
D_FF = 4 * D_MODEL
ODD_MIX = RET_HEADS * RET_DV
RMS_EPS = 1e-6
GN_EPS = 1e-5

LANES = 128
MXU_N = 256
PROJ_CHUNK = 1024
NEG = -1e30
LOG2E = 1.4426950408889634
VMEM_LIMIT = 60 * 1024 * 1024

F32 = jnp.float32
BF16 = jnp.bfloat16

E_QA, E_KA, E_VA, E_QB, E_KB, E_VB = 0, 512, 640, 768, 1280, 1792
E_COLS = 2304


def _const_spec(shape):
    return pl.BlockSpec(shape, lambda *_: (0,) * len(shape), pipeline_mode=pl.Buffered(1))


def _params(*sem):
    return pltpu.CompilerParams(dimension_semantics=sem, vmem_limit_bytes=VMEM_LIMIT)


def _rmsnorm_rows(x, g):
    return x * lax.rsqrt(jnp.mean(x * x, axis=-1, keepdims=True) + RMS_EPS) * g


def _silu(g):
    return g * (0.5 * jnp.tanh(0.5 * g) + 0.5)


def _dot(a, b):
    return jnp.dot(a, b, preferred_element_type=F32)


CAST_STEPS = 8


def _slab_spec(w, layer):
    return pl.BlockSpec((None, w.shape[1] // CAST_STEPS, w.shape[2]),
                        lambda t: (layer, jnp.minimum(t, CAST_STEPS - 1), 0))


def _cast_slab(t, src_ref, dst_ref):
    rows = src_ref.shape[0]
    dst_ref[pl.ds(pl.multiple_of(t * rows, rows), rows), :] = src_ref[...].astype(BF16)


def _after_cast(index):
    return lambda t: index(jnp.maximum(t - CAST_STEPS, 0))


def _dot_nt(a, b):
    return lax.dot_general(a, b, (((1,), (1,)), ((), ())), preferred_element_type=F32)


def _inproj_even_kernel(x_ref, g_ref, w_f32, gain_ref, qa_ref, ka2_ref, va2_ref, qb_ref, kb_ref, vb_ref, w_ref):
    t = pl.program_id(0)

    @pl.when(t < CAST_STEPS)
    def _():
        _cast_slab(t, w_f32, w_ref)

    @pl.when(t >= CAST_STEPS)
    def _():
        xb = _rmsnorm_rows(x_ref[...], g_ref[...]).astype(BF16)
        r = lax.broadcasted_iota(jnp.int32, (MXU_N, MXU_N), 0) // HEAD_DIM
        c = lax.broadcasted_iota(jnp.int32, (MXU_N, MXU_N), 1) // HEAD_DIM
        head_mean = jnp.where(r == c, 1.0 / HEAD_DIM, 0.0).astype(BF16)
        is_ka = lax.broadcasted_iota(jnp.int32, (1, MXU_N), 1) < E_VA - E_KA
        low_half = lax.broadcasted_iota(jnp.int32, (1, LANES), 1) < HEAD_DIM
        outs = ((qa_ref, E_QA, 512, "all"), (None, E_KA, 256, "ka"), (qb_ref, E_QB, 512, "all"),
                (kb_ref, E_KB, 512, "all"), (vb_ref, E_VB, 512, "none"))
        chunks = [(ref, start + off, off, normed) for ref, start, width, normed in outs
                  for off in range(0, width, MXU_N)]
        ys = [_dot(xb, w_ref[:, col:col + MXU_N]) for _, col, _, _ in chunks]
        ms = [_dot((y * y).astype(BF16), head_mean) if normed != "none" else None
              for y, (_, _, _, normed) in zip(ys, chunks)]
        for y, m, (ref, col, off, normed) in zip(ys, ms, chunks):
            if normed != "none":
                scale = lax.rsqrt(m + RMS_EPS) * gain_ref[:, col:col + MXU_N]
                y = y * (scale if normed == "all" else jnp.where(is_ka, scale, 1.0))
            if ref is not None:
                ref[:, off:off + MXU_N] = y.astype(BF16)
                continue
            for pair, dup_ref in ((y[:, :LANES], ka2_ref), (y[:, LANES:], va2_ref)):
                swapped = pltpu.roll(pair, HEAD_DIM, axis=1)
                dup_ref[:, :LANES] = jnp.where(low_half, pair, swapped).astype(BF16)
                dup_ref[:, LANES:] = jnp.where(low_half, swapped, pair).astype(BF16)


def _inproj_even(x, g, w_all, layer, gain, tm=1024):
    s = x.shape[0]
    row = _after_cast(lambda i: (i, 0))
    widths = (512, 256, 256, 512, 512, 512)
    return pl.pallas_call(
        _inproj_even_kernel,
        grid=(CAST_STEPS + s // tm,),
        in_specs=[pl.BlockSpec((tm, D_MODEL), row), _const_spec((1, D_MODEL)),
                  _slab_spec(w_all, layer), _const_spec((1, E_COLS))],
        out_specs=[pl.BlockSpec((tm, wd), row) for wd in widths],
        out_shape=[jax.ShapeDtypeStruct((s, wd), BF16) for wd in widths],
        scratch_shapes=[pltpu.VMEM(w_all.shape[1:], BF16)],
        compiler_params=_params("arbitrary"),
        name="inproj_even",
    )(x, g, w_all, gain)


def _window_penalty():
    absrel = np.abs(np.arange(3 * A_BLOCK)[None, :] - A_BLOCK - np.arange(A_BLOCK)[:, None])
    slopes = LOG2E * 2.0 ** (-(8.0 / A_Q_HEADS) * (np.arange(A_Q_HEADS) + 1.0))
    pen = np.where(absrel <= A_WINDOW, -slopes[:, None, None] * absrel[None], NEG)
    return jnp.asarray(pen, F32)


def _window_kernel(sink_ref, pen_ref, q_ref, kp_ref, kc_ref, kn_ref, vp_ref, vc_ref, vn_ref, o_ref, *, tq):
    i = pl.program_id(0)
    lane = lax.broadcasted_iota(jnp.int32, (1, LANES), 1)
    low_half = lane < HEAD_DIM
    kcol = lax.broadcasted_iota(jnp.int32, (1, 3 * A_BLOCK), 1)
    edge_first = jnp.where((i == 0) & (kcol < A_BLOCK), NEG, 0.0)
    edge_last = jnp.where((i == pl.num_programs(0) - 1) & (kcol >= 2 * A_BLOCK), NEG, 0.0)
    nblk = tq // A_BLOCK
    edges = {0: edge_first, nblk - 1: edge_last}
    units = [(j, b) for j in range(A_KV_HEADS) for b in range(nblk)]
    kcat, vcat = [], []
    for j in range(A_KV_HEADS):
        ks = slice(LANES * j, LANES * (j + 1))
        kcat.append(jnp.concatenate([kp_ref[tq - A_BLOCK:, ks], kc_ref[:, ks], kn_ref[:A_BLOCK, ks]], axis=0))
        vcat.append(jnp.concatenate([vp_ref[tq - A_BLOCK:, ks], vc_ref[:, ks], vn_ref[:A_BLOCK, ks]], axis=0))
    def scores_of(j, b):
        rows = slice(A_BLOCK * b, A_BLOCK * (b + 1))
        lhs = []
        for e in range(A_GROUP):
            pair = q_ref[rows, LANES * (2 * j + e // 2):LANES * (2 * j + e // 2 + 1)]
            lhs.append(jnp.where(low_half if e % 2 == 0 else ~low_half, pair, jnp.zeros_like(pair)))
        return _dot_nt(jnp.concatenate(lhs, axis=0), kcat[j][A_BLOCK * b:A_BLOCK * (b + 3)])

    def softmax_of(j, b, s_all):
        p_unit, inv_unit = [], []
        for e in range(A_GROUP):
            h = A_GROUP * j + e
            sk = sink_ref[h]
            pen = pen_ref[h] + edges[b] if b in edges else pen_ref[h]
            s = s_all[A_BLOCK * e:A_BLOCK * (e + 1)] + pen
            m = jnp.maximum(jnp.max(s, axis=-1, keepdims=True), sk)
            p = jnp.exp2(s - m)
            inv_unit.append(1.0 / (jnp.sum(p, axis=-1, keepdims=True) + jnp.exp2(sk - m)))
            p_unit.append(p.astype(BF16))
        return jnp.concatenate(p_unit, axis=0), jnp.concatenate(inv_unit, axis=0)

    def emit(j, b, p_all, inv):
        rows = slice(A_BLOCK * b, A_BLOCK * (b + 1))
        o = _dot(p_all, vcat[j][A_BLOCK * b:A_BLOCK * (b + 3)]) * inv
        for pr in range(A_GROUP // 2):
            lo = o[A_BLOCK * 2 * pr:A_BLOCK * (2 * pr + 1)]
            hi = o[A_BLOCK * (2 * pr + 1):A_BLOCK * (2 * pr + 2)]
            col = LANES * (2 * j + pr)
            o_ref[rows, col:col + LANES] = jnp.where(low_half, lo, hi).astype(BF16)

    scores = [scores_of(j, b) for j, b in units]
    probs = [softmax_of(j, b, s_all) for (j, b), s_all in zip(units, scores)]
    for (j, b), (p_all, inv) in zip(units, probs):
        emit(j, b, p_all, inv)


def _window_attn(sink, qa, ka2, va2, tq=512):
    s = qa.shape[0]
    nb = s // tq
    prev = lambda i: (jnp.maximum(i - 1, 0), 0)
    cur = lambda i: (i, 0)
    nxt = lambda i: (jnp.minimum(i + 1, nb - 1), 0)
    kv = lambda f: pl.BlockSpec((tq, 2 * LANES), f)
    return pl.pallas_call(
        functools.partial(_window_kernel, tq=tq),
        grid=(nb,),
        in_specs=[pl.BlockSpec(memory_space=pltpu.SMEM), _const_spec((A_Q_HEADS, A_BLOCK, 3 * A_BLOCK)),
                  pl.BlockSpec((tq, 512), cur), kv(prev), kv(cur), kv(nxt), kv(prev), kv(cur), kv(nxt)],
        out_specs=pl.BlockSpec((tq, 512), cur),
        out_shape=jax.ShapeDtypeStruct((s, 512), BF16),
        compiler_params=_params("parallel"),
        name="window_gqa",
    )(sink, _window_penalty(), qa, ka2, ka2, ka2, va2, va2, va2)


NA_ROWS_PER_STEP = 8
NA_TOK = NA_ROWS_PER_STEP * GRID_W
NA_KEYS = NA_KH * GRID_W
NA_ROWS_PER_GROUP = 2
NA_HALO_LO = (NA_KH // 2) * GRID_W
NA_HALO_HI = (NA_KH - 1 - NA_KH // 2) * GRID_W
NA_CAT = NA_HALO_LO + NA_TOK + NA_HALO_HI


def _na_cat_base(i):
    first_row = jnp.clip(i * NA_ROWS_PER_STEP - NA_KH // 2, 0, GRID_ROWS - NA_CAT // GRID_W)
    return first_row * GRID_W


def _na_kernel(q_ref, kcat, vcat, bias_ref, o_ref):
    i = pl.program_id(0)
    lane = lax.broadcasted_iota(jnp.int32, (1, LANES), 1)
    low_half = lane < HEAD_DIM

    npair = B_HEADS // 2

    def row_info(rr):
        r = i * NA_ROWS_PER_STEP + rr
        rstart = jnp.clip(r - NA_KH // 2, 0, GRID_ROWS - NA_KH)
        start = pl.multiple_of(rstart * GRID_W - _na_cat_base(i), GRID_W)
        return start, rstart - r + NA_KH - 1, slice(rr * GRID_W, (rr + 1) * GRID_W)

    def scores_of(rr, p):
        start, _, qrows = row_info(rr)
        cols = slice(LANES * p, LANES * (p + 1))
        qp = q_ref[qrows, cols]
        lhs = jnp.concatenate([jnp.where(low_half, qp, jnp.zeros_like(qp)),
                               jnp.where(low_half, jnp.zeros_like(qp), qp)], axis=0)
        return _dot_nt(lhs, kcat[pl.ds(start, NA_KEYS), cols])

    def softmax_of(rr, p, scores):
        _, ri0, _ = row_info(rr)
        bias = jnp.concatenate(
            [jnp.concatenate([bias_ref[2 * p + e, ri0 + 2 * m] for m in range(NA_KH // 2)], axis=1)
             for e in range(2)], axis=0)
        s = scores + bias
        pexp = jnp.exp2(s - jnp.max(s, axis=-1, keepdims=True))
        return pexp.astype(BF16), 1.0 / jnp.sum(pexp, axis=-1, keepdims=True)

    def emit(rr, p, probs, inv):
        start, _, qrows = row_info(rr)
        cols = slice(LANES * p, LANES * (p + 1))
        o = _dot(probs, vcat[pl.ds(start, NA_KEYS), cols]) * inv
        o_ref[qrows, cols] = jnp.where(low_half, o[:GRID_W], o[GRID_W:]).astype(BF16)

    for g in range(0, NA_ROWS_PER_STEP, NA_ROWS_PER_GROUP):
        units = [(rr, p) for rr in range(g, g + NA_ROWS_PER_GROUP) for p in range(npair)]
        scores = [scores_of(rr, p) for rr, p in units]
        probs = [softmax_of(rr, p, s) for (rr, p), s in zip(units, scores)]
        for (rr, p), (pr, inv) in zip(units, probs):
            emit(rr, p, pr, inv)


def _na_bias_table(rpb):
    qc = np.arange(GRID_W)[:, None]
    kc = np.arange(GRID_W)[None, :]
    cstart = np.clip(qc - NA_KW // 2, 0, GRID_W - NA_KW)
    col_valid = (kc >= cstart) & (kc < cstart + NA_KW)
    col_idx = np.clip(kc - qc + NA_KW - 1, 0, 2 * NA_KW - 2)
    n_rel = 2 * NA_KW - 1
    pair_idx = np.concatenate([col_idx, col_idx + n_rel], axis=1)
    onehot = (pair_idx.reshape(1, -1) == np.arange(2 * n_rel)[:, None]).astype(np.float32)
    pair_valid = np.concatenate([col_valid, col_valid], axis=1)
    rpb = rpb.astype(F32)
    pairs = jnp.concatenate([rpb[:, :-1], rpb[:, 1:]], axis=-1)
    n_pair = 2 * NA_KH - 2
    t = jnp.dot(pairs.reshape(B_HEADS * n_pair, 2 * n_rel), jnp.asarray(onehot),
                precision=lax.Precision.HIGHEST).reshape(B_HEADS, n_pair, GRID_W, 2 * GRID_W)
    return jnp.where(pair_valid[None, None], t, NEG)


def _na_attn(qb, kb, vb, bias):
    s = qb.shape[0]
    nb = s // NA_TOK
    blk = pl.BlockSpec((NA_TOK, 512), lambda i: (i, 0))
    halo = pl.BlockSpec((pl.Element(NA_CAT), pl.Element(512)), lambda i: (_na_cat_base(i), 0))
    return pl.pallas_call(
        _na_kernel,
        grid=(nb,),
        in_specs=[blk, halo, halo, _const_spec(bias.shape)],
        out_specs=blk,
        out_shape=jax.ShapeDtypeStruct((s, 512), BF16),
        compiler_params=_params("parallel"),
        name="neighbourhood_attn",
    )(qb, kb, vb, bias)


FF_CHUNK = 1024


def _outproj_mlp_kernel(*refs, n_mix, gated):
    t = pl.program_id(0)
    wout_f32, g_ref, w1_f32, w2_f32, o_ref, wout_ref, w1_ref, w2_ref = refs[-8:]

    @pl.when(t < CAST_STEPS)
    def _():
        for src, dst in ((wout_f32, wout_ref), (w1_f32, w1_ref), (w2_f32, w2_ref)):
            _cast_slab(t, src, dst)

    @pl.when(t >= CAST_STEPS)
    def _():
        _outproj_mlp_tile(refs[:-8], wout_ref, g_ref, w1_ref, w2_ref, o_ref, n_mix=n_mix, gated=gated)


def _outproj_mlp_tile(act_refs, wout_ref, g_ref, w1_ref, w2_ref, o_ref, *, n_mix, gated):
    x_ref = act_refs[0]
    mix_refs = act_refs[1:1 + n_mix]
    if gated:
        (y_ref,), gate_ref, gn_ref = mix_refs, act_refs[1 + n_mix], act_refs[2 + n_mix]
        proj = None
        for h in range(RET_HEADS):
            cols = slice(h * RET_DV, (h + 1) * RET_DV)
            y = y_ref[:, cols].astype(F32)
            d = y - jnp.mean(y, axis=-1, keepdims=True)
            var = jnp.mean(d * d, axis=-1, keepdims=True)
            yn = d * lax.rsqrt(var + GN_EPS) * gn_ref[:, cols]
            mix_h = (yn * gate_ref[:, cols].astype(F32)).astype(BF16)
            part = _dot(mix_h, wout_ref[cols, :])
            proj = part if proj is None else proj + part
    else:
        proj = _dot(jnp.concatenate([m_ref[...] for m_ref in mix_refs], axis=1), wout_ref[...])
    x1 = x_ref[...] + proj
    hb = _rmsnorm_rows(x1, g_ref[...]).astype(BF16)
    acc = x1
    for c in range(0, D_FF, FF_CHUNK):
        a = jnp.maximum(_dot(hb, w1_ref[:, c:c + FF_CHUNK]), 0.0)
        acc = acc + _dot((a * a).astype(BF16), w2_ref[c:c + FF_CHUNK, :])
    o_ref[...] = acc


def _outproj_mlp(x, mixes, gate_gn, wout_all, mix_layer, g, w1_all, w2_all, layer, tm=512, name="outproj_mlp"):
    s = x.shape[0]
    row = _after_cast(lambda i: (i, 0))
    gated = gate_gn is not None
    acts = tuple(mixes) + ((gate_gn[0],) if gated else ())
    in_specs = [pl.BlockSpec((tm, D_MODEL), row)]
    in_specs += [pl.BlockSpec((tm, m.shape[1]), row) for m in acts]
    in_specs += [_const_spec(gate_gn[1].shape)] if gated else []
    in_specs += [_slab_spec(wout_all, mix_layer), _const_spec(g.shape),
                 _slab_spec(w1_all, layer), _slab_spec(w2_all, layer)]
    consts = ((gate_gn[1],) if gated else ()) + (wout_all, g, w1_all, w2_all)
    return pl.pallas_call(
        functools.partial(_outproj_mlp_kernel, n_mix=len(mixes), gated=gated),
        grid=(CAST_STEPS + s // tm,),
        in_specs=in_specs,
        out_specs=pl.BlockSpec((tm, D_MODEL), row),
        out_shape=jax.ShapeDtypeStruct((s, D_MODEL), F32),
        scratch_shapes=[pltpu.VMEM(w.shape[1:], BF16) for w in (wout_all, w1_all, w2_all)],
        compiler_params=_params("arbitrary"),
        name=name,
    )(x, *acts, *consts)


O_Q, O_V, O_G = 0, 2 * D_MODEL, 2 * D_MODEL + ODD_MIX


O_K = D_MODEL


def _inproj_odd_kernel(x_ref, g_ref, w_f32, q_ref, kt_ref, v_ref, gate_ref, w_ref, wkt_ref):
    t = pl.program_id(0)

    @pl.when(t < CAST_STEPS)
    def _():
        _cast_slab(t, w_f32, w_ref)
        rows = w_f32.shape[0]
        wkt_ref[:, pl.ds(pl.multiple_of(t * rows, rows), rows)] = w_f32[:, O_K:O_K + D_MODEL].T.astype(BF16)

    @pl.when(t >= CAST_STEPS)
    def _():
        xb = _rmsnorm_rows(x_ref[...], g_ref[...]).astype(BF16)
        for off in range(0, ODD_MIX, PROJ_CHUNK):
            gate_ref[:, off:off + PROJ_CHUNK] = _silu(
                _dot(xb, w_ref[:, O_G + off:O_G + off + PROJ_CHUNK])).astype(BF16)
        for off in range(0, ODD_MIX, PROJ_CHUNK):
            v_ref[:, off:off + PROJ_CHUNK] = _dot(xb, w_ref[:, O_V + off:O_V + off + PROJ_CHUNK]).astype(BF16)
        q_ref[...] = _dot(xb, w_ref[:, O_Q:O_Q + D_MODEL]).astype(BF16)
        kt_ref[...] = (_dot_nt(wkt_ref[...], xb) * RET_DK ** -0.5).astype(BF16)


def _inproj_odd(x, g, w_all, layer, tm=1024):
    s = x.shape[0]
    row = _after_cast(lambda i: (i, 0))
    return pl.pallas_call(
        _inproj_odd_kernel,
        grid=(CAST_STEPS + s // tm,),
        in_specs=[pl.BlockSpec((tm, D_MODEL), row), _const_spec((1, D_MODEL)),
                  _slab_spec(w_all, layer)],
        out_specs=[pl.BlockSpec((tm, D_MODEL), row), pl.BlockSpec((D_MODEL, tm), _after_cast(lambda i: (0, i))),
                   pl.BlockSpec((tm, ODD_MIX), row), pl.BlockSpec((tm, ODD_MIX), row)],
        out_shape=[jax.ShapeDtypeStruct((s, D_MODEL), BF16), jax.ShapeDtypeStruct((D_MODEL, s), BF16),
                   jax.ShapeDtypeStruct((s, ODD_MIX), BF16), jax.ShapeDtypeStruct((s, ODD_MIX), BF16)],
        scratch_shapes=[pltpu.VMEM(w_all.shape[1:], BF16), pltpu.VMEM((D_MODEL, D_MODEL), BF16)],
        compiler_params=_params("arbitrary"),
        name="inproj_odd",
    )(x, g, w_all)


RET_CHUNKS_PER_STEP = 32
RET_TOK = RET_CHUNKS_PER_STEP * RET_CHUNK
RET_STEPS = SEQ // RET_TOK
RET_NCHUNK = SEQ // RET_CHUNK


def _retention_kernel(lg_ref, q_ref, kt_ref, v_ref, o_ref, rf_all, state):
    h = pl.program_id(0)
    t = pl.program_id(1)
    second = t >= RET_STEPS
    blk = jnp.where(second, 2 * RET_STEPS - 1 - t, t)
    lgf = lg_ref[0, h]
    lgb = lg_ref[1, h]
    tok_lane = lax.broadcasted_iota(jnp.int32, (1, RET_CHUNK), 1).astype(F32)
    tok_row = lax.broadcasted_iota(jnp.int32, (RET_CHUNK, 1), 0).astype(F32)
    chunk_len = jnp.full((1, 1), RET_CHUNK, F32)

    @pl.when((t == 0) | (t == RET_STEPS))
    def _():
        state[...] = jnp.zeros_like(state)

    @pl.when(jnp.logical_not(second))
    def _():
        zeta = jnp.exp(lgf * (RET_CHUNK - 1.0 - tok_lane))
        decay = jnp.exp(lgf * chunk_len)
        r = state[...]
        for c in range(RET_CHUNKS_PER_STEP):
            tok = slice(c * RET_CHUNK, (c + 1) * RET_CHUNK)
            kz = (kt_ref[:, tok].astype(F32) * zeta).astype(BF16)
            rf_all[blk * RET_CHUNKS_PER_STEP + c] = r.astype(BF16)
            r = r * decay + _dot(kz, v_ref[tok, :])
        state[...] = r

    @pl.when(second)
    def _():
        diff = (lax.broadcasted_iota(jnp.int32, (RET_CHUNK, RET_CHUNK), 0)
                - lax.broadcasted_iota(jnp.int32, (RET_CHUNK, RET_CHUNK), 1)).astype(F32)
        dsum = jnp.where(diff >= 0, jnp.exp(lgf * jnp.maximum(diff, 0.0)), jnp.exp(lgb * jnp.maximum(-diff, 0.0)))
        cross_f = jnp.exp(lgf * (tok_row + 1.0))
        cross_b = jnp.exp(lgb * (RET_CHUNK - tok_row))
        zeta = jnp.exp(lgb * tok_lane)
        decay = jnp.exp(lgb * chunk_len)
        r = state[...]
        for c in reversed(range(RET_CHUNKS_PER_STEP)):
            tok = slice(c * RET_CHUNK, (c + 1) * RET_CHUNK)
            q = q_ref[tok, :]
            kt = kt_ref[:, tok]
            v = v_ref[tok, :]
            a = (_dot(q, kt) * dsum).astype(BF16)
            qf = q.astype(F32)
            lhs = jnp.concatenate([a, (qf * cross_f).astype(BF16), (qf * cross_b).astype(BF16)], axis=1)
            rhs = jnp.concatenate([v, rf_all[blk * RET_CHUNKS_PER_STEP + c], r.astype(BF16)], axis=0)
            o_ref[tok, :] = _dot(lhs, rhs).astype(BF16)
            r = r * decay + _dot((kt.astype(F32) * zeta).astype(BF16), v)
        state[...] = r


def _retention(lg, q, kt, v):
    s = q.shape[0]

    def blk_of(t):
        return jnp.where(t >= RET_STEPS, 2 * RET_STEPS - 1 - t, t)

    def second_blk(t):
        return jnp.where(t >= RET_STEPS, 2 * RET_STEPS - 1 - t, RET_STEPS - 1)

    return pl.pallas_call(
        _retention_kernel,
        grid=(RET_HEADS, 2 * RET_STEPS),
        in_specs=[pl.BlockSpec(memory_space=pltpu.SMEM),
                  pl.BlockSpec((RET_TOK, RET_DK), lambda h, t: (second_blk(t), h)),
                  pl.BlockSpec((RET_DK, RET_TOK), lambda h, t: (h, blk_of(t))),
                  pl.BlockSpec((RET_TOK, RET_DV), lambda h, t: (blk_of(t), h))],
        out_specs=pl.BlockSpec((RET_TOK, RET_DV), lambda h, t: (second_blk(t), h)),
        out_shape=jax.ShapeDtypeStruct((s, ODD_MIX), BF16),
        scratch_shapes=[pltpu.VMEM((RET_NCHUNK, RET_DK, RET_DV), BF16), pltpu.VMEM((RET_DK, RET_DV), F32)],
        compiler_params=_params("arbitrary", "arbitrary"),
        name="retention",
    )(lg, q, kt, v)


def _even_qk_gain(qn_a, kn_a, qn_b, kn_b):
    scale = HEAD_DIM ** -0.5 * LOG2E
    gain = jnp.concatenate([jnp.tile(qn_a, A_Q_HEADS) * scale, jnp.tile(kn_a, A_KV_HEADS),
                            jnp.ones((A_KV_HEADS * HEAD_DIM,), F32),
                            jnp.tile(qn_b, B_HEADS) * scale, jnp.tile(kn_b, B_HEADS),
                            jnp.ones((B_HEADS * HEAD_DIM,), F32)]).astype(F32)
    return gain[None, :]


def kernel(x, attn_norm_e, w_in_e, q_norm_a, k_norm_a, sink_a, q_norm_b, k_norm_b, rpb_b, w_out_e,
           ret_norm_o, w_in_o, decay_fwd_o, decay_bwd_o, ret_gn_o, w_out_o,
           mlp_norm, w_mlp_in, w_mlp_out):
    b, s, d = x.shape
    xs = x.reshape(b * s, d)
    depth = mlp_norm.shape[0]
    for layer in range(depth):
        i = layer // 2
        g_mlp = mlp_norm[layer][None, :]
        if layer % 2 == 0:
            gain = _even_qk_gain(q_norm_a[i], k_norm_a[i], q_norm_b[i], k_norm_b[i])
            qa, ka2, va2, qb, kb, vb = _inproj_even(xs, attn_norm_e[i][None, :], w_in_e, i, gain)
            oa = _window_attn(sink_a[i].astype(F32) * LOG2E, qa, ka2, va2)
            ob = _na_attn(qb, kb, vb, _na_bias_table(rpb_b[i].astype(F32) * LOG2E))
            xs = _outproj_mlp(xs, (oa, ob), None, w_out_e, i, g_mlp, w_mlp_in, w_mlp_out, layer,
                              tm=1024, name="outproj_mlp_even")
        else:
            q, kt, v, gate = _inproj_odd(xs, ret_norm_o[i][None, :], w_in_o, i)
            lg = jnp.stack([jax.nn.log_sigmoid(decay_fwd_o[i].astype(F32)),
                            jax.nn.log_sigmoid(decay_bwd_o[i].astype(F32))])
            y = _retention(lg, q, kt, v)
            xs = _outproj_mlp(xs, (y,), (gate, ret_gn_o[i].astype(F32)[None, :]), w_out_o, i,
                              g_mlp, w_mlp_in, w_mlp_out, layer, name="outproj_mlp_odd")
    return xs.reshape(b, s, d)
```

```python
import functools

import numpy as np
import jax
import jax.numpy as jnp
from jax import lax
from jax.experimental import pallas as pl
from jax.experimental.pallas import tpu as pltpu

D_MODEL = 1024
SEQ = 16384
HEAD_DIM = 64
A_Q_HEADS = 8
A_KV_HEADS = 2
A_GROUP = A_Q_HEADS // A_KV_HEADS
A_WINDOW = 128
A_BLOCK = 128
B_HEADS = 8
GRID_W = 64
GRID_ROWS = SEQ // GRID_W
NA_KH = 8
NA_KW = 16
RET_HEADS = 8
RET_DK = D_MODEL // RET_HEADS
RET_DV = 2 * D_MODEL // RET_HEADS
RET_CHUNK = 256
D_FF = 4 * D_MODEL
ODD_MIX = RET_HEADS * RET_DV
RMS_EPS = 1e-6
GN_EPS = 1e-5

LANES = 128
MXU_N = 256
PROJ_CHUNK = 1024
NEG = -1e30
LOG2E = 1.4426950408889634
VMEM_LIMIT = 60 * 1024 * 1024

F32 = jnp.float32
BF16 = jnp.bfloat16

E_QA, E_KA, E_VA, E_QB, E_KB, E_VB = 0, 512, 640, 768, 1280, 1792
E_COLS = 2304


def _const_spec(shape):
    return pl.BlockSpec(shape, lambda *_: (0,) * len(shape), pipeline_mode=pl.Buffered(1))


def _params(*sem):
    return pltpu.CompilerParams(dimension_semantics=sem, vmem_limit_bytes=VMEM_LIMIT)


def _rmsnorm_rows(x, g):
    return x * lax.rsqrt(jnp.mean(x * x, axis=-1, keepdims=True) + RMS_EPS) * g


def _silu(g):
    return g * (0.5 * jnp.tanh(0.5 * g) + 0.5)


def _dot(a, b):
    return jnp.dot(a, b, preferred_element_type=F32)


CAST_STEPS = 8


def _slab_spec(w, layer):
    return pl.BlockSpec((None, w.shape[1] // CAST_STEPS, w.shape[2]),
                        lambda t: (layer, jnp.minimum(t, CAST_STEPS - 1), 0))


def _cast_slab(t, src_ref, dst_ref):
    rows = src_ref.shape[0]
    dst_ref[pl.ds(pl.multiple_of(t * rows, rows), rows), :] = src_ref[...].astype(BF16)


def _after_cast(index):
    return lambda t: index(jnp.maximum(t - CAST_STEPS, 0))


def _dot_nt(a, b):
    return lax.dot_general(a, b, (((1,), (1,)), ((), ())), preferred_element_type=F32)


def _inproj_even_kernel(x_ref, g_ref, w_f32, gain_ref, qa_ref, ka2_ref, va2_ref, qb_ref, kb_ref, vb_ref, w_ref):
    t = pl.program_id(0)

    @pl.when(t < CAST_STEPS)
    def _():
        _cast_slab(t, w_f32, w_ref)

    @pl.when(t >= CAST_STEPS)
    def _():
        xb = _rmsnorm_rows(x_ref[...], g_ref[...]).astype(BF16)
        r = lax.broadcasted_iota(jnp.int32, (MXU_N, MXU_N), 0) // HEAD_DIM
        c = lax.broadcasted_iota(jnp.int32, (MXU_N, MXU_N), 1) // HEAD_DIM
        head_mean = jnp.where(r == c, 1.0 / HEAD_DIM, 0.0).astype(BF16)
        is_ka = lax.broadcasted_iota(jnp.int32, (1, MXU_N), 1) < E_VA - E_KA
        low_half = lax.broadcasted_iota(jnp.int32, (1, LANES), 1) < HEAD_DIM
        outs = ((qa_ref, E_QA, 512, "all"), (None, E_KA, 256, "ka"), (qb_ref, E_QB, 512, "all"),
                (kb_ref, E_KB, 512, "all"), (vb_ref, E_VB, 512, "none"))
        chunks = [(ref, start + off, off, normed) for ref, start, width, normed in outs
                  for off in range(0, width, MXU_N)]
        ys = [_dot(xb, w_ref[:, col:col + MXU_N]) for _, col, _, _ in chunks]
        ms = [_dot((y * y).astype(BF16), head_mean) if normed != "none" else None
              for y, (_, _, _, normed) in zip(ys, chunks)]
        for y, m, (ref, col, off, normed) in zip(ys, ms, chunks):
            if normed != "none":
                scale = lax.rsqrt(m + RMS_EPS) * gain_ref[:, col:col + MXU_N]
                y = y * (scale if normed == "all" else jnp.where(is_ka, scale, 1.0))
            if ref is not None:
                ref[:, off:off + MXU_N] = y.astype(BF16)
                continue
            for pair, dup_ref in ((y[:, :LANES], ka2_ref), (y[:, LANES:], va2_ref)):
                swapped = pltpu.roll(pair, HEAD_DIM, axis=1)
                dup_ref[:, :LANES] = jnp.where(low_half, pair, swapped).astype(BF16)
                dup_ref[:, LANES:] = jnp.where(low_half, swapped, pair).astype(BF16)


def _inproj_even(x, g, w_all, layer, gain, tm=1024):
    s = x.shape[0]
    row = _after_cast(lambda i: (i, 0))
    widths = (512, 256, 256, 512, 512, 512)
    return pl.pallas_call(
        _inproj_even_kernel,
        grid=(CAST_STEPS + s // tm,),
        in_specs=[pl.BlockSpec((tm, D_MODEL), row), _const_spec((1, D_MODEL)),
                  _slab_spec(w_all, layer), _const_spec((1, E_COLS))],
        out_specs=[pl.BlockSpec((tm, wd), row) for wd in widths],
        out_shape=[jax.ShapeDtypeStruct((s, wd), BF16) for wd in widths],
        scratch_shapes=[pltpu.VMEM(w_all.shape[1:], BF16)],
        compiler_params=_params("arbitrary"),
        name="inproj_even",
    )(x, g, w_all, gain)


def _window_penalty():
    absrel = np.abs(np.arange(3 * A_BLOCK)[None, :] - A_BLOCK - np.arange(A_BLOCK)[:, None])
    slopes = LOG2E * 2.0 ** (-(8.0 / A_Q_HEADS) * (np.arange(A_Q_HEADS) + 1.0))
    pen = np.where(absrel <= A_WINDOW, -slopes[:, None, None] * absrel[None], NEG)
    return jnp.asarray(pen, F32)


def _window_kernel(sink_ref, pen_ref, q_ref, kp_ref, kc_ref, kn_ref, vp_ref, vc_ref, vn_ref, o_ref, *, tq):
    i = pl.program_id(0)
    lane = lax.broadcasted_iota(jnp.int32, (1, LANES), 1)
    low_half = lane < HEAD_DIM
    kcol = lax.broadcasted_iota(jnp.int32, (1, 3 * A_BLOCK), 1)
    edge_first = jnp.where((i == 0) & (kcol < A_BLOCK), NEG, 0.0)
    edge_last = jnp.where((i == pl.num_programs(0) - 1) & (kcol >= 2 * A_BLOCK), NEG, 0.0)
    nblk = tq // A_BLOCK
    edges = {0: edge_first, nblk - 1: edge_last}
    units = [(j, b) for j in range(A_KV_HEADS) for b in range(nblk)]
    kcat, vcat = [], []
    for j in range(A_KV_HEADS):
        ks = slice(LANES * j, LANES * (j + 1))
        kcat.append(jnp.concatenate([kp_ref[tq - A_BLOCK:, ks], kc_ref[:, ks], kn_ref[:A_BLOCK, ks]], axis=0))
        vcat.append(jnp.concatenate([vp_ref[tq - A_BLOCK:, ks], vc_ref[:, ks], vn_ref[:A_BLOCK, ks]], axis=0))
    def scores_of(j, b):
        rows = slice(A_BLOCK * b, A_BLOCK * (b + 1))
        lhs = []
        for e in range(A_GROUP):
            pair = q_ref[rows, LANES * (2 * j + e // 2):LANES * (2 * j + e // 2 + 1)]
            lhs.append(jnp.where(low_half if e % 2 == 0 else ~low_half, pair, jnp.zeros_like(pair)))
        return _dot_nt(jnp.concatenate(lhs, axis=0), kcat[j][A_BLOCK * b:A_BLOCK * (b + 3)])

    def softmax_of(j, b, s_all):
        p_unit, inv_unit = [], []
        for e in range(A_GROUP):
            h = A_GROUP * j + e
            sk = sink_ref[h]
            pen = pen_ref[h] + edges[b] if b in edges else pen_ref[h]
            s = s_all[A_BLOCK * e:A_BLOCK * (e + 1)] + pen
            m = jnp.maximum(jnp.max(s, axis=-1, keepdims=True), sk)
            p = jnp.exp2(s - m)
            inv_unit.append(1.0 / (jnp.sum(p, axis=-1, keepdims=True) + jnp.exp2(sk - m)))
            p_unit.append(p.astype(BF16))
        return jnp.concatenate(p_unit, axis=0), jnp.concatenate(inv_unit, axis=0)

    def emit(j, b, p_all, inv):
        rows = slice(A_BLOCK * b, A_BLOCK * (b + 1))
        o = _dot(p_all, vcat[j][A_BLOCK * b:A_BLOCK * (b + 3)]) * inv
        for pr in range(A_GROUP // 2):
            lo = o[A_BLOCK * 2 * pr:A_BLOCK * (2 * pr + 1)]
            hi = o[A_BLOCK * (2 * pr + 1):A_BLOCK * (2 * pr + 2)]
            col = LANES * (2 * j + pr)
            o_ref[rows, col:col + LANES] = jnp.where(low_half, lo, hi).astype(BF16)

    scores = [scores_of(j, b) for j, b in units]
    probs = [softmax_of(j, b, s_all) for (j, b), s_all in zip(units, scores)]
    for (j, b), (p_all, inv) in zip(units, probs):
        emit(j, b, p_all, inv)


def _window_attn(sink, qa, ka2, va2, tq=512):
    s = qa.shape[0]
    nb = s // tq
    prev = lambda i: (jnp.maximum(i - 1, 0), 0)
    cur = lambda i: (i, 0)
    nxt = lambda i: (jnp.minimum(i + 1, nb - 1), 0)
    kv = lambda f: pl.BlockSpec((tq, 2 * LANES), f)
    return pl.pallas_call(
        functools.partial(_window_kernel, tq=tq),
        grid=(nb,),
        in_specs=[pl.BlockSpec(memory_space=pltpu.SMEM), _const_spec((A_Q_HEADS, A_BLOCK, 3 * A_BLOCK)),
                  pl.BlockSpec((tq, 512), cur), kv(prev), kv(cur), kv(nxt), kv(prev), kv(cur), kv(nxt)],
        out_specs=pl.BlockSpec((tq, 512), cur),
        out_shape=jax.ShapeDtypeStruct((s, 512), BF16),
        compiler_params=_params("parallel"),
        name="window_gqa",
    )(sink, _window_penalty(), qa, ka2, ka2, ka2, va2, va2, va2)


NA_ROWS_PER_STEP = 8
NA_TOK = NA_ROWS_PER_STEP * GRID_W
NA_KEYS = NA_KH * GRID_W
NA_ROWS_PER_GROUP = 2
NA_HALO_LO = (NA_KH // 2) * GRID_W
NA_HALO_HI = (NA_KH - 1 - NA_KH // 2) * GRID_W
NA_CAT = NA_HALO_LO + NA_TOK + NA_HALO_HI


def _na_cat_base(i):
    first_row = jnp.clip(i * NA_ROWS_PER_STEP - NA_KH // 2, 0, GRID_ROWS - NA_CAT // GRID_W)
    return first_row * GRID_W


def _na_kernel(q_ref, kcat, vcat, bias_ref, o_ref):
    i = pl.program_id(0)
    lane = lax.broadcasted_iota(jnp.int32, (1, LANES), 1)
    low_half = lane < HEAD_DIM

    npair = B_HEADS // 2

    def row_info(rr):
        r = i * NA_ROWS_PER_STEP + rr
        rstart = jnp.clip(r - NA_KH // 2, 0, GRID_ROWS - NA_KH)
        start = pl.multiple_of(rstart * GRID_W - _na_cat_base(i), GRID_W)
        return start, rstart - r + NA_KH - 1, slice(rr * GRID_W, (rr + 1) * GRID_W)

    def scores_of(rr, p):
        start, _, qrows = row_info(rr)
        cols = slice(LANES * p, LANES * (p + 1))
        qp = q_ref[qrows, cols]
        lhs = jnp.concatenate([jnp.where(low_half, qp, jnp.zeros_like(qp)),
                               jnp.where(low_half, jnp.zeros_like(qp), qp)], axis=0)
        return _dot_nt(lhs, kcat[pl.ds(start, NA_KEYS), cols])

    def softmax_of(rr, p, scores):
        _, ri0, _ = row_info(rr)
        bias = jnp.concatenate(
            [jnp.concatenate([bias_ref[2 * p + e, ri0 + 2 * m] for m in range(NA_KH // 2)], axis=1)
             for e in range(2)], axis=0)
        s = scores + bias
        pexp = jnp.exp2(s - jnp.max(s, axis=-1, keepdims=True))
        return pexp.astype(BF16), 1.0 / jnp.sum(pexp, axis=-1, keepdims=True)

    def emit(rr, p, probs, inv):
        start, _, qrows = row_info(rr)
        cols = slice(LANES * p, LANES * (p + 1))
        o = _dot(probs, vcat[pl.ds(start, NA_KEYS), cols]) * inv
        o_ref[qrows, cols] = jnp.where(low_half, o[:GRID_W], o[GRID_W:]).astype(BF16)

    for g in range(0, NA_ROWS_PER_STEP, NA_ROWS_PER_GROUP):
        units = [(rr, p) for rr in range(g, g + NA_ROWS_PER_GROUP) for p in range(npair)]
        scores = [scores_of(rr, p) for rr, p in units]
        probs = [softmax_of(rr, p, s) for (rr, p), s in zip(units, scores)]
        for (rr, p), (pr, inv) in zip(units, probs):
            emit(rr, p, pr, inv)


def _na_bias_table(rpb):
    qc = np.arange(GRID_W)[:, None]
    kc = np.arange(GRID_W)[None, :]
    cstart = np.clip(qc - NA_KW // 2, 0, GRID_W - NA_KW)
    col_valid = (kc >= cstart) & (kc < cstart + NA_KW)
    col_idx = np.clip(kc - qc + NA_KW - 1, 0, 2 * NA_KW - 2)
    n_rel = 2 * NA_KW - 1
    pair_idx = np.concatenate([col_idx, col_idx + n_rel], axis=1)
    onehot = (pair_idx.reshape(1, -1) == np.arange(2 * n_rel)[:, None]).astype(np.float32)
    pair_valid = np.concatenate([col_valid, col_valid], axis=1)
    rpb = rpb.astype(F32)
    pairs = jnp.concatenate([rpb[:, :-1], rpb[:, 1:]], axis=-1)
    n_pair = 2 * NA_KH - 2
    t = jnp.dot(pairs.reshape(B_HEADS * n_pair, 2 * n_rel), jnp.asarray(onehot),
                precision=lax.Precision.HIGHEST).reshape(B_HEADS, n_pair, GRID_W, 2 * GRID_W)
    return jnp.where(pair_valid[None, None], t, NEG)


def _na_attn(qb, kb, vb, bias):
    s = qb.shape[0]
    nb = s // NA_TOK
    blk = pl.BlockSpec((NA_TOK, 512), lambda i: (i, 0))
    halo = pl.BlockSpec((pl.Element(NA_CAT), pl.Element(512)), lambda i: (_na_cat_base(i), 0))
    return pl.pallas_call(
        _na_kernel,
        grid=(nb,),
        in_specs=[blk, halo, halo, _const_spec(bias.shape)],
        out_specs=blk,
        out_shape=jax.ShapeDtypeStruct((s, 512), BF16),
        compiler_params=_params("parallel"),
        name="neighbourhood_attn",
    )(qb, kb, vb, bias)


FF_CHUNK = 1024


def _outproj_mlp_kernel(*refs, n_mix, gated):
    t = pl.program_id(0)
    wout_f32, g_ref, w1_f32, w2_f32, o_ref, wout_ref, w1_ref, w2_ref = refs[-8:]

    @pl.when(t < CAST_STEPS)
    def _():
        for src, dst in ((wout_f32, wout_ref), (w1_f32, w1_ref), (w2_f32, w2_ref)):
            _cast_slab(t, src, dst)

    @pl.when(t >= CAST_STEPS)
    def _():
        _outproj_mlp_tile(refs[:-8], wout_ref, g_ref, w1_ref, w2_ref, o_ref, n_mix=n_mix, gated=gated)


def _outproj_mlp_tile(act_refs, wout_ref, g_ref, w1_ref, w2_ref, o_ref, *, n_mix, gated):
    x_ref = act_refs[0]
    mix_refs = act_refs[1:1 + n_mix]
    if gated:
        (y_ref,), gate_ref, gn_ref = mix_refs, act_refs[1 + n_mix], act_refs[2 + n_mix]
        proj = None
        for h in range(RET_HEADS):
            cols = slice(h * RET_DV, (h + 1) * RET_DV)
            y = y_ref[:, cols].astype(F32)
            d = y - jnp.mean(y, axis=-1, keepdims=True)
            var = jnp.mean(d * d, axis=-1, keepdims=True)
            yn = d * lax.rsqrt(var + GN_EPS) * gn_ref[:, cols]
            mix_h = (yn * gate_ref[:, cols].astype(F32)).astype(BF16)
            part = _dot(mix_h, wout_ref[cols, :])
            proj = part if proj is None else proj + part
    else:
        proj = _dot(jnp.concatenate([m_ref[...] for m_ref in mix_refs], axis=1), wout_ref[...])
    x1 = x_ref[...] + proj
    hb = _rmsnorm_rows(x1, g_ref[...]).astype(BF16)
    acc = x1
    for c in range(0, D_FF, FF_CHUNK):
        a = jnp.maximum(_dot(hb, w1_ref[:, c:c + FF_CHUNK]), 0.0)
        acc = acc + _dot((a * a).astype(BF16), w2_ref[c:c + FF_CHUNK, :])
    o_ref[...] = acc


def _outproj_mlp(x, mixes, gate_gn, wout_all, mix_layer, g, w1_all, w2_all, layer, tm=512, name="outproj_mlp"):
    s = x.shape[0]
    row = _after_cast(lambda i: (i, 0))
    gated = gate_gn is not None
    acts = tuple(mixes) + ((gate_gn[0],) if gated else ())
    in_specs = [pl.BlockSpec((tm, D_MODEL), row)]
    in_specs += [pl.BlockSpec((tm, m.shape[1]), row) for m in acts]
    in_specs += [_const_spec(gate_gn[1].shape)] if gated else []
    in_specs += [_slab_spec(wout_all, mix_layer), _const_spec(g.shape),
                 _slab_spec(w1_all, layer), _slab_spec(w2_all, layer)]
    consts = ((gate_gn[1],) if gated else ()) + (wout_all, g, w1_all, w2_all)
    return pl.pallas_call(
        functools.partial(_outproj_mlp_kernel, n_mix=len(mixes), gated=gated),
        grid=(CAST_STEPS + s // tm,),
        in_specs=in_specs,
        out_specs=pl.BlockSpec((tm, D_MODEL), row),
        out_shape=jax.ShapeDtypeStruct((s, D_MODEL), F32),
        scratch_shapes=[pltpu.VMEM(w.shape[1:], BF16) for w in (wout_all, w1_all, w2_all)],
        compiler_params=_params("arbitrary"),
        name=name,
    )(x, *acts, *consts)


O_Q, O_K, O_V, O_G = 0, D_MODEL, 2 * D_MODEL, 2 * D_MODEL + ODD_MIX


def _inproj_odd_kernel(lgf_ref, x_ref, g_ref, w_f32, q_ref, kt_ref, v_ref, gate_ref, rf_ref, w_ref, wkt_ref, state):
    t = pl.program_id(0)

    @pl.when(t < CAST_STEPS)
    def _():
        _cast_slab(t, w_f32, w_ref)
        rows = w_f32.shape[0]
        wkt_ref[:, pl.ds(pl.multiple_of(t * rows, rows), rows)] = w_f32[:, O_K:O_K + D_MODEL].T.astype(BF16)

    @pl.when(t == CAST_STEPS)
    def _():
        state[...] = jnp.zeros_like(state)

    @pl.when(t >= CAST_STEPS)
    def _():
        xb = _rmsnorm_rows(x_ref[...], g_ref[...]).astype(BF16)
        for off in range(0, ODD_MIX, PROJ_CHUNK):
            gate_ref[:, off:off + PROJ_CHUNK] = _silu(
                _dot(xb, w_ref[:, O_G + off:O_G + off + PROJ_CHUNK])).astype(BF16)
        vs = []
        for off in range(0, ODD_MIX, PROJ_CHUNK):
            vs.append(_dot(xb, w_ref[:, O_V + off:O_V + off + PROJ_CHUNK]).astype(BF16))
            v_ref[:, off:off + PROJ_CHUNK] = vs[-1]
        kt = _dot_nt(wkt_ref[...], xb) * RET_DK ** -0.5
        kt_ref[...] = kt.astype(BF16)
        tok = lax.broadcasted_iota(jnp.int32, (1, RET_CHUNK), 1).astype(F32)
        for h in range(RET_HEADS):
            lg = lgf_ref[h]
            zeta = jnp.exp(lg * (RET_CHUNK - 1.0 - tok))
            decay = jnp.exp(lg * jnp.full((1, 1), RET_CHUNK, F32))
            r = state[h]
            for c in range(x_ref.shape[0] // RET_CHUNK):
                toks = slice(c * RET_CHUNK, (c + 1) * RET_CHUNK)
                kz = (kt[h * RET_DK:(h + 1) * RET_DK, toks] * zeta).astype(BF16)
                col = h * RET_DV
                v = vs[col // PROJ_CHUNK][toks, col % PROJ_CHUNK:col % PROJ_CHUNK + RET_DV]
                rf_ref[c, h] = r.astype(BF16)
                r = r * decay + _dot(kz, v)
            state[h] = r
        q_ref[...] = _dot(xb, w_ref[:, O_Q:O_Q + D_MODEL]).astype(BF16)


def _inproj_odd(lgf, x, g, w_all, layer, tm=512):
    s = x.shape[0]
    row = _after_cast(lambda i: (i, 0))
    chunks = tm // RET_CHUNK
    return pl.pallas_call(
        _inproj_odd_kernel,
        grid=(CAST_STEPS + s // tm,),
        in_specs=[pl.BlockSpec(memory_space=pltpu.SMEM), pl.BlockSpec((tm, D_MODEL), row),
                  _const_spec((1, D_MODEL)), _slab_spec(w_all, layer)],
        out_specs=[pl.BlockSpec((tm, D_MODEL), row), pl.BlockSpec((D_MODEL, tm), _after_cast(lambda i: (0, i))),
                   pl.BlockSpec((tm, ODD_MIX), row), pl.BlockSpec((tm, ODD_MIX), row),
                   pl.BlockSpec((chunks, RET_HEADS, RET_DK, RET_DV), _after_cast(lambda i: (i, 0, 0, 0)))],
        out_shape=[jax.ShapeDtypeStruct((s, D_MODEL), BF16), jax.ShapeDtypeStruct((D_MODEL, s), BF16),
                   jax.ShapeDtypeStruct((s, ODD_MIX), BF16), jax.ShapeDtypeStruct((s, ODD_MIX), BF16),
                   jax.ShapeDtypeStruct((s // RET_CHUNK, RET_HEADS, RET_DK, RET_DV), BF16)],
        scratch_shapes=[pltpu.VMEM(w_all.shape[1:], BF16), pltpu.VMEM((D_MODEL, D_MODEL), BF16),
                        pltpu.VMEM((RET_HEADS, RET_DK, RET_DV), F32)],
        compiler_params=_params("arbitrary"),
        name="inproj_odd",
    )(lgf, x, g, w_all)


RET_CHUNKS_PER_STEP = 32
RET_TOK = RET_CHUNKS_PER_STEP * RET_CHUNK
RET_STEPS = SEQ // RET_TOK


def _retention_kernel(lg_ref, q_ref, kt_ref, v_ref, rf_ref, o_ref, state):
    h = pl.program_id(0)
    lgf = lg_ref[0, h]
    lgb = lg_ref[1, h]
    tok_lane = lax.broadcasted_iota(jnp.int32, (1, RET_CHUNK), 1).astype(F32)
    tok_row = lax.broadcasted_iota(jnp.int32, (RET_CHUNK, 1), 0).astype(F32)
    chunk_len = jnp.full((1, 1), RET_CHUNK, F32)

    @pl.when(pl.program_id(1) == 0)
    def _():
        state[...] = jnp.zeros_like(state)

    diff = (lax.broadcasted_iota(jnp.int32, (RET_CHUNK, RET_CHUNK), 0)
            - lax.broadcasted_iota(jnp.int32, (RET_CHUNK, RET_CHUNK), 1)).astype(F32)
    dsum = jnp.where(diff >= 0, jnp.exp(lgf * jnp.maximum(diff, 0.0)), jnp.exp(lgb * jnp.maximum(-diff, 0.0)))
    cross_f = jnp.exp(lgf * (tok_row + 1.0))
    cross_b = jnp.exp(lgb * (RET_CHUNK - tok_row))
    zeta = jnp.exp(lgb * tok_lane)
    decay = jnp.exp(lgb * chunk_len)
    r = state[...]
    for c in reversed(range(RET_CHUNKS_PER_STEP)):
        tok = slice(c * RET_CHUNK, (c + 1) * RET_CHUNK)
        q = q_ref[tok, :]
        kt = kt_ref[:, tok]
        v = v_ref[tok, :]
        a = (_dot(q, kt) * dsum).astype(BF16)
        qf = q.astype(F32)
        lhs = jnp.concatenate([a, (qf * cross_f).astype(BF16), (qf * cross_b).astype(BF16)], axis=1)
        rhs = jnp.concatenate([v, rf_ref[c], r.astype(BF16)], axis=0)
        o_ref[tok, :] = _dot(lhs, rhs).astype(BF16)
        r = r * decay + _dot((kt.astype(F32) * zeta).astype(BF16), v)
    state[...] = r


def _retention(lg, q, kt, v, rf):
    s = q.shape[0]
    blk = lambda t: RET_STEPS - 1 - t
    return pl.pallas_call(
        _retention_kernel,
        grid=(RET_HEADS, RET_STEPS),
        in_specs=[pl.BlockSpec(memory_space=pltpu.SMEM),
                  pl.BlockSpec((RET_TOK, RET_DK), lambda h, t: (blk(t), h)),
                  pl.BlockSpec((RET_DK, RET_TOK), lambda h, t: (h, blk(t))),
                  pl.BlockSpec((RET_TOK, RET_DV), lambda h, t: (blk(t), h)),
                  pl.BlockSpec((RET_CHUNKS_PER_STEP, None, RET_DK, RET_DV), lambda h, t: (blk(t), h, 0, 0))],
        out_specs=pl.BlockSpec((RET_TOK, RET_DV), lambda h, t: (blk(t), h)),
        out_shape=jax.ShapeDtypeStruct((s, ODD_MIX), BF16),
        scratch_shapes=[pltpu.VMEM((RET_DK, RET_DV), F32)],
        compiler_params=_params("arbitrary", "arbitrary"),
        name="retention",
    )(lg, q, kt, v, rf)


def _even_qk_gain(qn_a, kn_a, qn_b, kn_b):
    scale = HEAD_DIM ** -0.5 * LOG2E
    gain = jnp.concatenate([jnp.tile(qn_a, A_Q_HEADS) * scale, jnp.tile(kn_a, A_KV_HEADS),
                            jnp.ones((A_KV_HEADS * HEAD_DIM,), F32),
                            jnp.tile(qn_b, B_HEADS) * scale, jnp.tile(kn_b, B_HEADS),
                            jnp.ones((B_HEADS * HEAD_DIM,), F32)]).astype(F32)
    return gain[None, :]


def kernel(x, attn_norm_e, w_in_e, q_norm_a, k_norm_a, sink_a, q_norm_b, k_norm_b, rpb_b, w_out_e,
           ret_norm_o, w_in_o, decay_fwd_o, decay_bwd_o, ret_gn_o, w_out_o,
           mlp_norm, w_mlp_in, w_mlp_out):
    b, s, d = x.shape
    xs = x.reshape(b * s, d)
    depth = mlp_norm.shape[0]
    for layer in range(depth):
        i = layer // 2
        g_mlp = mlp_norm[layer][None, :]
        if layer % 2 == 0:
            gain = _even_qk_gain(q_norm_a[i], k_norm_a[i], q_norm_b[i], k_norm_b[i])
            qa, ka2, va2, qb, kb, vb = _inproj_even(xs, attn_norm_e[i][None, :], w_in_e, i, gain)
            oa = _window_attn(sink_a[i].astype(F32) * LOG2E, qa, ka2, va2)
            ob = _na_attn(qb, kb, vb, _na_bias_table(rpb_b[i].astype(F32) * LOG2E))
            xs = _outproj_mlp(xs, (oa, ob), None, w_out_e, i, g_mlp, w_mlp_in, w_mlp_out, layer,
                              tm=1024, name="outproj_mlp_even")
        else:
            lg = jnp.stack([jax.nn.log_sigmoid(decay_fwd_o[i].astype(F32)),
                            jax.nn.log_sigmoid(decay_bwd_o[i].astype(F32))])
            q, kt, v, gate, rf = _inproj_odd(lg[0], xs, ret_norm_o[i][None, :], w_in_o, i)
            y = _retention(lg, q, kt, v, rf)
            xs = _outproj_mlp(xs, (y,), (gate, ret_gn_o[i].astype(F32)[None, :]), w_out_o, i,
                              g_mlp, w_mlp_in, w_mlp_out, layer, name="outproj_mlp_odd")
    return xs.reshape(b, s, d)
```

```python
import functools

import numpy as np
import jax
import jax.numpy as jnp
from jax import lax
from jax.experimental import pallas as pl
from jax.experimental.pallas import tpu as pltpu

D_MODEL = 1024
SEQ = 16384
HEAD_DIM = 64
A_Q_HEADS = 8
A_KV_HEADS = 2
A_GROUP = A_Q_HEADS // A_KV_HEADS
A_WINDOW = 128
A_BLOCK = 128
B_HEADS = 8
GRID_W = 64
GRID_ROWS = SEQ // GRID_W
NA_KH = 8
NA_KW = 16
RET_HEADS = 8
RET_DK = D_MODEL // RET_HEADS
RET_DV = 2 * D_MODEL // RET_HEADS
RET_CHUNK = 256
D_FF = 4 * D_MODEL
ODD_MIX = RET_HEADS * RET_DV
RMS_EPS = 1e-6
GN_EPS = 1e-5

LANES = 128
MXU_N = 256
PROJ_CHUNK = 1024
NEG = -1e30
LOG2E = 1.4426950408889634
VMEM_LIMIT = 60 * 1024 * 1024

F32 = jnp.float32
BF16 = jnp.bfloat16

E_QA, E_KA, E_VA, E_QB, E_KB, E_VB = 0, 512, 640, 768, 1280, 1792
E_COLS = 2304


def _const_spec(shape):
    return pl.BlockSpec(shape, lambda *_: (0,) * len(shape), pipeline_mode=pl.Buffered(1))


def _params(*sem):
    return pltpu.CompilerParams(dimension_semantics=sem, vmem_limit_bytes=VMEM_LIMIT)


def _rmsnorm_rows(x, g):
    return x * lax.rsqrt(jnp.mean(x * x, axis=-1, keepdims=True) + RMS_EPS) * g


def _silu(g):
    return g * (0.5 * jnp.tanh(0.5 * g) + 0.5)


def _dot(a, b):
    return jnp.dot(a, b, preferred_element_type=F32)


CAST_STEPS = 8


def _slab_spec(w, layer):
    return pl.BlockSpec((None, w.shape[1] // CAST_STEPS, w.shape[2]),
                        lambda t: (layer, jnp.minimum(t, CAST_STEPS - 1), 0))


def _cast_slab(t, src_ref, dst_ref):
    rows = src_ref.shape[0]
    dst_ref[pl.ds(pl.multiple_of(t * rows, rows), rows), :] = src_ref[...].astype(BF16)


def _after_cast(index):
    return lambda t: index(jnp.maximum(t - CAST_STEPS, 0))


def _dot_nt(a, b):
    return lax.dot_general(a, b, (((1,), (1,)), ((), ())), preferred_element_type=F32)


def _inproj_even_kernel(x_ref, g_ref, w_f32, gain_ref, qa_ref, ka2_ref, va2_ref, qb_ref, kb_ref, vb_ref, w_ref):
    t = pl.program_id(0)

    @pl.when(t < CAST_STEPS)
    def _():
        _cast_slab(t, w_f32, w_ref)

    @pl.when(t >= CAST_STEPS)
    def _():
        xb = _rmsnorm_rows(x_ref[...], g_ref[...]).astype(BF16)
        r = lax.broadcasted_iota(jnp.int32, (MXU_N, MXU_N), 0) // HEAD_DIM
        c = lax.broadcasted_iota(jnp.int32, (MXU_N, MXU_N), 1) // HEAD_DIM
        head_mean = jnp.where(r == c, 1.0 / HEAD_DIM, 0.0).astype(BF16)
        is_ka = lax.broadcasted_iota(jnp.int32, (1, MXU_N), 1) < E_VA - E_KA
        low_half = lax.broadcasted_iota(jnp.int32, (1, LANES), 1) < HEAD_DIM
        outs = ((qa_ref, E_QA, 512, "all"), (None, E_KA, 256, "ka"), (qb_ref, E_QB, 512, "all"),
                (kb_ref, E_KB, 512, "all"), (vb_ref, E_VB, 512, "none"))
        chunks = [(ref, start + off, off, normed) for ref, start, width, normed in outs
                  for off in range(0, width, MXU_N)]
        ys = [_dot(xb, w_ref[:, col:col + MXU_N]) for _, col, _, _ in chunks]
        ms = [_dot((y * y).astype(BF16), head_mean) if normed != "none" else None
              for y, (_, _, _, normed) in zip(ys, chunks)]
        for y, m, (ref, col, off, normed) in zip(ys, ms, chunks):
            if normed != "none":
                scale = lax.rsqrt(m + RMS_EPS) * gain_ref[:, col:col + MXU_N]
                y = y * (scale if normed == "all" else jnp.where(is_ka, scale, 1.0))
            if ref is not None:
                ref[:, off:off + MXU_N] = y.astype(BF16)
                continue
            for pair, dup_ref in ((y[:, :LANES], ka2_ref), (y[:, LANES:], va2_ref)):
                swapped = pltpu.roll(pair, HEAD_DIM, axis=1)
                dup_ref[:, :LANES] = jnp.where(low_half, pair, swapped).astype(BF16)
                dup_ref[:, LANES:] = jnp.where(low_half, swapped, pair).astype(BF16)


def _inproj_even(x, g, w_all, layer, gain, tm=1024):
    s = x.shape[0]
    row = _after_cast(lambda i: (i, 0))
    widths = (512, 256, 256, 512, 512, 512)
    return pl.pallas_call(
        _inproj_even_kernel,
        grid=(CAST_STEPS + s // tm,),
        in_specs=[pl.BlockSpec((tm, D_MODEL), row), _const_spec((1, D_MODEL)),
                  _slab_spec(w_all, layer), _const_spec((1, E_COLS))],
        out_specs=[pl.BlockSpec((tm, wd), row) for wd in widths],
        out_shape=[jax.ShapeDtypeStruct((s, wd), BF16) for wd in widths],
        scratch_shapes=[pltpu.VMEM(w_all.shape[1:], BF16)],
        compiler_params=_params("arbitrary"),
        name="inproj_even",
    )(x, g, w_all, gain)


def _window_penalty():
    absrel = np.abs(np.arange(3 * A_BLOCK)[None, :] - A_BLOCK - np.arange(A_BLOCK)[:, None])
    slopes = LOG2E * 2.0 ** (-(8.0 / A_Q_HEADS) * (np.arange(A_Q_HEADS) + 1.0))
    pen = np.where(absrel <= A_WINDOW, -slopes[:, None, None] * absrel[None], NEG)
    return jnp.asarray(pen, F32)


def _window_kernel(sink_ref, pen_ref, q_ref, kp_ref, kc_ref, kn_ref, vp_ref, vc_ref, vn_ref, o_ref, *, tq):
    i = pl.program_id(0)
    lane = lax.broadcasted_iota(jnp.int32, (1, LANES), 1)
    low_half = lane < HEAD_DIM
    kcol = lax.broadcasted_iota(jnp.int32, (1, 3 * A_BLOCK), 1)
    edge_first = jnp.where((i == 0) & (kcol < A_BLOCK), NEG, 0.0)
    edge_last = jnp.where((i == pl.num_programs(0) - 1) & (kcol >= 2 * A_BLOCK), NEG, 0.0)
    nblk = tq // A_BLOCK
    edges = {0: edge_first, nblk - 1: edge_last}
    units = [(j, b) for j in range(A_KV_HEADS) for b in range(nblk)]
    kcat, vcat = [], []
    for j in range(A_KV_HEADS):
        ks = slice(LANES * j, LANES * (j + 1))
        kcat.append(jnp.concatenate([kp_ref[tq - A_BLOCK:, ks], kc_ref[:, ks], kn_ref[:A_BLOCK, ks]], axis=0))
        vcat.append(jnp.concatenate([vp_ref[tq - A_BLOCK:, ks], vc_ref[:, ks], vn_ref[:A_BLOCK, ks]], axis=0))
    def scores_of(j, b):
        rows = slice(A_BLOCK * b, A_BLOCK * (b + 1))
        lhs = []
        for e in range(A_GROUP):
            pair = q_ref[rows, LANES * (2 * j + e // 2):LANES * (2 * j + e // 2 + 1)]
            lhs.append(jnp.where(low_half if e % 2 == 0 else ~low_half, pair, jnp.zeros_like(pair)))
        return _dot_nt(jnp.concatenate(lhs, axis=0), kcat[j][A_BLOCK * b:A_BLOCK * (b + 3)])

    def softmax_of(j, b, s_all):
        p_unit, inv_unit = [], []
        for e in range(A_GROUP):
            h = A_GROUP * j + e
            sk = sink_ref[h]
            pen = pen_ref[h] + edges[b] if b in edges else pen_ref[h]
            s = s_all[A_BLOCK * e:A_BLOCK * (e + 1)] + pen
            m = jnp.maximum(jnp.max(s, axis=-1, keepdims=True), sk)
            p = jnp.exp2(s - m)
            inv_unit.append(1.0 / (jnp.sum(p, axis=-1, keepdims=True) + jnp.exp2(sk - m)))
            p_unit.append(p.astype(BF16))
        return jnp.concatenate(p_unit, axis=0), jnp.concatenate(inv_unit, axis=0)

    def emit(j, b, p_all, inv):
        rows = slice(A_BLOCK * b, A_BLOCK * (b + 1))
        o = _dot(p_all, vcat[j][A_BLOCK * b:A_BLOCK * (b + 3)]) * inv
        for pr in range(A_GROUP // 2):
            lo = o[A_BLOCK * 2 * pr:A_BLOCK * (2 * pr + 1)]
            hi = o[A_BLOCK * (2 * pr + 1):A_BLOCK * (2 * pr + 2)]
            col = LANES * (2 * j + pr)
            o_ref[rows, col:col + LANES] = jnp.where(low_half, lo, hi).astype(BF16)

    scores = [scores_of(j, b) for j, b in units]
    probs = [softmax_of(j, b, s_all) for (j, b), s_all in zip(units, scores)]
    for (j, b), (p_all, inv) in zip(units, probs):
        emit(j, b, p_all, inv)


def _window_attn(sink, qa, ka2, va2, tq=512):
    s = qa.shape[0]
    nb = s // tq
    prev = lambda i: (jnp.maximum(i - 1, 0), 0)
    cur = lambda i: (i, 0)
    nxt = lambda i: (jnp.minimum(i + 1, nb - 1), 0)
    kv = lambda f: pl.BlockSpec((tq, 2 * LANES), f)
    return pl.pallas_call(
        functools.partial(_window_kernel, tq=tq),
        grid=(nb,),
        in_specs=[pl.BlockSpec(memory_space=pltpu.SMEM), _const_spec((A_Q_HEADS, A_BLOCK, 3 * A_BLOCK)),
                  pl.BlockSpec((tq, 512), cur), kv(prev), kv(cur), kv(nxt), kv(prev), kv(cur), kv(nxt)],
        out_specs=pl.BlockSpec((tq, 512), cur),
        out_shape=jax.ShapeDtypeStruct((s, 512), BF16),
        compiler_params=_params("parallel"),
        name="window_gqa",
    )(sink, _window_penalty(), qa, ka2, ka2, ka2, va2, va2, va2)


NA_ROWS_PER_STEP = 16
NA_TOK = NA_ROWS_PER_STEP * GRID_W
NA_KEYS = NA_KH * GRID_W
NA_ROWS_PER_GROUP = 2
NA_HALO_LO = (NA_KH // 2) * GRID_W
NA_HALO_HI = (NA_KH - 1 - NA_KH // 2) * GRID_W
NA_CAT = NA_HALO_LO + NA_TOK + NA_HALO_HI


def _na_cat_base(i):
    first_row = jnp.clip(i * NA_ROWS_PER_STEP - NA_KH // 2, 0, GRID_ROWS - NA_CAT // GRID_W)
    return first_row * GRID_W


def _na_kernel(q_ref, kcat, vcat, rows_ref, o_ref, bias_ref):
    i = pl.program_id(0)

    @pl.when(i == 0)
    def _():
        _na_build_bias(rows_ref, bias_ref)

    lane = lax.broadcasted_iota(jnp.int32, (1, LANES), 1)
    low_half = lane < HEAD_DIM

    npair = B_HEADS // 2

    def row_info(rr):
        r = i * NA_ROWS_PER_STEP + rr
        rstart = jnp.clip(r - NA_KH // 2, 0, GRID_ROWS - NA_KH)
        start = pl.multiple_of(rstart * GRID_W - _na_cat_base(i), GRID_W)
        return start, rstart - r + NA_KH - 1, slice(rr * GRID_W, (rr + 1) * GRID_W)

    def scores_of(rr, p):
        start, _, qrows = row_info(rr)
        cols = slice(LANES * p, LANES * (p + 1))
        qp = q_ref[qrows, cols]
        lhs = jnp.concatenate([jnp.where(low_half, qp, jnp.zeros_like(qp)),
                               jnp.where(low_half, jnp.zeros_like(qp), qp)], axis=0)
        return _dot_nt(lhs, kcat[pl.ds(start, NA_KEYS), cols])

    def softmax_of(rr, p, scores):
        _, ri0, _ = row_info(rr)
        bias = jnp.concatenate(
            [jnp.concatenate([bias_ref[(2 * p + e) * NA_PAIRS + ri0 + 2 * m] for m in range(NA_KH // 2)], axis=1)
             for e in range(2)], axis=0)
        s = scores + bias
        pexp = jnp.exp2(s - jnp.max(s, axis=-1, keepdims=True))
        return pexp.astype(BF16), 1.0 / jnp.sum(pexp, axis=-1, keepdims=True)

    def emit(rr, p, probs, inv):
        start, _, qrows = row_info(rr)
        cols = slice(LANES * p, LANES * (p + 1))
        o = _dot(probs, vcat[pl.ds(start, NA_KEYS), cols]) * inv
        o_ref[qrows, cols] = jnp.where(low_half, o[:GRID_W], o[GRID_W:]).astype(BF16)

    for g in range(0, NA_ROWS_PER_STEP, NA_ROWS_PER_GROUP):
        units = [(rr, p) for rr in range(g, g + NA_ROWS_PER_GROUP) for p in range(npair)]
        scores = [scores_of(rr, p) for rr, p in units]
        probs = [softmax_of(rr, p, s) for (rr, p), s in zip(units, scores)]
        for (rr, p), (pr, inv) in zip(units, probs):
            emit(rr, p, pr, inv)


NA_PAIRS = 2 * NA_KH - 2
NA_REL = 2 * NA_KW - 1


def _na_bias_rows(rpb):
    pad = jnp.zeros(rpb.shape[:1] + (NA_PAIRS, GRID_W - NA_REL), F32)
    rows = jnp.concatenate([rpb[:, :-1], pad, rpb[:, 1:], pad], axis=-1)
    return rows.reshape(B_HEADS * NA_PAIRS, 2 * GRID_W)


def _na_build_bias(rows_ref, bias_ref):
    qc = lax.broadcasted_iota(jnp.int32, (GRID_W, 2 * GRID_W), 0)
    kc = lax.broadcasted_iota(jnp.int32, (GRID_W, 2 * GRID_W), 1) % GRID_W
    cstart = jnp.clip(qc - NA_KW // 2, 0, GRID_W - NA_KW)
    valid = (kc >= cstart) & (kc < cstart + NA_KW)

    def body(p, carry):
        row = jnp.broadcast_to(rows_ref[pl.ds(p, 1), :], (GRID_W, 2 * GRID_W))
        tile = pltpu.roll(row, 2 * GRID_W - (NA_KW - 1), 1, stride=1, stride_axis=0)
        bias_ref[p] = jnp.where(valid, tile, NEG)
        return carry

    lax.fori_loop(0, B_HEADS * NA_PAIRS, body, 0)


def _na_attn(qb, kb, vb, bias_rows):
    s = qb.shape[0]
    nb = s // NA_TOK
    blk = pl.BlockSpec((NA_TOK, 512), lambda i: (i, 0))
    halo = pl.BlockSpec((pl.Element(NA_CAT), pl.Element(512)), lambda i: (_na_cat_base(i), 0))
    return pl.pallas_call(
        _na_kernel,
        grid=(nb,),
        in_specs=[blk, halo, halo, _const_spec(bias_rows.shape)],
        out_specs=blk,
        out_shape=jax.ShapeDtypeStruct((s, 512), BF16),
        scratch_shapes=[pltpu.VMEM((B_HEADS * NA_PAIRS, GRID_W, 2 * GRID_W), F32)],
        compiler_params=_params("arbitrary"),
        name="neighbourhood_attn",
    )(qb, kb, vb, bias_rows)


FF_CHUNK = 1024


def _outproj_mlp_kernel(*refs, n_mix, gated):
    t = pl.program_id(0)
    wout_f32, g_ref, w1_f32, w2_f32, o_ref, wout_ref, w1_ref, w2_ref = refs[-8:]

    @pl.when(t < CAST_STEPS)
    def _():
        for src, dst in ((wout_f32, wout_ref), (w1_f32, w1_ref), (w2_f32, w2_ref)):
            _cast_slab(t, src, dst)

    @pl.when(t >= CAST_STEPS)
    def _():
        _outproj_mlp_tile(refs[:-8], wout_ref, g_ref, w1_ref, w2_ref, o_ref, n_mix=n_mix, gated=gated)


def _outproj_mlp_tile(act_refs, wout_ref, g_ref, w1_ref, w2_ref, o_ref, *, n_mix, gated):
    x_ref = act_refs[0]
    mix_refs = act_refs[1:1 + n_mix]
    if gated:
        (y_ref,), gate_ref, gn_ref = mix_refs, act_refs[1 + n_mix], act_refs[2 + n_mix]
        proj = None
        for h in range(RET_HEADS):
            cols = slice(h * RET_DV, (h + 1) * RET_DV)
            y = y_ref[:, cols].astype(F32)
            d = y - jnp.mean(y, axis=-1, keepdims=True)
            var = jnp.mean(d * d, axis=-1, keepdims=True)
            yn = d * lax.rsqrt(var + GN_EPS) * gn_ref[:, cols]
            mix_h = (yn * gate_ref[:, cols].astype(F32)).astype(BF16)
            part = _dot(mix_h, wout_ref[cols, :])
            proj = part if proj is None else proj + part
    else:
        proj = _dot(jnp.concatenate([m_ref[...] for m_ref in mix_refs], axis=1), wout_ref[...])
    x1 = x_ref[...] + proj
    hb = _rmsnorm_rows(x1, g_ref[...]).astype(BF16)
    acc = x1
    for c in range(0, D_FF, FF_CHUNK):
        a = jnp.maximum(_dot(hb, w1_ref[:, c:c + FF_CHUNK]), 0.0)
        acc = acc + _dot((a * a).astype(BF16), w2_ref[c:c + FF_CHUNK, :])
    o_ref[...] = acc


def _outproj_mlp(x, mixes, gate_gn, wout_all, mix_layer, g, w1_all, w2_all, layer, tm=512, name="outproj_mlp"):
    s = x.shape[0]
    row = _after_cast(lambda i: (i, 0))
    gated = gate_gn is not None
    acts = tuple(mixes) + ((gate_gn[0],) if gated else ())
    in_specs = [pl.BlockSpec((tm, D_MODEL), row)]
    in_specs += [pl.BlockSpec((tm, m.shape[1]), row) for m in acts]
    in_specs += [_const_spec(gate_gn[1].shape)] if gated else []
    in_specs += [_slab_spec(wout_all, mix_layer), _const_spec(g.shape),
                 _slab_spec(w1_all, layer), _slab_spec(w2_all, layer)]
    consts = ((gate_gn[1],) if gated else ()) + (wout_all, g, w1_all, w2_all)
    return pl.pallas_call(
        functools.partial(_outproj_mlp_kernel, n_mix=len(mixes), gated=gated),
        grid=(CAST_STEPS + s // tm,),
        in_specs=in_specs,
        out_specs=pl.BlockSpec((tm, D_MODEL), row),
        out_shape=jax.ShapeDtypeStruct((s, D_MODEL), F32),
        scratch_shapes=[pltpu.VMEM(w.shape[1:], BF16) for w in (wout_all, w1_all, w2_all)],
        compiler_params=_params("arbitrary"),
        name=name,
    )(x, *acts, *consts)


O_Q, O_K, O_V, O_G = 0, D_MODEL, 2 * D_MODEL, 2 * D_MODEL + ODD_MIX


def _inproj_odd_kernel(lgf_ref, x_ref, g_ref, w_f32, q_ref, kt_ref, v_ref, gate_ref, rf_ref, w_ref, wkt_ref, state):
    t = pl.program_id(0)

    @pl.when(t < CAST_STEPS)
    def _():
        _cast_slab(t, w_f32, w_ref)
        rows = w_f32.shape[0]
        wkt_ref[:, pl.ds(pl.multiple_of(t * rows, rows), rows)] = w_f32[:, O_K:O_K + D_MODEL].T.astype(BF16)

    @pl.when(t == CAST_STEPS)
    def _():
        state[...] = jnp.zeros_like(state)

    @pl.when(t >= CAST_STEPS)
    def _():
        xb = _rmsnorm_rows(x_ref[...], g_ref[...]).astype(BF16)
        for off in range(0, ODD_MIX, PROJ_CHUNK):
            gate_ref[:, off:off + PROJ_CHUNK] = _silu(
                _dot(xb, w_ref[:, O_G + off:O_G + off + PROJ_CHUNK])).astype(BF16)
        vs = []
        for off in range(0, ODD_MIX, PROJ_CHUNK):
            vs.append(_dot(xb, w_ref[:, O_V + off:O_V + off + PROJ_CHUNK]).astype(BF16))
            v_ref[:, off:off + PROJ_CHUNK] = vs[-1]
        kt = _dot_nt(wkt_ref[...], xb) * RET_DK ** -0.5
        kt_ref[...] = kt.astype(BF16)
        tok = lax.broadcasted_iota(jnp.int32, (1, RET_CHUNK), 1).astype(F32)
        for h in range(RET_HEADS):
            lg = lgf_ref[h]
            zeta = jnp.exp(lg * (RET_CHUNK - 1.0 - tok))
            decay = jnp.exp(lg * jnp.full((1, 1), RET_CHUNK, F32))
            r = state[h]
            for c in range(x_ref.shape[0] // RET_CHUNK):
                toks = slice(c * RET_CHUNK, (c + 1) * RET_CHUNK)
                kz = (kt[h * RET_DK:(h + 1) * RET_DK, toks] * zeta).astype(BF16)
                col = h * RET_DV
                v = vs[col // PROJ_CHUNK][toks, col % PROJ_CHUNK:col % PROJ_CHUNK + RET_DV]
                rf_ref[c, h] = r.astype(BF16)
                r = r * decay + _dot(kz, v)
            state[h] = r
        q_ref[...] = _dot(xb, w_ref[:, O_Q:O_Q + D_MODEL]).astype(BF16)


def _inproj_odd(lgf, x, g, w_all, layer, tm=512):
    s = x.shape[0]
    row = _after_cast(lambda i: (i, 0))
    chunks = tm // RET_CHUNK
    return pl.pallas_call(
        _inproj_odd_kernel,
        grid=(CAST_STEPS + s // tm,),
        in_specs=[pl.BlockSpec(memory_space=pltpu.SMEM), pl.BlockSpec((tm, D_MODEL), row),
                  _const_spec((1, D_MODEL)), _slab_spec(w_all, layer)],
        out_specs=[pl.BlockSpec((tm, D_MODEL), row), pl.BlockSpec((D_MODEL, tm), _after_cast(lambda i: (0, i))),
                   pl.BlockSpec((tm, ODD_MIX), row), pl.BlockSpec((tm, ODD_MIX), row),
                   pl.BlockSpec((chunks, RET_HEADS, RET_DK, RET_DV), _after_cast(lambda i: (i, 0, 0, 0)))],
        out_shape=[jax.ShapeDtypeStruct((s, D_MODEL), BF16), jax.ShapeDtypeStruct((D_MODEL, s), BF16),
                   jax.ShapeDtypeStruct((s, ODD_MIX), BF16), jax.ShapeDtypeStruct((s, ODD_MIX), BF16),
                   jax.ShapeDtypeStruct((s // RET_CHUNK, RET_HEADS, RET_DK, RET_DV), BF16)],
        scratch_shapes=[pltpu.VMEM(w_all.shape[1:], BF16), pltpu.VMEM((D_MODEL, D_MODEL), BF16),
                        pltpu.VMEM((RET_HEADS, RET_DK, RET_DV), F32)],
        compiler_params=_params("arbitrary"),
        name="inproj_odd",
    )(lgf, x, g, w_all)


RET_CHUNKS_PER_STEP = 32
RET_TOK = RET_CHUNKS_PER_STEP * RET_CHUNK
RET_STEPS = SEQ // RET_TOK


def _retention_kernel(lg_ref, q_ref, kt_ref, v_ref, rf_ref, o_ref, state):
    h = pl.program_id(0)
    lgf = lg_ref[0, h]
    lgb = lg_ref[1, h]
    tok_lane = lax.broadcasted_iota(jnp.int32, (1, RET_CHUNK), 1).astype(F32)
    tok_row = lax.broadcasted_iota(jnp.int32, (RET_CHUNK, 1), 0).astype(F32)
    chunk_len = jnp.full((1, 1), RET_CHUNK, F32)

    @pl.when(pl.program_id(1) == 0)
    def _():
        state[...] = jnp.zeros_like(state)

    diff = (lax.broadcasted_iota(jnp.int32, (RET_CHUNK, RET_CHUNK), 0)
            - lax.broadcasted_iota(jnp.int32, (RET_CHUNK, RET_CHUNK), 1)).astype(F32)
    dsum = jnp.where(diff >= 0, jnp.exp(lgf * jnp.maximum(diff, 0.0)), jnp.exp(lgb * jnp.maximum(-diff, 0.0)))
    cross_f = jnp.exp(lgf * (tok_row + 1.0))
    cross_b = jnp.exp(lgb * (RET_CHUNK - tok_row))
    zeta = jnp.exp(lgb * tok_lane)
    decay = jnp.exp(lgb * chunk_len)
    r = state[...]
    for c in reversed(range(RET_CHUNKS_PER_STEP)):
        tok = slice(c * RET_CHUNK, (c + 1) * RET_CHUNK)
        q = q_ref[tok, :]
        kt = kt_ref[:, tok]
        v = v_ref[tok, :]
        a = (_dot(q, kt) * dsum).astype(BF16)
        qf = q.astype(F32)
        lhs = jnp.concatenate([a, (qf * cross_f).astype(BF16), (qf * cross_b).astype(BF16)], axis=1)
        rhs = jnp.concatenate([v, rf_ref[c], r.astype(BF16)], axis=0)
        o_ref[tok, :] = _dot(lhs, rhs).astype(BF16)
        r = r * decay + _dot((kt.astype(F32) * zeta).astype(BF16), v)
    state[...] = r


def _retention(lg, q, kt, v, rf):
    s = q.shape[0]
    blk = lambda t: RET_STEPS - 1 - t
    return pl.pallas_call(
        _retention_kernel,
        grid=(RET_HEADS, RET_STEPS),
        in_specs=[pl.BlockSpec(memory_space=pltpu.SMEM),
                  pl.BlockSpec((RET_TOK, RET_DK), lambda h, t: (blk(t), h)),
                  pl.BlockSpec((RET_DK, RET_TOK), lambda h, t: (h, blk(t))),
                  pl.BlockSpec((RET_TOK, RET_DV), lambda h, t: (blk(t), h)),
                  pl.BlockSpec((RET_CHUNKS_PER_STEP, None, RET_DK, RET_DV), lambda h, t: (blk(t), h, 0, 0))],
        out_specs=pl.BlockSpec((RET_TOK, RET_DV), lambda h, t: (blk(t), h)),
        out_shape=jax.ShapeDtypeStruct((s, ODD_MIX), BF16),
        scratch_shapes=[pltpu.VMEM((RET_DK, RET_DV), F32)],
        compiler_params=_params("arbitrary", "arbitrary"),
        name="retention",
    )(lg, q, kt, v, rf)


def _even_qk_gain(qn_a, kn_a, qn_b, kn_b):
    scale = HEAD_DIM ** -0.5 * LOG2E
    gain = jnp.concatenate([jnp.tile(qn_a, A_Q_HEADS) * scale, jnp.tile(kn_a, A_KV_HEADS),
                            jnp.ones((A_KV_HEADS * HEAD_DIM,), F32),
                            jnp.tile(qn_b, B_HEADS) * scale, jnp.tile(kn_b, B_HEADS),
                            jnp.ones((B_HEADS * HEAD_DIM,), F32)]).astype(F32)
    return gain[None, :]


def kernel(x, attn_norm_e, w_in_e, q_norm_a, k_norm_a, sink_a, q_norm_b, k_norm_b, rpb_b, w_out_e,
           ret_norm_o, w_in_o, decay_fwd_o, decay_bwd_o, ret_gn_o, w_out_o,
           mlp_norm, w_mlp_in, w_mlp_out):
    b, s, d = x.shape
    xs = x.reshape(b * s, d)
    depth = mlp_norm.shape[0]
    for layer in range(depth):
        i = layer // 2
        g_mlp = mlp_norm[layer][None, :]
        if layer % 2 == 0:
            gain = _even_qk_gain(q_norm_a[i], k_norm_a[i], q_norm_b[i], k_norm_b[i])
            qa, ka2, va2, qb, kb, vb = _inproj_even(xs, attn_norm_e[i][None, :], w_in_e, i, gain)
            oa = _window_attn(sink_a[i].astype(F32) * LOG2E, qa, ka2, va2)
            ob = _na_attn(qb, kb, vb, _na_bias_rows(rpb_b[i].astype(F32) * LOG2E))
            xs = _outproj_mlp(xs, (oa, ob), None, w_out_e, i, g_mlp, w_mlp_in, w_mlp_out, layer,
                              tm=1024, name="outproj_mlp_even")
        else:
            lg = jnp.stack([jax.nn.log_sigmoid(decay_fwd_o[i].astype(F32)),
                            jax.nn.log_sigmoid(decay_bwd_o[i].astype(F32))])
            q, kt, v, gate, rf = _inproj_odd(lg[0], xs, ret_norm_o[i][None, :], w_in_o, i)
            y = _retention(lg, q, kt, v, rf)
            xs = _outproj_mlp(xs, (y,), (gate, ret_gn_o[i].astype(F32)[None, :]), w_out_o, i,
                              g_mlp, w_mlp_in, w_mlp_out, layer, name="outproj_mlp_odd")
    return xs.reshape(b, s, d)
```

```python
import functools

import numpy as np
import jax
import jax.numpy as jnp
from jax import lax
from jax.experimental import pallas as pl
from jax.experimental.pallas import tpu as pltpu

D_MODEL = 1024
SEQ = 16384
HEAD_DIM = 64
A_Q_HEADS = 8
A_KV_HEADS = 2
A_GROUP = A_Q_HEADS // A_KV_HEADS
A_WINDOW = 128
A_BLOCK = 128
B_HEADS = 8
GRID_W = 64
GRID_ROWS = SEQ // GRID_W
NA_KH = 8
NA_KW = 16
RET_HEADS = 8
RET_DK = D_MODEL // RET_HEADS
RET_DV = 2 * D_MODEL // RET_HEADS
RET_CHUNK = 256
D_FF = 4 * D_MODEL
ODD_MIX = RET_HEADS * RET_DV
RMS_EPS = 1e-6
GN_EPS = 1e-5

LANES = 128
MXU_N = 256
PROJ_CHUNK = 1024
NEG = -1e30
LOG2E = 1.4426950408889634
VMEM_LIMIT = 60 * 1024 * 1024

F32 = jnp.float32
BF16 = jnp.bfloat16

E_QA, E_KA, E_VA, E_QB, E_KB, E_VB = 0, 512, 640, 768, 1280, 1792
E_COLS = 2304


def _const_spec(shape):
    return pl.BlockSpec(shape, lambda *_: (0,) * len(shape), pipeline_mode=pl.Buffered(1))


def _params(*sem):
    return pltpu.CompilerParams(dimension_semantics=sem, vmem_limit_bytes=VMEM_LIMIT)


def _rmsnorm_rows(x, g):
    return x * lax.rsqrt(jnp.mean(x * x, axis=-1, keepdims=True) + RMS_EPS) * g


def _silu(g):
    return g * (0.5 * jnp.tanh(0.5 * g) + 0.5)


def _dot(a, b):
    return jnp.dot(a, b, preferred_element_type=F32)


CAST_STEPS = 8


def _slab_spec(w, layer):
    return pl.BlockSpec((None, w.shape[1] // CAST_STEPS, w.shape[2]),
                        lambda t: (layer, jnp.minimum(t, CAST_STEPS - 1), 0))


def _cast_slab(t, src_ref, dst_ref):
    rows = src_ref.shape[0]
    dst_ref[pl.ds(pl.multiple_of(t * rows, rows), rows), :] = src_ref[...].astype(BF16)


def _after_cast(index):
    return lambda t: index(jnp.maximum(t - CAST_STEPS, 0))


def _dot_nt(a, b):
    return lax.dot_general(a, b, (((1,), (1,)), ((), ())), preferred_element_type=F32)


def _inproj_even_kernel(x_ref, g_ref, w_f32, gain_ref, qa_ref, ka2_ref, va2_ref, qb_ref, kb_ref, vb_ref, w_ref):
    t = pl.program_id(0)

    @pl.when(t < CAST_STEPS)
    def _():
        _cast_slab(t, w_f32, w_ref)

    @pl.when(t >= CAST_STEPS)
    def _():
        xb = _rmsnorm_rows(x_ref[...], g_ref[...]).astype(BF16)
        r = lax.broadcasted_iota(jnp.int32, (MXU_N, MXU_N), 0) // HEAD_DIM
        c = lax.broadcasted_iota(jnp.int32, (MXU_N, MXU_N), 1) // HEAD_DIM
        head_mean = jnp.where(r == c, 1.0 / HEAD_DIM, 0.0).astype(BF16)
        is_ka = lax.broadcasted_iota(jnp.int32, (1, MXU_N), 1) < E_VA - E_KA
        low_half = lax.broadcasted_iota(jnp.int32, (1, LANES), 1) < HEAD_DIM
        outs = ((qa_ref, E_QA, 512, "all"), (None, E_KA, 256, "ka"), (qb_ref, E_QB, 512, "all"),
                (kb_ref, E_KB, 512, "all"), (vb_ref, E_VB, 512, "none"))
        chunks = [(ref, start + off, off, normed) for ref, start, width, normed in outs
                  for off in range(0, width, MXU_N)]
        ys = [_dot(xb, w_ref[:, col:col + MXU_N]) for _, col, _, _ in chunks]
        ms = [_dot((y * y).astype(BF16), head_mean) if normed != "none" else None
              for y, (_, _, _, normed) in zip(ys, chunks)]
        for y, m, (ref, col, off, normed) in zip(ys, ms, chunks):
            if normed != "none":
                scale = lax.rsqrt(m + RMS_EPS) * gain_ref[:, col:col + MXU_N]
                y = y * (scale if normed == "all" else jnp.where(is_ka, scale, 1.0))
            if ref is not None:
                ref[:, off:off + MXU_N] = y.astype(BF16)
                continue
            for pair, dup_ref in ((y[:, :LANES], ka2_ref), (y[:, LANES:], va2_ref)):
                swapped = pltpu.roll(pair, HEAD_DIM, axis=1)
                dup_ref[:, :LANES] = jnp.where(low_half, pair, swapped).astype(BF16)
                dup_ref[:, LANES:] = jnp.where(low_half, swapped, pair).astype(BF16)


def _inproj_even(x, g, w_all, layer, gain, tm=1024):
    s = x.shape[0]
    row = _after_cast(lambda i: (i, 0))
    widths = (512, 256, 256, 512, 512, 512)
    return pl.pallas_call(
        _inproj_even_kernel,
        grid=(CAST_STEPS + s // tm,),
        in_specs=[pl.BlockSpec((tm, D_MODEL), row), _const_spec((1, D_MODEL)),
                  _slab_spec(w_all, layer), _const_spec((1, E_COLS))],
        out_specs=[pl.BlockSpec((tm, wd), row) for wd in widths],
        out_shape=[jax.ShapeDtypeStruct((s, wd), BF16) for wd in widths],
        scratch_shapes=[pltpu.VMEM(w_all.shape[1:], BF16)],
        compiler_params=_params("arbitrary"),
        name="inproj_even",
    )(x, g, w_all, gain)


def _window_penalty():
    absrel = np.abs(np.arange(3 * A_BLOCK)[None, :] - A_BLOCK - np.arange(A_BLOCK)[:, None])
    slopes = LOG2E * 2.0 ** (-(8.0 / A_Q_HEADS) * (np.arange(A_Q_HEADS) + 1.0))
    pen = np.where(absrel <= A_WINDOW, -slopes[:, None, None] * absrel[None], NEG)
    return jnp.asarray(pen, F32)


def _window_kernel(sink_ref, pen_ref, q_ref, kp_ref, kc_ref, kn_ref, vp_ref, vc_ref, vn_ref, o_ref, *, tq):
    i = pl.program_id(0)
    lane = lax.broadcasted_iota(jnp.int32, (1, LANES), 1)
    low_half = lane < HEAD_DIM
    kcol = lax.broadcasted_iota(jnp.int32, (1, 3 * A_BLOCK), 1)
    edge_first = jnp.where((i == 0) & (kcol < A_BLOCK), NEG, 0.0)
    edge_last = jnp.where((i == pl.num_programs(0) - 1) & (kcol >= 2 * A_BLOCK), NEG, 0.0)
    nblk = tq // A_BLOCK
    edges = {0: edge_first, nblk - 1: edge_last}
    units = [(j, b) for j in range(A_KV_HEADS) for b in range(nblk)]
    kcat, vcat = [], []
    for j in range(A_KV_HEADS):
        ks = slice(LANES * j, LANES * (j + 1))
        kcat.append(jnp.concatenate([kp_ref[tq - A_BLOCK:, ks], kc_ref[:, ks], kn_ref[:A_BLOCK, ks]], axis=0))
        vcat.append(jnp.concatenate([vp_ref[tq - A_BLOCK:, ks], vc_ref[:, ks], vn_ref[:A_BLOCK, ks]], axis=0))
    def scores_of(j, b):
        rows = slice(A_BLOCK * b, A_BLOCK * (b + 1))
        lhs = []
        for e in range(A_GROUP):
            pair = q_ref[rows, LANES * (2 * j + e // 2):LANES * (2 * j + e // 2 + 1)]
            lhs.append(jnp.where(low_half if e % 2 == 0 else ~low_half, pair, jnp.zeros_like(pair)))
        return _dot_nt(jnp.concatenate(lhs, axis=0), kcat[j][A_BLOCK * b:A_BLOCK * (b + 3)])

    def softmax_of(j, b, s_all):
        p_unit, inv_unit = [], []
        for e in range(A_GROUP):
            h = A_GROUP * j + e
            sk = sink_ref[h]
            pen = pen_ref[h] + edges[b] if b in edges else pen_ref[h]
            s = s_all[A_BLOCK * e:A_BLOCK * (e + 1)] + pen
            m = jnp.maximum(jnp.max(s, axis=-1, keepdims=True), sk)
            p = jnp.exp2(s - m)
            inv_unit.append(1.0 / (jnp.sum(p, axis=-1, keepdims=True) + jnp.exp2(sk - m)))
            p_unit.append(p.astype(BF16))
        return jnp.concatenate(p_unit, axis=0), jnp.concatenate(inv_unit, axis=0)

    def emit(j, b, p_all, inv):
        rows = slice(A_BLOCK * b, A_BLOCK * (b + 1))
        o = _dot(p_all, vcat[j][A_BLOCK * b:A_BLOCK * (b + 3)]) * inv
        for pr in range(A_GROUP // 2):
            lo = o[A_BLOCK * 2 * pr:A_BLOCK * (2 * pr + 1)]
            hi = o[A_BLOCK * (2 * pr + 1):A_BLOCK * (2 * pr + 2)]
            col = LANES * (2 * j + pr)
            o_ref[rows, col:col + LANES] = jnp.where(low_half, lo, hi).astype(BF16)

    scores = [scores_of(j, b) for j, b in units]
    probs = [softmax_of(j, b, s_all) for (j, b), s_all in zip(units, scores)]
    for (j, b), (p_all, inv) in zip(units, probs):
        emit(j, b, p_all, inv)


def _window_attn(sink, qa, ka2, va2, tq=512):
    s = qa.shape[0]
    nb = s // tq
    prev = lambda i: (jnp.maximum(i - 1, 0), 0)
    cur = lambda i: (i, 0)
    nxt = lambda i: (jnp.minimum(i + 1, nb - 1), 0)
    kv = lambda f: pl.BlockSpec((tq, 2 * LANES), f)
    return pl.pallas_call(
        functools.partial(_window_kernel, tq=tq),
        grid=(nb,),
        in_specs=[pl.BlockSpec(memory_space=pltpu.SMEM), _const_spec((A_Q_HEADS, A_BLOCK, 3 * A_BLOCK)),
                  pl.BlockSpec((tq, 512), cur), kv(prev), kv(cur), kv(nxt), kv(prev), kv(cur), kv(nxt)],
        out_specs=pl.BlockSpec((tq, 512), cur),
        out_shape=jax.ShapeDtypeStruct((s, 512), BF16),
        compiler_params=_params("parallel"),
        name="window_gqa",
    )(sink, _window_penalty(), qa, ka2, ka2, ka2, va2, va2, va2)


NA_ROWS_PER_STEP = 16
NA_TOK = NA_ROWS_PER_STEP * GRID_W
NA_KEYS = NA_KH * GRID_W
NA_ROWS_PER_GROUP = 2
NA_HALO_LO = (NA_KH // 2) * GRID_W
NA_HALO_HI = (NA_KH - 1 - NA_KH // 2) * GRID_W
NA_CAT = NA_HALO_LO + NA_TOK + NA_HALO_HI


def _na_cat_base(i):
    first_row = jnp.clip(i * NA_ROWS_PER_STEP - NA_KH // 2, 0, GRID_ROWS - NA_CAT // GRID_W)
    return first_row * GRID_W


def _na_kernel(q_ref, kcat, vcat, rows_ref, o_ref, bias_ref):
    i = pl.program_id(0)

    @pl.when(i == 0)
    def _():
        _na_build_bias(rows_ref, bias_ref)

    lane = lax.broadcasted_iota(jnp.int32, (1, LANES), 1)
    low_half = lane < HEAD_DIM

    npair = B_HEADS // 2

    def row_info(rr):
        r = i * NA_ROWS_PER_STEP + rr
        rstart = jnp.clip(r - NA_KH // 2, 0, GRID_ROWS - NA_KH)
        start = pl.multiple_of(rstart * GRID_W - _na_cat_base(i), GRID_W)
        return start, rstart - r + NA_KH - 1, slice(rr * GRID_W, (rr + 1) * GRID_W)

    def scores_of(rr, p):
        start, _, qrows = row_info(rr)
        cols = slice(LANES * p, LANES * (p + 1))
        qp = q_ref[qrows, cols]
        lhs = jnp.concatenate([jnp.where(low_half, qp, jnp.zeros_like(qp)),
                               jnp.where(low_half, jnp.zeros_like(qp), qp)], axis=0)
        return _dot_nt(lhs, kcat[pl.ds(start, NA_KEYS), cols])

    def softmax_of(rr, p, scores):
        _, ri0, _ = row_info(rr)
        bias = jnp.concatenate(
            [jnp.concatenate([bias_ref[(2 * p + e) * NA_PAIRS + ri0 + 2 * m] for m in range(NA_KH // 2)], axis=1)
             for e in range(2)], axis=0)
        s = scores + bias
        pexp = jnp.exp2(s - jnp.max(s, axis=-1, keepdims=True))
        return pexp.astype(BF16), 1.0 / jnp.sum(pexp, axis=-1, keepdims=True)

    def emit(rr, p, probs, inv):
        start, _, qrows = row_info(rr)
        cols = slice(LANES * p, LANES * (p + 1))
        o = _dot(probs, vcat[pl.ds(start, NA_KEYS), cols]) * inv
        o_ref[qrows, cols] = jnp.where(low_half, o[:GRID_W], o[GRID_W:]).astype(BF16)

    for g in range(0, NA_ROWS_PER_STEP, NA_ROWS_PER_GROUP):
        units = [(rr, p) for rr in range(g, g + NA_ROWS_PER_GROUP) for p in range(npair)]
        scores = [scores_of(rr, p) for rr, p in units]
        probs = [softmax_of(rr, p, s) for (rr, p), s in zip(units, scores)]
        for (rr, p), (pr, inv) in zip(units, probs):
            emit(rr, p, pr, inv)


NA_PAIRS = 2 * NA_KH - 2
NA_REL = 2 * NA_KW - 1


def _na_bias_rows(rpb):
    pad = jnp.zeros(rpb.shape[:1] + (NA_PAIRS, GRID_W - NA_REL), F32)
    rows = jnp.concatenate([rpb[:, :-1], pad, rpb[:, 1:], pad], axis=-1)
    return rows.reshape(B_HEADS * NA_PAIRS, 2 * GRID_W)


def _na_build_bias(rows_ref, bias_ref):
    qc = lax.broadcasted_iota(jnp.int32, (GRID_W, 2 * GRID_W), 0)
    kc = lax.broadcasted_iota(jnp.int32, (GRID_W, 2 * GRID_W), 1) % GRID_W
    cstart = jnp.clip(qc - NA_KW // 2, 0, GRID_W - NA_KW)
    valid = (kc >= cstart) & (kc < cstart + NA_KW)

    def body(p, carry):
        row = jnp.broadcast_to(rows_ref[pl.ds(p, 1), :], (GRID_W, 2 * GRID_W))
        tile = pltpu.roll(row, 2 * GRID_W - (NA_KW - 1), 1, stride=1, stride_axis=0)
        bias_ref[p] = jnp.where(valid, tile, NEG)
        return carry

    lax.fori_loop(0, B_HEADS * NA_PAIRS, body, 0, unroll=8)


def _na_attn(qb, kb, vb, bias_rows):
    s = qb.shape[0]
    nb = s // NA_TOK
    blk = pl.BlockSpec((NA_TOK, 512), lambda i: (i, 0))
    halo = pl.BlockSpec((pl.Element(NA_CAT), pl.Element(512)), lambda i: (_na_cat_base(i), 0))
    return pl.pallas_call(
        _na_kernel,
        grid=(nb,),
        in_specs=[blk, halo, halo, _const_spec(bias_rows.shape)],
        out_specs=blk,
        out_shape=jax.ShapeDtypeStruct((s, 512), BF16),
        scratch_shapes=[pltpu.VMEM((B_HEADS * NA_PAIRS, GRID_W, 2 * GRID_W), F32)],
        compiler_params=_params("arbitrary"),
        name="neighbourhood_attn",
    )(qb, kb, vb, bias_rows)


FF_CHUNK = 1024


def _outproj_mlp_kernel(*refs, n_mix, gated):
    t = pl.program_id(0)
    wout_f32, g_ref, w1_f32, w2_f32, o_ref, wout_ref, w1_ref, w2_ref = refs[-8:]

    @pl.when(t < CAST_STEPS)
    def _():
        for src, dst in ((wout_f32, wout_ref), (w1_f32, w1_ref), (w2_f32, w2_ref)):
            _cast_slab(t, src, dst)

    @pl.when(t >= CAST_STEPS)
    def _():
        _outproj_mlp_tile(refs[:-8], wout_ref, g_ref, w1_ref, w2_ref, o_ref, n_mix=n_mix, gated=gated)


def _outproj_mlp_tile(act_refs, wout_ref, g_ref, w1_ref, w2_ref, o_ref, *, n_mix, gated):
    x_ref = act_refs[0]
    mix_refs = act_refs[1:1 + n_mix]
    if gated:
        (y_ref,), gate_ref, gn_ref = mix_refs, act_refs[1 + n_mix], act_refs[2 + n_mix]
        proj = None
        for h in range(RET_HEADS):
            cols = slice(h * RET_DV, (h + 1) * RET_DV)
            y = y_ref[:, cols].astype(F32)
            d = y - jnp.mean(y, axis=-1, keepdims=True)
            var = jnp.mean(d * d, axis=-1, keepdims=True)
            yn = d * lax.rsqrt(var + GN_EPS) * gn_ref[:, cols]
            mix_h = (yn * gate_ref[:, cols].astype(F32)).astype(BF16)
            part = _dot(mix_h, wout_ref[cols, :])
            proj = part if proj is None else proj + part
    else:
        proj = _dot(jnp.concatenate([m_ref[...] for m_ref in mix_refs], axis=1), wout_ref[...])
    x1 = x_ref[...] + proj
    hb = _rmsnorm_rows(x1, g_ref[...]).astype(BF16)
    acc = x1
    for c in range(0, D_FF, FF_CHUNK):
        a = jnp.maximum(_dot(hb, w1_ref[:, c:c + FF_CHUNK]), 0.0)
        acc = acc + _dot((a * a).astype(BF16), w2_ref[c:c + FF_CHUNK, :])
    o_ref[...] = acc


def _outproj_mlp(x, mixes, gate_gn, wout_all, mix_layer, g, w1_all, w2_all, layer, tm=512, name="outproj_mlp"):
    s = x.shape[0]
    row = _after_cast(lambda i: (i, 0))
    gated = gate_gn is not None
    acts = tuple(mixes) + ((gate_gn[0],) if gated else ())
    in_specs = [pl.BlockSpec((tm, D_MODEL), row)]
    in_specs += [pl.BlockSpec((tm, m.shape[1]), row) for m in acts]
    in_specs += [_const_spec(gate_gn[1].shape)] if gated else []
    in_specs += [_slab_spec(wout_all, mix_layer), _const_spec(g.shape),
                 _slab_spec(w1_all, layer), _slab_spec(w2_all, layer)]
    consts = ((gate_gn[1],) if gated else ()) + (wout_all, g, w1_all, w2_all)
    return pl.pallas_call(
        functools.partial(_outproj_mlp_kernel, n_mix=len(mixes), gated=gated),
        grid=(CAST_STEPS + s // tm,),
        in_specs=in_specs,
        out_specs=pl.BlockSpec((tm, D_MODEL), row),
        out_shape=jax.ShapeDtypeStruct((s, D_MODEL), F32),
        scratch_shapes=[pltpu.VMEM(w.shape[1:], BF16) for w in (wout_all, w1_all, w2_all)],
        compiler_params=_params("arbitrary"),
        name=name,
    )(x, *acts, *consts)


O_Q, O_K, O_V, O_G = 0, D_MODEL, 2 * D_MODEL, 2 * D_MODEL + ODD_MIX


def _inproj_odd_kernel(lgf_ref, x_ref, g_ref, w_f32, q_ref, kt_ref, v_ref, gate_ref, rf_ref, w_ref, wkt_ref, state):
    t = pl.program_id(0)

    @pl.when(t < CAST_STEPS)
    def _():
        _cast_slab(t, w_f32, w_ref)
        rows = w_f32.shape[0]
        wkt_ref[:, pl.ds(pl.multiple_of(t * rows, rows), rows)] = w_f32[:, O_K:O_K + D_MODEL].T.astype(BF16)

    @pl.when(t == CAST_STEPS)
    def _():
        state[...] = jnp.zeros_like(state)

    @pl.when(t >= CAST_STEPS)
    def _():
        xb = _rmsnorm_rows(x_ref[...], g_ref[...]).astype(BF16)
        for off in range(0, ODD_MIX, PROJ_CHUNK):
            gate_ref[:, off:off + PROJ_CHUNK] = _silu(
                _dot(xb, w_ref[:, O_G + off:O_G + off + PROJ_CHUNK])).astype(BF16)
        vs = []
        for off in range(0, ODD_MIX, PROJ_CHUNK):
            vs.append(_dot(xb, w_ref[:, O_V + off:O_V + off + PROJ_CHUNK]).astype(BF16))
            v_ref[:, off:off + PROJ_CHUNK] = vs[-1]
        kt = _dot_nt(wkt_ref[...], xb) * RET_DK ** -0.5
        kt_ref[...] = kt.astype(BF16)
        tok = lax.broadcasted_iota(jnp.int32, (1, RET_CHUNK), 1).astype(F32)
        for h in range(RET_HEADS):
            lg = lgf_ref[h]
            zeta = jnp.exp(lg * (RET_CHUNK - 1.0 - tok))
            decay = jnp.exp(lg * jnp.full((1, 1), RET_CHUNK, F32))
            r = state[h]
            for c in range(x_ref.shape[0] // RET_CHUNK):
                toks = slice(c * RET_CHUNK, (c + 1) * RET_CHUNK)
                kz = (kt[h * RET_DK:(h + 1) * RET_DK, toks] * zeta).astype(BF16)
                col = h * RET_DV
                v = vs[col // PROJ_CHUNK][toks, col % PROJ_CHUNK:col % PROJ_CHUNK + RET_DV]
                rf_ref[c, h] = r.astype(BF16)
                r = r * decay + _dot(kz, v)
            state[h] = r
        q_ref[...] = _dot(xb, w_ref[:, O_Q:O_Q + D_MODEL]).astype(BF16)


def _inproj_odd(lgf, x, g, w_all, layer, tm=512):
    s = x.shape[0]
    row = _after_cast(lambda i: (i, 0))
    chunks = tm // RET_CHUNK
    return pl.pallas_call(
        _inproj_odd_kernel,
        grid=(CAST_STEPS + s // tm,),
        in_specs=[pl.BlockSpec(memory_space=pltpu.SMEM), pl.BlockSpec((tm, D_MODEL), row),
                  _const_spec((1, D_MODEL)), _slab_spec(w_all, layer)],
        out_specs=[pl.BlockSpec((tm, D_MODEL), row), pl.BlockSpec((D_MODEL, tm), _after_cast(lambda i: (0, i))),
                   pl.BlockSpec((tm, ODD_MIX), row), pl.BlockSpec((tm, ODD_MIX), row),
                   pl.BlockSpec((chunks, RET_HEADS, RET_DK, RET_DV), _after_cast(lambda i: (i, 0, 0, 0)))],
        out_shape=[jax.ShapeDtypeStruct((s, D_MODEL), BF16), jax.ShapeDtypeStruct((D_MODEL, s), BF16),
                   jax.ShapeDtypeStruct((s, ODD_MIX), BF16), jax.ShapeDtypeStruct((s, ODD_MIX), BF16),
                   jax.ShapeDtypeStruct((s // RET_CHUNK, RET_HEADS, RET_DK, RET_DV), BF16)],
        scratch_shapes=[pltpu.VMEM(w_all.shape[1:], BF16), pltpu.VMEM((D_MODEL, D_MODEL), BF16),
                        pltpu.VMEM((RET_HEADS, RET_DK, RET_DV), F32)],
        compiler_params=_params("arbitrary"),
        name="inproj_odd",
    )(lgf, x, g, w_all)


RET_CHUNKS_PER_STEP = 32
RET_TOK = RET_CHUNKS_PER_STEP * RET_CHUNK
RET_STEPS = SEQ // RET_TOK


def _retention_kernel(lg_ref, q_ref, kt_ref, v_ref, rf_ref, o_ref, state):
    h = pl.program_id(0)
    lgf = lg_ref[0, h]
    lgb = lg_ref[1, h]
    tok_lane = lax.broadcasted_iota(jnp.int32, (1, RET_CHUNK), 1).astype(F32)
    tok_row = lax.broadcasted_iota(jnp.int32, (RET_CHUNK, 1), 0).astype(F32)
    chunk_len = jnp.full((1, 1), RET_CHUNK, F32)

    @pl.when(pl.program_id(1) == 0)
    def _():
        state[...] = jnp.zeros_like(state)

    diff = (lax.broadcasted_iota(jnp.int32, (RET_CHUNK, RET_CHUNK), 0)
            - lax.broadcasted_iota(jnp.int32, (RET_CHUNK, RET_CHUNK), 1)).astype(F32)
    dsum = jnp.where(diff >= 0, jnp.exp(lgf * jnp.maximum(diff, 0.0)), jnp.exp(lgb * jnp.maximum(-diff, 0.0)))
    cross_f = jnp.exp(lgf * (tok_row + 1.0))
    cross_b = jnp.exp(lgb * (RET_CHUNK - tok_row))
    zeta = jnp.exp(lgb * tok_lane)
    decay = jnp.exp(lgb * chunk_len)
    r = state[...]
    for c in reversed(range(RET_CHUNKS_PER_STEP)):
        tok = slice(c * RET_CHUNK, (c + 1) * RET_CHUNK)
        q = q_ref[tok, :]
        kt = kt_ref[:, tok]
        v = v_ref[tok, :]
        a = (_dot(q, kt) * dsum).astype(BF16)
        qf = q.astype(F32)
        lhs = jnp.concatenate([a, (qf * cross_f).astype(BF16), (qf * cross_b).astype(BF16)], axis=1)
        rhs = jnp.concatenate([v, rf_ref[c], r.astype(BF16)], axis=0)
        o_ref[tok, :] = _dot(lhs, rhs).astype(BF16)
        r = r * decay + _dot((kt.astype(F32) * zeta).astype(BF16), v)
    state[...] = r


def _retention(lg, q, kt, v, rf):
    s = q.shape[0]
    blk = lambda t: RET_STEPS - 1 - t
    return pl.pallas_call(
        _retention_kernel,
        grid=(RET_HEADS, RET_STEPS),
        in_specs=[pl.BlockSpec(memory_space=pltpu.SMEM),
                  pl.BlockSpec((RET_TOK, RET_DK), lambda h, t: (blk(t), h)),
                  pl.BlockSpec((RET_DK, RET_TOK), lambda h, t: (h, blk(t))),
                  pl.BlockSpec((RET_TOK, RET_DV), lambda h, t: (blk(t), h)),
                  pl.BlockSpec((RET_CHUNKS_PER_STEP, None, RET_DK, RET_DV), lambda h, t: (blk(t), h, 0, 0))],
        out_specs=pl.BlockSpec((RET_TOK, RET_DV), lambda h, t: (blk(t), h)),
        out_shape=jax.ShapeDtypeStruct((s, ODD_MIX), BF16),
        scratch_shapes=[pltpu.VMEM((RET_DK, RET_DV), F32)],
        compiler_params=_params("arbitrary", "arbitrary"),
        name="retention",
    )(lg, q, kt, v, rf)


def _even_qk_gain(qn_a, kn_a, qn_b, kn_b):
    scale = HEAD_DIM ** -0.5 * LOG2E
    gain = jnp.concatenate([jnp.tile(qn_a, A_Q_HEADS) * scale, jnp.tile(kn_a, A_KV_HEADS),
                            jnp.ones((A_KV_HEADS * HEAD_DIM,), F32),
                            jnp.tile(qn_b, B_HEADS) * scale, jnp.tile(kn_b, B_HEADS),
                            jnp.ones((B_HEADS * HEAD_DIM,), F32)]).astype(F32)
    return gain[None, :]


def kernel(x, attn_norm_e, w_in_e, q_norm_a, k_norm_a, sink_a, q_norm_b, k_norm_b, rpb_b, w_out_e,
           ret_norm_o, w_in_o, decay_fwd_o, decay_bwd_o, ret_gn_o, w_out_o,
           mlp_norm, w_mlp_in, w_mlp_out):
    b, s, d = x.shape
    xs = x.reshape(b * s, d)
    depth = mlp_norm.shape[0]
    for layer in range(depth):
        i = layer // 2
        g_mlp = mlp_norm[layer][None, :]
        if layer % 2 == 0:
            gain = _even_qk_gain(q_norm_a[i], k_norm_a[i], q_norm_b[i], k_norm_b[i])
            qa, ka2, va2, qb, kb, vb = _inproj_even(xs, attn_norm_e[i][None, :], w_in_e, i, gain)
            oa = _window_attn(sink_a[i].astype(F32) * LOG2E, qa, ka2, va2)
            ob = _na_attn(qb, kb, vb, _na_bias_rows(rpb_b[i].astype(F32) * LOG2E))
            xs = _outproj_mlp(xs, (oa, ob), None, w_out_e, i, g_mlp, w_mlp_in, w_mlp_out, layer,
                              tm=1024, name="outproj_mlp_even")
        else:
            lg = jnp.stack([jax.nn.log_sigmoid(decay_fwd_o[i].astype(F32)),
                            jax.nn.log_sigmoid(decay_bwd_o[i].astype(F32))])
            q, kt, v, gate, rf = _inproj_odd(lg[0], xs, ret_norm_o[i][None, :], w_in_o, i)
            y = _retention(lg, q, kt, v, rf)
            xs = _outproj_mlp(xs, (y,), (gate, ret_gn_o[i].astype(F32)[None, :]), w_out_o, i,
                              g_mlp, w_mlp_in, w_mlp_out, layer, name="outproj_mlp_odd")
    return xs.reshape(b, s, d)
```

```python
import functools

import numpy as np
import jax
import jax.numpy as jnp
from jax import lax
from jax.experimental import pallas as pl
from jax.experimental.pallas import tpu as pltpu

D_MODEL = 1024
SEQ = 16384
HEAD_DIM = 64
A_Q_HEADS = 8
A_KV_HEADS = 2
A_GROUP = A_Q_HEADS // A_KV_HEADS
A_WINDOW = 128
A_BLOCK = 128
B_HEADS = 8
GRID_W = 64
GRID_ROWS = SEQ // GRID_W
NA_KH = 8
NA_KW = 16
RET_HEADS = 8
RET_DK = D_MODEL // RET_HEADS
RET_DV = 2 * D_MODEL // RET_HEADS
RET_CHUNK = 256
D_FF = 4 * D_MODEL
ODD_MIX = RET_HEADS * RET_DV
RMS_EPS = 1e-6
GN_EPS = 1e-5

LANES = 128
MXU_N = 256
PROJ_CHUNK = 1024
NEG = -1e30
LOG2E = 1.4426950408889634
VMEM_LIMIT = 60 * 1024 * 1024

F32 = jnp.float32
BF16 = jnp.bfloat16

E_QA, E_KA, E_VA, E_QB, E_KB, E_VB = 0, 512, 640, 768, 1280, 1792
E_COLS = 2304


def _const_spec(shape):
    return pl.BlockSpec(shape, lambda *_: (0,) * len(shape), pipeline_mode=pl.Buffered(1))


def _params(*sem):
    return pltpu.CompilerParams(dimension_semantics=sem, vmem_limit_bytes=VMEM_LIMIT)


def _rmsnorm_rows(x, g):
    return x * lax.rsqrt(jnp.mean(x * x, axis=-1, keepdims=True) + RMS_EPS) * g


def _silu(g):
    return g * (0.5 * jnp.tanh(0.5 * g) + 0.5)


def _dot(a, b):
    return jnp.dot(a, b, preferred_element_type=F32)


CAST_STEPS = 8


def _slab_spec(w, layer):
    return pl.BlockSpec((None, w.shape[1] // CAST_STEPS, w.shape[2]),
                        lambda t: (layer, jnp.minimum(t, CAST_STEPS - 1), 0))


def _cast_slab(t, src_ref, dst_ref):
    rows = src_ref.shape[0]
    dst_ref[pl.ds(pl.multiple_of(t * rows, rows), rows), :] = src_ref[...].astype(BF16)


def _after_cast(index):
    return lambda t: index(jnp.maximum(t - CAST_STEPS, 0))


def _dot_nt(a, b):
    return lax.dot_general(a, b, (((1,), (1,)), ((), ())), preferred_element_type=F32)


def _inproj_even_kernel(x_ref, g_ref, w_f32, gain_ref, qa_ref, ka2_ref, va2_ref, qb_ref, kb_ref, vb_ref, w_ref):
    t = pl.program_id(0)

    @pl.when(t < CAST_STEPS)
    def _():
        _cast_slab(t, w_f32, w_ref)

    @pl.when(t >= CAST_STEPS)
    def _():
        xb = _rmsnorm_rows(x_ref[...], g_ref[...]).astype(BF16)
        r = lax.broadcasted_iota(jnp.int32, (MXU_N, MXU_N), 0) // HEAD_DIM
        c = lax.broadcasted_iota(jnp.int32, (MXU_N, MXU_N), 1) // HEAD_DIM
        head_mean = jnp.where(r == c, 1.0 / HEAD_DIM, 0.0).astype(BF16)
        is_ka = lax.broadcasted_iota(jnp.int32, (1, MXU_N), 1) < E_VA - E_KA
        low_half = lax.broadcasted_iota(jnp.int32, (1, LANES), 1) < HEAD_DIM
        outs = ((qa_ref, E_QA, 512, "all"), (None, E_KA, 256, "ka"), (qb_ref, E_QB, 512, "all"),
                (kb_ref, E_KB, 512, "all"), (vb_ref, E_VB, 512, "none"))
        chunks = [(ref, start + off, off, normed) for ref, start, width, normed in outs
                  for off in range(0, width, MXU_N)]
        ys = [_dot(xb, w_ref[:, col:col + MXU_N]) for _, col, _, _ in chunks]
        ms = [_dot((y * y).astype(BF16), head_mean) if normed != "none" else None
              for y, (_, _, _, normed) in zip(ys, chunks)]
        for y, m, (ref, col, off, normed) in zip(ys, ms, chunks):
            if normed != "none":
                scale = lax.rsqrt(m + RMS_EPS) * gain_ref[:, col:col + MXU_N]
                y = y * (scale if normed == "all" else jnp.where(is_ka, scale, 1.0))
            if ref is not None:
                ref[:, off:off + MXU_N] = y.astype(BF16)
                continue
            for pair, dup_ref in ((y[:, :LANES], ka2_ref), (y[:, LANES:], va2_ref)):
                swapped = pltpu.roll(pair, HEAD_DIM, axis=1)
                dup_ref[:, :LANES] = jnp.where(low_half, pair, swapped).astype(BF16)
                dup_ref[:, LANES:] = jnp.where(low_half, swapped, pair).astype(BF16)


def _inproj_even(x, g, w_all, layer, gain, tm=1024):
    s = x.shape[0]
    row = _after_cast(lambda i: (i, 0))
    widths = (512, 256, 256, 512, 512, 512)
    return pl.pallas_call(
        _inproj_even_kernel,
        grid=(CAST_STEPS + s // tm,),
        in_specs=[pl.BlockSpec((tm, D_MODEL), row), _const_spec((1, D_MODEL)),
                  _slab_spec(w_all, layer), _const_spec((1, E_COLS))],
        out_specs=[pl.BlockSpec((tm, wd), row) for wd in widths],
        out_shape=[jax.ShapeDtypeStruct((s, wd), BF16) for wd in widths],
        scratch_shapes=[pltpu.VMEM(w_all.shape[1:], BF16)],
        compiler_params=_params("arbitrary"),
        name="inproj_even",
    )(x, g, w_all, gain)


WIN_UNITS_PER_GROUP = 8


def _window_penalty():
    absrel = np.abs(np.arange(3 * A_BLOCK)[None, :] - A_BLOCK - np.arange(A_BLOCK)[:, None])
    slopes = LOG2E * 2.0 ** (-(8.0 / A_Q_HEADS) * (np.arange(A_Q_HEADS) + 1.0))
    pen = np.where(absrel <= A_WINDOW, -slopes[:, None, None] * absrel[None], NEG)
    return jnp.asarray(pen, F32)


def _window_kernel(sink_ref, pen_ref, q_ref, kp_ref, kc_ref, kn_ref, vp_ref, vc_ref, vn_ref, o_ref, *, tq):
    i = pl.program_id(0)
    lane = lax.broadcasted_iota(jnp.int32, (1, LANES), 1)
    low_half = lane < HEAD_DIM
    kcol = lax.broadcasted_iota(jnp.int32, (1, 3 * A_BLOCK), 1)
    edge_first = jnp.where((i == 0) & (kcol < A_BLOCK), NEG, 0.0)
    edge_last = jnp.where((i == pl.num_programs(0) - 1) & (kcol >= 2 * A_BLOCK), NEG, 0.0)
    nblk = tq // A_BLOCK
    edges = {0: edge_first, nblk - 1: edge_last}
    units = [(j, b) for j in range(A_KV_HEADS) for b in range(nblk)]
    kcat, vcat = [], []
    for j in range(A_KV_HEADS):
        ks = slice(LANES * j, LANES * (j + 1))
        kcat.append(jnp.concatenate([kp_ref[tq - A_BLOCK:, ks], kc_ref[:, ks], kn_ref[:A_BLOCK, ks]], axis=0))
        vcat.append(jnp.concatenate([vp_ref[tq - A_BLOCK:, ks], vc_ref[:, ks], vn_ref[:A_BLOCK, ks]], axis=0))
    def scores_of(j, b):
        rows = slice(A_BLOCK * b, A_BLOCK * (b + 1))
        lhs = []
        for e in range(A_GROUP):
            pair = q_ref[rows, LANES * (2 * j + e // 2):LANES * (2 * j + e // 2 + 1)]
            lhs.append(jnp.where(low_half if e % 2 == 0 else ~low_half, pair, jnp.zeros_like(pair)))
        return _dot_nt(jnp.concatenate(lhs, axis=0), kcat[j][A_BLOCK * b:A_BLOCK * (b + 3)])

    def softmax_of(j, b, s_all):
        p_unit, inv_unit = [], []
        for e in range(A_GROUP):
            h = A_GROUP * j + e
            sk = sink_ref[h]
            pen = pen_ref[h] + edges[b] if b in edges else pen_ref[h]
            s = s_all[A_BLOCK * e:A_BLOCK * (e + 1)] + pen
            m = jnp.maximum(jnp.max(s, axis=-1, keepdims=True), sk)
            p = jnp.exp2(s - m)
            inv_unit.append(1.0 / (jnp.sum(p, axis=-1, keepdims=True) + jnp.exp2(sk - m)))
            p_unit.append(p.astype(BF16))
        return jnp.concatenate(p_unit, axis=0), jnp.concatenate(inv_unit, axis=0)

    def emit(j, b, p_all, inv):
        rows = slice(A_BLOCK * b, A_BLOCK * (b + 1))
        o = _dot(p_all, vcat[j][A_BLOCK * b:A_BLOCK * (b + 3)]) * inv
        for pr in range(A_GROUP // 2):
            lo = o[A_BLOCK * 2 * pr:A_BLOCK * (2 * pr + 1)]
            hi = o[A_BLOCK * (2 * pr + 1):A_BLOCK * (2 * pr + 2)]
            col = LANES * (2 * j + pr)
            o_ref[rows, col:col + LANES] = jnp.where(low_half, lo, hi).astype(BF16)

    for g in range(0, len(units), WIN_UNITS_PER_GROUP):
        group = units[g:g + WIN_UNITS_PER_GROUP]
        scores = [scores_of(j, b) for j, b in group]
        probs = [softmax_of(j, b, s_all) for (j, b), s_all in zip(group, scores)]
        for (j, b), (p_all, inv) in zip(group, probs):
            emit(j, b, p_all, inv)


def _window_attn(sink, qa, ka2, va2, tq=1024):
    s = qa.shape[0]
    nb = s // tq
    prev = lambda i: (jnp.maximum(i - 1, 0), 0)
    cur = lambda i: (i, 0)
    nxt = lambda i: (jnp.minimum(i + 1, nb - 1), 0)
    kv = lambda f: pl.BlockSpec((tq, 2 * LANES), f)
    return pl.pallas_call(
        functools.partial(_window_kernel, tq=tq),
        grid=(nb,),
        in_specs=[pl.BlockSpec(memory_space=pltpu.SMEM), _const_spec((A_Q_HEADS, A_BLOCK, 3 * A_BLOCK)),
                  pl.BlockSpec((tq, 512), cur), kv(prev), kv(cur), kv(nxt), kv(prev), kv(cur), kv(nxt)],
        out_specs=pl.BlockSpec((tq, 512), cur),
        out_shape=jax.ShapeDtypeStruct((s, 512), BF16),
        compiler_params=_params("parallel"),
        name="window_gqa",
    )(sink, _window_penalty(), qa, ka2, ka2, ka2, va2, va2, va2)


NA_ROWS_PER_STEP = 16
NA_TOK = NA_ROWS_PER_STEP * GRID_W
NA_KEYS = NA_KH * GRID_W
NA_ROWS_PER_GROUP = 2
NA_HALO_LO = (NA_KH // 2) * GRID_W
NA_HALO_HI = (NA_KH - 1 - NA_KH // 2) * GRID_W
NA_CAT = NA_HALO_LO + NA_TOK + NA_HALO_HI


def _na_cat_base(i):
    first_row = jnp.clip(i * NA_ROWS_PER_STEP - NA_KH // 2, 0, GRID_ROWS - NA_CAT // GRID_W)
    return first_row * GRID_W


def _na_kernel(q_ref, kcat, vcat, rows_ref, o_ref, bias_ref):
    i = pl.program_id(0)

    @pl.when(i == 0)
    def _():
        _na_build_bias(rows_ref, bias_ref)

    lane = lax.broadcasted_iota(jnp.int32, (1, LANES), 1)
    low_half = lane < HEAD_DIM

    npair = B_HEADS // 2

    def row_info(rr):
        r = i * NA_ROWS_PER_STEP + rr
        rstart = jnp.clip(r - NA_KH // 2, 0, GRID_ROWS - NA_KH)
        start = pl.multiple_of(rstart * GRID_W - _na_cat_base(i), GRID_W)
        return start, rstart - r + NA_KH - 1, slice(rr * GRID_W, (rr + 1) * GRID_W)

    def scores_of(rr, p):
        start, _, qrows = row_info(rr)
        cols = slice(LANES * p, LANES * (p + 1))
        qp = q_ref[qrows, cols]
        lhs = jnp.concatenate([jnp.where(low_half, qp, jnp.zeros_like(qp)),
                               jnp.where(low_half, jnp.zeros_like(qp), qp)], axis=0)
        return _dot_nt(lhs, kcat[pl.ds(start, NA_KEYS), cols])

    def softmax_of(rr, p, scores):
        _, ri0, _ = row_info(rr)
        bias = jnp.concatenate(
            [jnp.concatenate([bias_ref[(2 * p + e) * NA_PAIRS + ri0 + 2 * m] for m in range(NA_KH // 2)], axis=1)
             for e in range(2)], axis=0)
        s = scores + bias
        pexp = jnp.exp2(s - jnp.max(s, axis=-1, keepdims=True))
        return pexp.astype(BF16), 1.0 / jnp.sum(pexp, axis=-1, keepdims=True)

    def emit(rr, p, probs, inv):
        start, _, qrows = row_info(rr)
        cols = slice(LANES * p, LANES * (p + 1))
        o = _dot(probs, vcat[pl.ds(start, NA_KEYS), cols]) * inv
        o_ref[qrows, cols] = jnp.where(low_half, o[:GRID_W], o[GRID_W:]).astype(BF16)

    for g in range(0, NA_ROWS_PER_STEP, NA_ROWS_PER_GROUP):
        units = [(rr, p) for rr in range(g, g + NA_ROWS_PER_GROUP) for p in range(npair)]
        scores = [scores_of(rr, p) for rr, p in units]
        probs = [softmax_of(rr, p, s) for (rr, p), s in zip(units, scores)]
        for (rr, p), (pr, inv) in zip(units, probs):
            emit(rr, p, pr, inv)


NA_PAIRS = 2 * NA_KH - 2
NA_REL = 2 * NA_KW - 1


def _na_bias_rows(rpb):
    pad = jnp.zeros(rpb.shape[:1] + (NA_PAIRS, GRID_W - NA_REL), F32)
    rows = jnp.concatenate([rpb[:, :-1], pad, rpb[:, 1:], pad], axis=-1)
    return rows.reshape(B_HEADS * NA_PAIRS, 2 * GRID_W)


def _na_build_bias(rows_ref, bias_ref):
    qc = lax.broadcasted_iota(jnp.int32, (GRID_W, 2 * GRID_W), 0)
    kc = lax.broadcasted_iota(jnp.int32, (GRID_W, 2 * GRID_W), 1) % GRID_W
    cstart = jnp.clip(qc - NA_KW // 2, 0, GRID_W - NA_KW)
    valid = (kc >= cstart) & (kc < cstart + NA_KW)

    def body(p, carry):
        row = jnp.broadcast_to(rows_ref[pl.ds(p, 1), :], (GRID_W, 2 * GRID_W))
        tile = pltpu.roll(row, 2 * GRID_W - (NA_KW - 1), 1, stride=1, stride_axis=0)
        bias_ref[p] = jnp.where(valid, tile, NEG)
        return carry

    lax.fori_loop(0, B_HEADS * NA_PAIRS, body, 0, unroll=8)


def _na_attn(qb, kb, vb, bias_rows):
    s = qb.shape[0]
    nb = s // NA_TOK
    blk = pl.BlockSpec((NA_TOK, 512), lambda i: (i, 0))
    halo = pl.BlockSpec((pl.Element(NA_CAT), pl.Element(512)), lambda i: (_na_cat_base(i), 0))
    return pl.pallas_call(
        _na_kernel,
        grid=(nb,),
        in_specs=[blk, halo, halo, _const_spec(bias_rows.shape)],
        out_specs=blk,
        out_shape=jax.ShapeDtypeStruct((s, 512), BF16),
        scratch_shapes=[pltpu.VMEM((B_HEADS * NA_PAIRS, GRID_W, 2 * GRID_W), F32)],
        compiler_params=_params("arbitrary"),
        name="neighbourhood_attn",
    )(qb, kb, vb, bias_rows)


FF_CHUNK = 1024


def _outproj_mlp_kernel(*refs, n_mix, gated):
    t = pl.program_id(0)
    wout_f32, g_ref, w1_f32, w2_f32, o_ref, wout_ref, w1_ref, w2_ref = refs[-8:]

    @pl.when(t < CAST_STEPS)
    def _():
        for src, dst in ((wout_f32, wout_ref), (w1_f32, w1_ref), (w2_f32, w2_ref)):
            _cast_slab(t, src, dst)

    @pl.when(t >= CAST_STEPS)
    def _():
        _outproj_mlp_tile(refs[:-8], wout_ref, g_ref, w1_ref, w2_ref, o_ref, n_mix=n_mix, gated=gated)


def _outproj_mlp_tile(act_refs, wout_ref, g_ref, w1_ref, w2_ref, o_ref, *, n_mix, gated):
    x_ref = act_refs[0]
    mix_refs = act_refs[1:1 + n_mix]
    if gated:
        (y_ref,), gate_ref, gn_ref = mix_refs, act_refs[1 + n_mix], act_refs[2 + n_mix]
        proj = None
        for h in range(RET_HEADS):
            cols = slice(h * RET_DV, (h + 1) * RET_DV)
            y = y_ref[:, cols].astype(F32)
            d = y - jnp.mean(y, axis=-1, keepdims=True)
            var = jnp.mean(d * d, axis=-1, keepdims=True)
            yn = d * lax.rsqrt(var + GN_EPS) * gn_ref[:, cols]
            mix_h = (yn * gate_ref[:, cols].astype(F32)).astype(BF16)
            part = _dot(mix_h, wout_ref[cols, :])
            proj = part if proj is None else proj + part
    else:
        proj = _dot(jnp.concatenate([m_ref[...] for m_ref in mix_refs], axis=1), wout_ref[...])
    x1 = x_ref[...] + proj
    hb = _rmsnorm_rows(x1, g_ref[...]).astype(BF16)
    acc = x1
    for c in range(0, D_FF, FF_CHUNK):
        a = jnp.maximum(_dot(hb, w1_ref[:, c:c + FF_CHUNK]), 0.0)
        acc = acc + _dot((a * a).astype(BF16), w2_ref[c:c + FF_CHUNK, :])
    o_ref[...] = acc


def _outproj_mlp(x, mixes, gate_gn, wout_all, mix_layer, g, w1_all, w2_all, layer, tm=512, name="outproj_mlp"):
    s = x.shape[0]
    row = _after_cast(lambda i: (i, 0))
    gated = gate_gn is not None
    acts = tuple(mixes) + ((gate_gn[0],) if gated else ())
    in_specs = [pl.BlockSpec((tm, D_MODEL), row)]
    in_specs += [pl.BlockSpec((tm, m.shape[1]), row) for m in acts]
    in_specs += [_const_spec(gate_gn[1].shape)] if gated else []
    in_specs += [_slab_spec(wout_all, mix_layer), _const_spec(g.shape),
                 _slab_spec(w1_all, layer), _slab_spec(w2_all, layer)]
    consts = ((gate_gn[1],) if gated else ()) + (wout_all, g, w1_all, w2_all)
    return pl.pallas_call(
        functools.partial(_outproj_mlp_kernel, n_mix=len(mixes), gated=gated),
        grid=(CAST_STEPS + s // tm,),
        in_specs=in_specs,
        out_specs=pl.BlockSpec((tm, D_MODEL), row),
        out_shape=jax.ShapeDtypeStruct((s, D_MODEL), F32),
        scratch_shapes=[pltpu.VMEM(w.shape[1:], BF16) for w in (wout_all, w1_all, w2_all)],
        compiler_params=_params("arbitrary"),
        name=name,
    )(x, *acts, *consts)


O_Q, O_K, O_V, O_G = 0, D_MODEL, 2 * D_MODEL, 2 * D_MODEL + ODD_MIX


def _inproj_odd_kernel(lgf_ref, x_ref, g_ref, w_f32, q_ref, kt_ref, v_ref, gate_ref, rf_ref, w_ref, wkt_ref, state):
    t = pl.program_id(0)

    @pl.when(t < CAST_STEPS)
    def _():
        _cast_slab(t, w_f32, w_ref)
        rows = w_f32.shape[0]
        wkt_ref[:, pl.ds(pl.multiple_of(t * rows, rows), rows)] = w_f32[:, O_K:O_K + D_MODEL].T.astype(BF16)

    @pl.when(t == CAST_STEPS)
    def _():
        state[...] = jnp.zeros_like(state)

    @pl.when(t >= CAST_STEPS)
    def _():
        xb = _rmsnorm_rows(x_ref[...], g_ref[...]).astype(BF16)
        for off in range(0, ODD_MIX, PROJ_CHUNK):
            gate_ref[:, off:off + PROJ_CHUNK] = _silu(
                _dot(xb, w_ref[:, O_G + off:O_G + off + PROJ_CHUNK])).astype(BF16)
        vs = []
        for off in range(0, ODD_MIX, PROJ_CHUNK):
            vs.append(_dot(xb, w_ref[:, O_V + off:O_V + off + PROJ_CHUNK]).astype(BF16))
            v_ref[:, off:off + PROJ_CHUNK] = vs[-1]
        kt = _dot_nt(wkt_ref[...], xb) * RET_DK ** -0.5
        kt_ref[...] = kt.astype(BF16)
        tok = lax.broadcasted_iota(jnp.int32, (1, RET_CHUNK), 1).astype(F32)
        for h in range(RET_HEADS):
            lg = lgf_ref[h]
            zeta = jnp.exp(lg * (RET_CHUNK - 1.0 - tok))
            decay = jnp.exp(lg * jnp.full((1, 1), RET_CHUNK, F32))
            r = state[h]
            for c in range(x_ref.shape[0] // RET_CHUNK):
                toks = slice(c * RET_CHUNK, (c + 1) * RET_CHUNK)
                kz = (kt[h * RET_DK:(h + 1) * RET_DK, toks] * zeta).astype(BF16)
                col = h * RET_DV
                v = vs[col // PROJ_CHUNK][toks, col % PROJ_CHUNK:col % PROJ_CHUNK + RET_DV]
                rf_ref[c, h] = r.astype(BF16)
                r = r * decay + _dot(kz, v)
            state[h] = r
        q_ref[...] = _dot(xb, w_ref[:, O_Q:O_Q + D_MODEL]).astype(BF16)


def _inproj_odd(lgf, x, g, w_all, layer, tm=512):
    s = x.shape[0]
    row = _after_cast(lambda i: (i, 0))
    chunks = tm // RET_CHUNK
    return pl.pallas_call(
        _inproj_odd_kernel,
        grid=(CAST_STEPS + s // tm,),
        in_specs=[pl.BlockSpec(memory_space=pltpu.SMEM), pl.BlockSpec((tm, D_MODEL), row),
                  _const_spec((1, D_MODEL)), _slab_spec(w_all, layer)],
        out_specs=[pl.BlockSpec((tm, D_MODEL), row), pl.BlockSpec((D_MODEL, tm), _after_cast(lambda i: (0, i))),
                   pl.BlockSpec((tm, ODD_MIX), row), pl.BlockSpec((tm, ODD_MIX), row),
                   pl.BlockSpec((chunks, RET_HEADS, RET_DK, RET_DV), _after_cast(lambda i: (i, 0, 0, 0)))],
        out_shape=[jax.ShapeDtypeStruct((s, D_MODEL), BF16), jax.ShapeDtypeStruct((D_MODEL, s), BF16),
                   jax.ShapeDtypeStruct((s, ODD_MIX), BF16), jax.ShapeDtypeStruct((s, ODD_MIX), BF16),
                   jax.ShapeDtypeStruct((s // RET_CHUNK, RET_HEADS, RET_DK, RET_DV), BF16)],
        scratch_shapes=[pltpu.VMEM(w_all.shape[1:], BF16), pltpu.VMEM((D_MODEL, D_MODEL), BF16),
                        pltpu.VMEM((RET_HEADS, RET_DK, RET_DV), F32)],
        compiler_params=_params("arbitrary"),
        name="inproj_odd",
    )(lgf, x, g, w_all)


RET_CHUNKS_PER_STEP = 32
RET_TOK = RET_CHUNKS_PER_STEP * RET_CHUNK
RET_STEPS = SEQ // RET_TOK


def _retention_kernel(lg_ref, q_ref, kt_ref, v_ref, rf_ref, o_ref, state):
    h = pl.program_id(0)
    lgf = lg_ref[0, h]
    lgb = lg_ref[1, h]
    tok_lane = lax.broadcasted_iota(jnp.int32, (1, RET_CHUNK), 1).astype(F32)
    tok_row = lax.broadcasted_iota(jnp.int32, (RET_CHUNK, 1), 0).astype(F32)
    chunk_len = jnp.full((1, 1), RET_CHUNK, F32)

    @pl.when(pl.program_id(1) == 0)
    def _():
        state[...] = jnp.zeros_like(state)

    diff = (lax.broadcasted_iota(jnp.int32, (RET_CHUNK, RET_CHUNK), 0)
            - lax.broadcasted_iota(jnp.int32, (RET_CHUNK, RET_CHUNK), 1)).astype(F32)
    dsum = jnp.where(diff >= 0, jnp.exp(lgf * jnp.maximum(diff, 0.0)), jnp.exp(lgb * jnp.maximum(-diff, 0.0)))
    cross_f = jnp.exp(lgf * (tok_row + 1.0))
    cross_b = jnp.exp(lgb * (RET_CHUNK - tok_row))
    zeta = jnp.exp(lgb * tok_lane)
    decay = jnp.exp(lgb * chunk_len)
    r = state[...]
    for c in reversed(range(RET_CHUNKS_PER_STEP)):
        tok = slice(c * RET_CHUNK, (c + 1) * RET_CHUNK)
        q = q_ref[tok, :]
        kt = kt_ref[:, tok]
        v = v_ref[tok, :]
        a = (_dot(q, kt) * dsum).astype(BF16)
        qf = q.astype(F32)
        lhs = jnp.concatenate([a, (qf * cross_f).astype(BF16), (qf * cross_b).astype(BF16)], axis=1)
        rhs = jnp.concatenate([v, rf_ref[c], r.astype(BF16)], axis=0)
        o_ref[tok, :] = _dot(lhs, rhs).astype(BF16)
        r = r * decay + _dot((kt.astype(F32) * zeta).astype(BF16), v)
    state[...] = r


def _retention(lg, q, kt, v, rf):
    s = q.shape[0]
    blk = lambda t: RET_STEPS - 1 - t
    return pl.pallas_call(
        _retention_kernel,
        grid=(RET_HEADS, RET_STEPS),
        in_specs=[pl.BlockSpec(memory_space=pltpu.SMEM),
                  pl.BlockSpec((RET_TOK, RET_DK), lambda h, t: (blk(t), h)),
                  pl.BlockSpec((RET_DK, RET_TOK), lambda h, t: (h, blk(t))),
                  pl.BlockSpec((RET_TOK, RET_DV), lambda h, t: (blk(t), h)),
                  pl.BlockSpec((RET_CHUNKS_PER_STEP, None, RET_DK, RET_DV), lambda h, t: (blk(t), h, 0, 0))],
        out_specs=pl.BlockSpec((RET_TOK, RET_DV), lambda h, t: (blk(t), h)),
        out_shape=jax.ShapeDtypeStruct((s, ODD_MIX), BF16),
        scratch_shapes=[pltpu.VMEM((RET_DK, RET_DV), F32)],
        compiler_params=_params("arbitrary", "arbitrary"),
        name="retention",
    )(lg, q, kt, v, rf)


def _even_qk_gain(qn_a, kn_a, qn_b, kn_b):
    scale = HEAD_DIM ** -0.5 * LOG2E
    gain = jnp.concatenate([jnp.tile(qn_a, A_Q_HEADS) * scale, jnp.tile(kn_a, A_KV_HEADS),
                            jnp.ones((A_KV_HEADS * HEAD_DIM,), F32),
                            jnp.tile(qn_b, B_HEADS) * scale, jnp.tile(kn_b, B_HEADS),
                            jnp.ones((B_HEADS * HEAD_DIM,), F32)]).astype(F32)
    return gain[None, :]


def kernel(x, attn_norm_e, w_in_e, q_norm_a, k_norm_a, sink_a, q_norm_b, k_norm_b, rpb_b, w_out_e,
           ret_norm_o, w_in_o, decay_fwd_o, decay_bwd_o, ret_gn_o, w_out_o,
           mlp_norm, w_mlp_in, w_mlp_out):
    b, s, d = x.shape
    xs = x.reshape(b * s, d)
    depth = mlp_norm.shape[0]
    for layer in range(depth):
        i = layer // 2
        g_mlp = mlp_norm[layer][None, :]
        if layer % 2 == 0:
            gain = _even_qk_gain(q_norm_a[i], k_norm_a[i], q_norm_b[i], k_norm_b[i])
            qa, ka2, va2, qb, kb, vb = _inproj_even(xs, attn_norm_e[i][None, :], w_in_e, i, gain)
            oa = _window_attn(sink_a[i].astype(F32) * LOG2E, qa, ka2, va2)
            ob = _na_attn(qb, kb, vb, _na_bias_rows(rpb_b[i].astype(F32) * LOG2E))
            xs = _outproj_mlp(xs, (oa, ob), None, w_out_e, i, g_mlp, w_mlp_in, w_mlp_out, layer,
                              tm=1024, name="outproj_mlp_even")
        else:
            lg = jnp.stack([jax.nn.log_sigmoid(decay_fwd_o[i].astype(F32)),
                            jax.nn.log_sigmoid(decay_bwd_o[i].astype(F32))])
            q, kt, v, gate, rf = _inproj_odd(lg[0], xs, ret_norm_o[i][None, :], w_in_o, i)
            y = _retention(lg, q, kt, v, rf)
            xs = _outproj_mlp(xs, (y,), (gate, ret_gn_o[i].astype(F32)[None, :]), w_out_o, i,
                              g_mlp, w_mlp_in, w_mlp_out, layer, name="outproj_mlp_odd")
    return xs.reshape(b, s, d)
```
